```python
import functools
import jax, jax.numpy as jnp
from jax import lax
import numpy as np

D_MODEL = 1024
BATCH = 4
SEQ = 4096
DEPTH = 1
DEC_BATCH = 128
DEC_SEQ = 1
PAST_LEN = 2048
PAGE_SIZE = 128

ATT_HEADS = 8
HEAD_DIM = 64
ATT_WIDTH = ATT_HEADS * HEAD_DIM
CONV_CH = D_MODEL - ATT_WIDTH
CONV_GROUPS = 8
DW_WIDTH = 31
N_IN = 2 * CONV_CH + 3 * ATT_WIDTH + ATT_HEADS
Q_BLOCK = 128
N_EXPERTS = 32
TOP_K = 4
D_EXPERT = D_MODEL
SWIGLU_LIMIT = 7.0
SWIGLU_ALPHA = 1.702
MOE_BLOCK = 128
PLE_DIM = 256
DEEPNORM_ALPHA = (2.0 * DEPTH) ** 0.25
DEEPNORM_BETA = (8.0 * DEPTH) ** -0.25
LN_EPS = 1e-5
NEG_INF = -1e30

kernel_name = "hymba_conformer_fox_moe_step"


def layer_norm(x, g, b):
    xf = x.astype(jnp.float32)
    mu = jnp.mean(xf, axis=-1, keepdims=True)
    var = jnp.mean(jnp.square(xf - mu), axis=-1, keepdims=True)
    return ((xf - mu) * lax.rsqrt(var + LN_EPS) * g.astype(jnp.float32) + b.astype(jnp.float32)).astype(x.dtype)


def conv_mixer(a, gate, hist, w_dw, b_dw, gn_gain, gn_bias):
    u = a * jax.nn.sigmoid(gate)
    full = jnp.concatenate([hist.astype(u.dtype), u], axis=1)
    y = lax.conv_general_dilated(full, w_dw[:, None, :].astype(u.dtype), (1,), 'VALID',
                                 dimension_numbers=('NWC', 'WIO', 'NWC'),
                                 feature_group_count=CONV_CH) + b_dw
    new_hist = full[:, -(DW_WIDTH - 1):]
    n, t, c = y.shape
    yg = y.reshape(n, t, CONV_GROUPS, c // CONV_GROUPS).astype(jnp.float32)
    mu = jnp.mean(yg, axis=-1, keepdims=True)
    var = jnp.mean(jnp.square(yg - mu), axis=-1, keepdims=True)
    yn = ((yg - mu) * lax.rsqrt(var + LN_EPS)).reshape(n, t, c) * gn_gain.astype(jnp.float32) + gn_bias.astype(jnp.float32)
    return jax.nn.silu(yn).astype(u.dtype), new_hist


def fox_prompt(q, k, v, lf):
    b, s, h, d = q.shape
    cum = jnp.cumsum(lf, axis=1)
    nb = s // Q_BLOCK
    qb = q.reshape(b, nb, Q_BLOCK, h, d).swapaxes(0, 1)
    cb = cum.reshape(b, nb, Q_BLOCK, h).swapaxes(0, 1)
    kbias = cum.transpose(0, 2, 1)[:, :, None, :]
    kpos = jnp.arange(s)
    scale = d ** -0.5

    def one_block(args):
        qi, ci, blk = args
        sc = jnp.einsum('bqhd,bkhd->bhqk', qi, k).astype(jnp.float32) * scale
        sc = sc + ci.transpose(0, 2, 1)[..., None] - kbias
        qpos = blk * Q_BLOCK + jnp.arange(Q_BLOCK)
        sc = jnp.where(kpos[None, :] <= qpos[:, None], sc, NEG_INF)
        p = jax.nn.softmax(sc, axis=-1)
        return jnp.einsum('bhqk,bkhd->bqhd', p.astype(v.dtype), v)

    o = lax.map(one_block, (qb, cb, jnp.arange(nb)))
    return o.swapaxes(0, 1).reshape(b, s, h * d)


def fox_sample(q, k, v, lf, k_past, v_past, lf_past):
    n, t, h, d = q.shape
    scale = d ** -0.5
    cum_past = jnp.cumsum(lf_past.astype(jnp.float32), axis=1)
    cum_new = cum_past[:, -1:] + jnp.cumsum(lf, axis=1)
    cq = cum_new.transpose(0, 2, 1)[..., None]
    sc_past = jnp.einsum('bqhd,bkhd->bhqk', q, k_past).astype(jnp.float32) * scale \
        + cq - cum_past.transpose(0, 2, 1)[:, :, None, :]
    sc_new = jnp.einsum('bqhd,bkhd->bhqk', q, k).astype(jnp.float32) * scale \
        + cq - cum_new.transpose(0, 2, 1)[:, :, None, :]
    causal = jnp.arange(t)[None, :] <= jnp.arange(t)[:, None]
    sc_new = jnp.where(causal, sc_new, NEG_INF)
    p = jax.nn.softmax(jnp.concatenate([sc_past, sc_new], axis=-1), axis=-1)
    past = k_past.shape[1]
    o = jnp.einsum('bhqk,bkhd->bqhd', p[..., :past].astype(v.dtype), v_past) \
        + jnp.einsum('bhqk,bkhd->bqhd', p[..., past:].astype(v.dtype), v)
    return o.reshape(n, t, h * d)


def moe(x, w_router, b_router, w_gate, b_gate, w_up, b_up, w_down, b_down):
    n_tok = x.shape[0]
    logits = (x @ w_router + b_router).astype(jnp.float32)
    top_val, top_idx = lax.top_k(logits, TOP_K)
    gates = jax.nn.softmax(top_val, axis=-1)
    n_assign = n_tok * TOP_K
    flat_e = top_idx.reshape(-1)
    flat_tok = jnp.repeat(jnp.arange(n_tok, dtype=jnp.int32), TOP_K)
    flat_g = gates.reshape(-1)
    order = jnp.argsort(flat_e)
    se, stok, sg = flat_e[order], flat_tok[order], flat_g[order]
    counts = jnp.bincount(flat_e, length=N_EXPERTS)
    starts = jnp.cumsum(counts) - counts
    padded = (counts + MOE_BLOCK - 1) // MOE_BLOCK * MOE_BLOCK
    pad_end = jnp.cumsum(padded)
    pad_start = pad_end - padded
    dest = pad_start[se] + jnp.arange(n_assign) - starts[se]
    n_blocks = -(-n_assign // MOE_BLOCK) + N_EXPERTS
    n_rows = n_blocks * MOE_BLOCK
    row_tok = jnp.zeros((n_rows,), jnp.int32).at[dest].set(stok)
    row_g = jnp.zeros((n_rows,), jnp.float32).at[dest].set(sg)
    block_e = jnp.minimum(jnp.searchsorted(pad_end, jnp.arange(n_blocks) * MOE_BLOCK, side='right'),
                          N_EXPERTS - 1)

    def expert_block(args):
        tok, e = args
        xb = x[tok]
        g = jnp.minimum(xb @ w_gate[e] + b_gate[e], SWIGLU_LIMIT)
        u = jnp.clip(xb @ w_up[e] + b_up[e], -SWIGLU_LIMIT, SWIGLU_LIMIT)
        hidden = (u + 1.0) * (g * jax.nn.sigmoid(SWIGLU_ALPHA * g))
        return hidden @ w_down[e] + b_down[e]

    y_rows = lax.map(expert_block, (row_tok.reshape(n_blocks, MOE_BLOCK), block_e))
    y_rows = y_rows.reshape(n_rows, -1) * row_g[:, None].astype(x.dtype)
    return jax.ops.segment_sum(y_rows, row_tok, num_segments=n_tok)


def trunk_layer(x, p, conv_hist, attend, lw):
    n, t, _ = x.shape
    z = x @ lw['w_in'] + lw['b_in']
    splits = [CONV_CH, 2 * CONV_CH, 2 * CONV_CH + ATT_WIDTH, 2 * CONV_CH + 2 * ATT_WIDTH,
              2 * CONV_CH + 3 * ATT_WIDTH]
    a, gate, q, k, v, f = jnp.split(z, splits, axis=-1)
    conv_out, new_hist = conv_mixer(a, gate, conv_hist, lw['w_dw'], lw['b_dw'], lw['gn_gain'], lw['gn_bias'])
    q = q.reshape(n, t, ATT_HEADS, HEAD_DIM)
    k = k.reshape(n, t, ATT_HEADS, HEAD_DIM)
    v = v.reshape(n, t, ATT_HEADS, HEAD_DIM)
    lf = jax.nn.log_sigmoid(f.astype(jnp.float32))
    att_out = attend(q, k, v, lf)
    mix = jnp.concatenate([conv_out, att_out.astype(conv_out.dtype)], axis=-1) @ lw['w_out']
    h = layer_norm(DEEPNORM_ALPHA * x + mix, lw['ln1_gain'], lw['ln1_bias'])
    ffn = moe(h.reshape(n * t, -1), lw['w_router'], lw['b_router'], lw['w_gate'], lw['b_gate'],
              lw['w_up'], lw['b_up'], lw['w_down'], lw['b_down']).reshape(n, t, -1)
    ple = jax.nn.sigmoid(h @ lw['w_ple_gate'] + lw['b_ple_gate']) * (p @ lw['w_ple_proj'])
    y = layer_norm(DEEPNORM_ALPHA * h + ffn + ple, lw['ln2_gain'], lw['ln2_bias'])
    return y, k, v, lf, new_hist


def setup_inputs(seed: int = 0) -> dict:
    key = jax.random.key(seed)
    ks = jax.random.split(key, 40)
    n_pages = PAST_LEN // PAGE_SIZE
    n_pool = (DEC_BATCH * n_pages * 5) // 4

    def nrm(k, shape, scale):
        return jax.random.normal(k, shape, jnp.float32) * scale

    x_prompt = nrm(ks[0], (BATCH, SEQ, D_MODEL), 1.0)
    x_sample = nrm(ks[1], (DEC_BATCH, DEC_SEQ, D_MODEL), 1.0)
    cache_k = nrm(ks[2], (DEPTH, n_pool, PAGE_SIZE, ATT_HEADS, HEAD_DIM), 1.0)
    cache_v = nrm(ks[3], (DEPTH, n_pool, PAGE_SIZE, ATT_HEADS, HEAD_DIM), 1.0)
    cache_logf = jax.nn.log_sigmoid(jax.random.uniform(ks[4], (DEPTH, n_pool, PAGE_SIZE, ATT_HEADS),
                                                       jnp.float32, 1.0, 5.0))
    state_conv = nrm(ks[5], (DEPTH, DEC_BATCH, DW_WIDTH - 1, CONV_CH), 0.5)
    page_table = jax.random.permutation(ks[6], n_pool)[:DEC_BATCH * n_pages].reshape(
        DEC_BATCH, n_pages).astype(jnp.int32)
    p_prompt = nrm(ks[7], (DEPTH, BATCH, SEQ, PLE_DIM), 1.0)
    p_sample = nrm(ks[8], (DEPTH, DEC_BATCH, DEC_SEQ, PLE_DIM), 1.0)

    b_in = nrm(ks[10], (DEPTH, N_IN), 0.02)
    b_in = b_in.at[:, -ATT_HEADS:].add(jax.random.uniform(ks[11], (DEPTH, ATT_HEADS), jnp.float32, 1.0, 4.0))
    return {
        "x_prompt": x_prompt, "x_sample": x_sample,
        "cache_k": cache_k, "cache_v": cache_v, "cache_logf": cache_logf, "state_conv": state_conv,
        "page_table": page_table, "p_prompt": p_prompt, "p_sample": p_sample,
        "w_in": nrm(ks[9], (DEPTH, D_MODEL, N_IN), D_MODEL ** -0.5),
        "b_in": b_in,
        "w_dw": nrm(ks[12], (DEPTH, DW_WIDTH, CONV_CH), DW_WIDTH ** -0.5),
        "b_dw": nrm(ks[13], (DEPTH, CONV_CH), 0.02),
        "gn_gain": 1.0 + nrm(ks[14], (DEPTH, CONV_CH), 0.05),
        "gn_bias": nrm(ks[15], (DEPTH, CONV_CH), 0.02),
        "w_out": nrm(ks[16], (DEPTH, D_MODEL, D_MODEL), D_MODEL ** -0.5 * DEEPNORM_BETA),
        "ln1_gain": 1.0 + nrm(ks[17], (DEPTH, D_MODEL), 0.05),
        "ln1_bias": nrm(ks[18], (DEPTH, D_MODEL), 0.02),
        "w_router": nrm(ks[19], (DEPTH, D_MODEL, N_EXPERTS), D_MODEL ** -0.5),
        "b_router": nrm(ks[20], (DEPTH, N_EXPERTS), 0.01),
        "w_gate": nrm(ks[21], (DEPTH, N_EXPERTS, D_MODEL, D_EXPERT), D_MODEL ** -0.5),
        "b_gate": nrm(ks[22], (DEPTH, N_EXPERTS, D_EXPERT), 0.02),
        "w_up": nrm(ks[23], (DEPTH, N_EXPERTS, D_MODEL, D_EXPERT), D_MODEL ** -0.5),
        "b_up": nrm(ks[24], (DEPTH, N_EXPERTS, D_EXPERT), 0.02),
        "w_down": nrm(ks[25], (DEPTH, N_EXPERTS, D_EXPERT, D_MODEL), D_EXPERT ** -0.5 * DEEPNORM_BETA),
        "b_down": nrm(ks[26], (DEPTH, N_EXPERTS, D_MODEL), 0.02),
        "w_ple_gate": nrm(ks[27], (DEPTH, D_MODEL, D_MODEL), D_MODEL ** -0.5),
        "b_ple_gate": nrm(ks[28], (DEPTH, D_MODEL), 0.02),
        "w_ple_proj": nrm(ks[29], (DEPTH, PLE_DIM, D_MODEL), PLE_DIM ** -0.5 * DEEPNORM_BETA),
        "ln2_gain": 1.0 + nrm(ks[30], (DEPTH, D_MODEL), 0.05),
        "ln2_bias": nrm(ks[31], (DEPTH, D_MODEL), 0.02),
    }


def reference(x_prompt, x_sample, cache_k, cache_v, cache_logf, state_conv, page_table, p_prompt, p_sample,
              w_in, b_in, w_dw, b_dw, gn_gain, gn_bias, w_out, ln1_gain, ln1_bias,
              w_router, b_router, w_gate, b_gate, w_up, b_up, w_down, b_down,
              w_ple_gate, b_ple_gate, w_ple_proj, ln2_gain, ln2_bias):
    n_prompt = x_prompt.shape[0]
    n_dec = page_table.shape[0]
    past_len = page_table.shape[1] * cache_k.shape[2]
    y_p, y_s = x_prompt, x_sample
    kp_l, vp_l, lfp_l, cp_l, ks_l, vs_l, lfs_l, cs_l = [], [], [], [], [], [], [], []
    for i in range(DEPTH):
        lw = dict(w_in=w_in[i], b_in=b_in[i], w_dw=w_dw[i], b_dw=b_dw[i], gn_gain=gn_gain[i], gn_bias=gn_bias[i],
                  w_out=w_out[i], ln1_gain=ln1_gain[i], ln1_bias=ln1_bias[i],
                  w_router=w_router[i], b_router=b_router[i], w_gate=w_gate[i], b_gate=b_gate[i],
                  w_up=w_up[i], b_up=b_up[i], w_down=w_down[i], b_down=b_down[i],
                  w_ple_gate=w_ple_gate[i], b_ple_gate=b_ple_gate[i], w_ple_proj=w_ple_proj[i],
                  ln2_gain=ln2_gain[i], ln2_bias=ln2_bias[i])
        hist0 = jnp.zeros((n_prompt, DW_WIDTH - 1, CONV_CH), x_prompt.dtype)
        y_p, kp, vp, lfp, cp = trunk_layer(y_p, p_prompt[i], hist0, fox_prompt, lw)
        k_past = cache_k[i, page_table].reshape(n_dec, past_len, ATT_HEADS, HEAD_DIM)
        v_past = cache_v[i, page_table].reshape(n_dec, past_len, ATT_HEADS, HEAD_DIM)
        lf_past = cache_logf[i, page_table].reshape(n_dec, past_len, ATT_HEADS)
        attend_s = functools.partial(fox_sample, k_past=k_past, v_past=v_past, lf_past=lf_past)
        y_s, ks, vs, lfs, cs = trunk_layer(y_s, p_sample[i], state_conv[i], attend_s, lw)
        kp_l.append(kp); vp_l.append(vp); lfp_l.append(lfp); cp_l.append(cp)
        ks_l.append(ks); vs_l.append(vs); lfs_l.append(lfs); cs_l.append(cs)
    k_prompt, v_prompt = jnp.stack(kp_l), jnp.stack(vp_l)
    logf_prompt, conv_prompt = jnp.stack(lfp_l), jnp.stack(cp_l)
    k_sample, v_sample = jnp.stack(ks_l), jnp.stack(vs_l)
    logf_sample, conv_sample = jnp.stack(lfs_l), jnp.stack(cs_l)
    return (y_p, y_s, k_prompt, v_prompt, logf_prompt, conv_prompt, k_sample, v_sample, logf_sample, conv_sample)
```

```python
import functools

import jax
import jax.numpy as jnp
from jax import lax
from jax.experimental import pallas as pl
from jax.experimental.pallas import tpu as pltpu

F32 = jnp.float32
BF16 = jnp.bfloat16
I32 = jnp.int32

ATT_HEADS = 8
HEAD_DIM = 64
CONV_GROUPS = 8
DW_WIDTH = 31
N_EXPERTS = 32
TOP_K = 4
SWIGLU_LIMIT = 7.0
SWIGLU_ALPHA = 1.702
LN_EPS = 1e-5
NEG_INF = -1e30

LANES = 128
SUBLANES = 8
CONV_HALO = 32
MOE_BLOCK = 256
VMEM_LIMIT = 56 * 1024 * 1024


def _pick(n, cands):
    for c in cands:
        if n % c == 0:
            return c
    raise ValueError(f"no block size in {cands} divides {n}")


def _params(sem, vmem=VMEM_LIMIT):
    return pltpu.CompilerParams(dimension_semantics=sem, vmem_limit_bytes=vmem)


def _sigmoid(x):
    return 1.0 / (1.0 + jnp.exp(-x))


def _log_sigmoid(x):
    return jnp.minimum(x, 0.0) - jnp.log1p(jnp.exp(-jnp.abs(x)))


def _dot(a, b):
    return jnp.dot(a, b, preferred_element_type=F32)


def _dot_nt(a, b):
    return lax.dot_general(a, b, (((1,), (1,)), ((), ())), preferred_element_type=F32)


def _split3(x):
    x1 = x.astype(BF16)
    r1 = x - x1.astype(F32)
    x2 = r1.astype(BF16)
    x3 = (r1 - x2.astype(F32)).astype(BF16)
    return x1, x2, x3


def _dot3(x, m):
    x1, x2, x3 = _split3(x)
    return _dot(x1, m) + _dot(x2, m) + _dot(x3, m)


def _dot3_left(m, x):
    x1, x2, x3 = _split3(x)
    return _dot(m, x1) + _dot(m, x2) + _dot(m, x3)


def _layer_norm(x, g, b):
    mu = jnp.mean(x, axis=-1, keepdims=True)
    d = x - mu
    var = jnp.mean(d * d, axis=-1, keepdims=True)
    return d * lax.rsqrt(var + LN_EPS) * g + b


def _inproj_body(x_ref, w_ref, b_ref, wft_ref, bft_ref,
                 u_ref, q_ref, k_ref, v_ref, kb_ref, vb_ref, lft_ref, *, cc, aw, scale):
    x = x_ref[...].astype(BF16)

    def seg(lo, hi):
        return _dot(x, w_ref[:, lo:hi]) + b_ref[:, lo:hi]

    a = seg(0, cc)
    g = seg(cc, 2 * cc)
    u_ref[...] = a * _sigmoid(g)
    o = 2 * cc
    q_ref[...] = (seg(o, o + aw) * scale).astype(BF16)
    k = seg(o + aw, o + 2 * aw)
    k_ref[...] = k
    kb_ref[...] = k.astype(BF16)
    v = seg(o + 2 * aw, o + 3 * aw)
    v_ref[...] = v
    vb_ref[...] = v.astype(BF16)
    ft = _dot_nt(wft_ref[...], x) + bft_ref[...]
    lft_ref[...] = _log_sigmoid(ft)


def _in_proj(x, w_b, b_row, wft_b, bft_col, cc, aw):
    n, d = x.shape
    tm = _pick(n, (512, 384, 256, 128, 64, 32, 16, 8))
    nw = w_b.shape[1]
    heads = wft_b.shape[0]
    row = lambda i: (i, 0)
    const = lambda i: (0, 0)
    outs = (
        jax.ShapeDtypeStruct((n, cc), F32),
        jax.ShapeDtypeStruct((n, aw), BF16),
        jax.ShapeDtypeStruct((n, aw), F32),
        jax.ShapeDtypeStruct((n, aw), F32),
        jax.ShapeDtypeStruct((n, aw), BF16),
        jax.ShapeDtypeStruct((n, aw), BF16),
        jax.ShapeDtypeStruct((heads, n), F32),
    )
    return pl.pallas_call(
        functools.partial(_inproj_body, cc=cc, aw=aw, scale=HEAD_DIM ** -0.5),
        grid=(n // tm,),
        in_specs=[
            pl.BlockSpec((tm, d), row),
            pl.BlockSpec((d, nw), const),
            pl.BlockSpec((1, nw), const),
            pl.BlockSpec((heads, d), const),
            pl.BlockSpec((heads, 1), const),
        ],
        out_specs=(
            pl.BlockSpec((tm, cc), row),
            pl.BlockSpec((tm, aw), row),
            pl.BlockSpec((tm, aw), row),
            pl.BlockSpec((tm, aw), row),
            pl.BlockSpec((tm, aw), row),
            pl.BlockSpec((tm, aw), row),
            pl.BlockSpec((heads, tm), lambda i: (0, i)),
        ),
        out_shape=outs,
        compiler_params=_params(("parallel",)),
        name="in_proj",
    )(x, w_b, b_row, wft_b, bft_col)


def _group_norm_silu(y, m, gg, gb):
    mean = _dot3(y, m)
    d = y - mean
    var = _dot3(d * d, m)
    yn = d * lax.rsqrt(var + LN_EPS) * gg + gb
    return yn * _sigmoid(yn)


def _conv_body(u_ref, halo_ref, w_ref, b_ref, gg_ref, gb_ref, m_ref, o_ref, ext_ref, y_ref, *, ts, rc):
    i = pl.program_id(1)
    halo = halo_ref[...]
    ext_ref[0:CONV_HALO, :] = jnp.where(i > 0, halo, jnp.zeros_like(halo))
    ext_ref[CONV_HALO:, :] = u_ref[...]
    off = CONV_HALO - (DW_WIDTH - 1)
    c = u_ref.shape[1]
    for r0 in range(0, ts, rc):
        acc = jnp.broadcast_to(b_ref[...], (rc, c))
        for j in range(DW_WIDTH):
            s = r0 + off + j
            acc = acc + w_ref[j:j + 1, :] * ext_ref[s:s + rc, :]
        y_ref[r0:r0 + rc, :] = acc
    o_ref[...] = _group_norm_silu(y_ref[...], m_ref[...], gg_ref[...], gb_ref[...]).astype(BF16)


def _conv_prompt(u, batch, seq, w_dw, b_row, gg_row, gb_row, gmat):
    n, c = u.shape
    ts = _pick(seq, (512, 256, 128, 64, 32))
    rc = min(ts, 64)
    nt = seq // ts
    hb = ts // CONV_HALO
    const = lambda b, i: (0, 0)
    return pl.pallas_call(
        functools.partial(_conv_body, ts=ts, rc=rc),
        grid=(batch, nt),
        in_specs=[
            pl.BlockSpec((ts, c), lambda b, i: (b * nt + i, 0)),
            pl.BlockSpec((CONV_HALO, c), lambda b, i: (jnp.maximum((b * nt + i) * hb - 1, 0), 0)),
            pl.BlockSpec((DW_WIDTH, c), const),
            pl.BlockSpec((1, c), const),
            pl.BlockSpec((1, c), const),
            pl.BlockSpec((1, c), const),
            pl.BlockSpec((c, c), const),
        ],
        out_specs=pl.BlockSpec((ts, c), lambda b, i: (b * nt + i, 0)),
        out_shape=jax.ShapeDtypeStruct((n, c), BF16),
        scratch_shapes=[pltpu.VMEM((CONV_HALO + ts, c), F32), pltpu.VMEM((ts, c), F32)],
        compiler_params=_params(("parallel", "parallel")),
        name="conv_prompt",
    )(u, u, w_dw, b_row, gg_row, gb_row, gmat)


def _conv_sample_body(hist_ref, u_ref, w_ref, b_ref, gg_ref, gb_ref, m_ref, o_ref):
    acc = b_ref[...] + w_ref[DW_WIDTH - 1:DW_WIDTH, :] * u_ref[...]
    for j in range(DW_WIDTH - 1):
        acc = acc + w_ref[j:j + 1, :] * hist_ref[j]
    o_ref[...] = _group_norm_silu(acc, m_ref[...], gg_ref[...], gb_ref[...]).astype(BF16)


def _conv_sample(hist_t, u, w_dw, b_row, gg_row, gb_row, gmat):
    n, c = u.shape
    full2 = lambda i: (0, 0)
    return pl.pallas_call(
        _conv_sample_body,
        grid=(1,),
        in_specs=[
            pl.BlockSpec(hist_t.shape, lambda i: (0, 0, 0)),
            pl.BlockSpec((n, c), full2),
            pl.BlockSpec((DW_WIDTH, c), full2),
            pl.BlockSpec((1, c), full2),
            pl.BlockSpec((1, c), full2),
            pl.BlockSpec((1, c), full2),
            pl.BlockSpec((c, c), full2),
        ],
        out_specs=pl.BlockSpec((n, c), full2),
        out_shape=jax.ShapeDtypeStruct((n, c), BF16),
        compiler_params=_params(("arbitrary",)),
        name="conv_sample",
    )(hist_t, u, w_dw, b_row, gg_row, gb_row, gmat)


def _cumsum_body(lf_ref, tri_ref, o_ref, carry_ref):
    @pl.when(pl.program_id(1) == 0)
    def _():
        carry_ref[...] = jnp.zeros_like(carry_ref)

    cl = lf_ref.shape[1]
    cum = _dot3(lf_ref[...], tri_ref[...]) + carry_ref[:, 0:1]
    o_ref[...] = -cum
    carry_ref[...] = jnp.broadcast_to(cum[:, cl - 1:cl], carry_ref.shape)


def _neg_cumsum(lft, batch, seq):
    heads, n = lft.shape
    cl = _pick(seq, (512, 256, 128))
    nc = seq // cl
    tri = jnp.triu(jnp.ones((cl, cl), F32)).astype(BF16)
    return pl.pallas_call(
        _cumsum_body,
        grid=(batch, nc),
        in_specs=[pl.BlockSpec((heads, cl), lambda b, c: (0, b * nc + c)),
                  pl.BlockSpec((cl, cl), lambda b, c: (0, 0))],
        out_specs=pl.BlockSpec((heads, cl), lambda b, c: (0, b * nc + c)),
        out_shape=jax.ShapeDtypeStruct((heads, n), F32),
        scratch_shapes=[pltpu.VMEM((heads, LANES), F32)],
        compiler_params=_params(("parallel", "arbitrary")),
        name="forget_cumsum",
    )(lft, tri)


def _att_body(qi_ref, ki_ref, q_ref, k_ref, v_ref, nd_ref, o_ref, m_ref, l_ref, acc_ref, *, hd):
    step = pl.program_id(2)
    qi = qi_ref[step]
    ki = ki_ref[step]
    tq = q_ref.shape[0]
    tk = k_ref.shape[0]
    is_a = lax.broadcasted_iota(I32, (1, LANES), 1) < hd

    @pl.when(ki == 0)
    def _():
        m_ref[...] = jnp.full(m_ref.shape, NEG_INF, F32)
        l_ref[...] = jnp.zeros_like(l_ref)
        acc_ref[...] = jnp.zeros_like(acc_ref)

    def process(diagonal):
        q = q_ref[...]
        k = k_ref[...]
        v = v_ref[...]
        nd = nd_ref[0]
        zero = jnp.zeros_like(q)
        pvs = []
        alphas = []
        for h in range(2):
            qh = jnp.where(is_a, q, zero) if h == 0 else jnp.where(is_a, zero, q)
            s = _dot_nt(qh, k) + nd[h:h + 1, :]
            if diagonal:
                row = lax.broadcasted_iota(I32, (tq, tk), 0)
                col = lax.broadcasted_iota(I32, (tq, tk), 1)
                s = jnp.where(col <= row, s, NEG_INF)
            m_old = m_ref[h]
            m_new = jnp.maximum(m_old, jnp.max(s, axis=1, keepdims=True))
            alpha = jnp.exp(m_old - m_new)
            p = jnp.exp(s - m_new)
            l_ref[h] = alpha * l_ref[h] + jnp.sum(p, axis=1, keepdims=True)
            m_ref[h] = m_new
            pvs.append(_dot(p.astype(BF16), v))
            alphas.append(alpha)
        acc_ref[...] = (acc_ref[...] * jnp.where(is_a, alphas[0], alphas[1])
                        + jnp.where(is_a, pvs[0], pvs[1]))

    @pl.when(ki < qi)
    def _():
        process(False)

    @pl.when(ki == qi)
    def _():
        process(True)
        o_ref[...] = (acc_ref[...] / jnp.where(is_a, l_ref[0], l_ref[1])).astype(BF16)


def _attention_prompt(q_b, k_b, v_b, negd, batch, seq):
    n, aw = q_b.shape
    pairs = aw // LANES
    t = _pick(seq, (512, 256, 128))
    nq = seq // t
    steps = [(a, b) for a in range(nq) for b in range(a + 1)]
    qi_tab = jnp.asarray([s[0] for s in steps], I32)
    ki_tab = jnp.asarray([s[1] for s in steps], I32)
    grid_spec = pltpu.PrefetchScalarGridSpec(
        num_scalar_prefetch=2,
        grid=(batch, pairs, len(steps)),
        in_specs=[
            pl.BlockSpec((t, LANES), lambda b, hp, s, qt, kt: (b * nq + qt[s], hp)),
            pl.BlockSpec((t, LANES), lambda b, hp, s, qt, kt: (b * nq + kt[s], hp)),
            pl.BlockSpec((t, LANES), lambda b, hp, s, qt, kt: (b * nq + kt[s], hp)),
            pl.BlockSpec((1, 2, t), lambda b, hp, s, qt, kt: (hp, 0, b * nq + kt[s])),
        ],
        out_specs=pl.BlockSpec((t, LANES), lambda b, hp, s, qt, kt: (b * nq + qt[s], hp)),
        scratch_shapes=[pltpu.VMEM((2, t, 1), F32), pltpu.VMEM((2, t, 1), F32), pltpu.VMEM((t, LANES), F32)],
    )
    return pl.pallas_call(
        functools.partial(_att_body, hd=HEAD_DIM),
        grid_spec=grid_spec,
        out_shape=jax.ShapeDtypeStruct((n, aw), BF16),
        compiler_params=_params(("parallel", "parallel", "arbitrary")),
        name="attention_prompt",
    )(qi_tab, ki_tab, q_b, k_b, v_b, negd)


def _satt_body(pt_ref, *refs, n_pages, heads, hd):
    k_refs = refs[:n_pages]
    v_refs = refs[n_pages:2 * n_pages]
    lf_refs = refs[2 * n_pages:3 * n_pages]
    q_ref, kn_ref, vn_ref, lfn_ref, tris_ref, ones_ref, after_ref, o_ref = refs[3 * n_pages:]
    aw = heads * hd
    q = q_ref[0]
    diag = (lax.broadcasted_iota(I32, (heads, aw), 0)
            == lax.broadcasted_iota(I32, (heads, aw), 1) // hd)
    qbd_f = jnp.where(diag, jnp.broadcast_to(q.astype(F32), (heads, aw)), 0.0)
    qbd = qbd_f.astype(BF16)

    s_past = jnp.concatenate([_dot_nt(qbd, k_refs[j][0].astype(BF16)) for j in range(n_pages)], axis=0)
    lf = jnp.concatenate([lf_refs[j][0] for j in range(n_pages)], axis=0)
    within = _dot3(lf, tris_ref[...])
    totals = _dot3(lf, ones_ref[...])
    later = _dot3_left(after_ref[...], totals)
    lfn = lfn_ref[0]
    lfn_rows = jnp.concatenate([lfn] * n_pages, axis=0)
    s_past = s_past + (within + later + lfn_rows)
    s_new = jnp.sum(qbd_f * kn_ref[0], axis=1, keepdims=True)

    row_max = jnp.max(s_past, axis=1, keepdims=True)
    m = s_new
    for j in range(n_pages):
        m = jnp.maximum(m, row_max[j * heads:(j + 1) * heads])
    p = jnp.exp(s_past - jnp.concatenate([m] * n_pages, axis=0))
    p_new = jnp.exp(s_new - m)
    row_sum = jnp.sum(p, axis=1, keepdims=True)
    l = p_new
    o = p_new * vn_ref[0]
    for j in range(n_pages):
        l = l + row_sum[j * heads:(j + 1) * heads]
        o = o + _dot(p[j * heads:(j + 1) * heads].astype(BF16), v_refs[j][0].astype(BF16))
    o = o / l
    o_ref[0] = jnp.sum(jnp.where(diag, o, 0.0), axis=0, keepdims=True).astype(BF16)


def _attention_sample(q_b, k_new, v_new, lf_new, cache_k, cache_v, cache_lf_t, page_table):
    n, aw = q_b.shape
    n_pages = page_table.shape[1]
    page = cache_k.shape[1]
    heads = cache_lf_t.shape[1]
    rows = n_pages * heads
    tris = jnp.tril(jnp.ones((page, page), F32), -1).astype(BF16)
    ones = jnp.ones((page, page), BF16)
    r = jnp.arange(rows)
    after = ((r[None, :] % heads == r[:, None] % heads)
             & (r[None, :] // heads > r[:, None] // heads)).astype(BF16)

    def page_spec(j, shape):
        return pl.BlockSpec((1,) + shape, lambda i, pt, j=j: (pt[i, j], 0, 0))

    per_seq = lambda shape: pl.BlockSpec((1,) + shape, lambda i, pt: (i, 0, 0))
    const = lambda shape: pl.BlockSpec(shape, lambda i, pt: (0, 0))
    grid_spec = pltpu.PrefetchScalarGridSpec(
        num_scalar_prefetch=1,
        grid=(n,),
        in_specs=([page_spec(j, (page, aw)) for j in range(n_pages)]
                  + [page_spec(j, (page, aw)) for j in range(n_pages)]
                  + [page_spec(j, (heads, page)) for j in range(n_pages)]
                  + [per_seq((1, aw)), per_seq((1, aw)), per_seq((1, aw)), per_seq((heads, 1)),
                     const((page, page)), const((page, page)), const((rows, rows))]),
        out_specs=per_seq((1, aw)),
    )
    out = pl.pallas_call(
        functools.partial(_satt_body, n_pages=n_pages, heads=heads, hd=aw // heads),
        grid_spec=grid_spec,
        out_shape=jax.ShapeDtypeStruct((n, 1, aw), BF16),
        compiler_params=_params(("arbitrary",)),
        name="attention_sample",
    )(page_table, *([cache_k] * n_pages), *([cache_v] * n_pages), *([cache_lf_t] * n_pages),
      q_b[:, None, :], k_new[:, None, :], v_new[:, None, :], lf_new[:, :, None], tris, ones, after)
    return out[:, 0, :]


def _mix_body(x_ref, c_ref, a_ref, p_ref, wo_ref, g1_ref, b1_ref, wrh_ref, wrl_ref, br_ref,
              wpg_ref, bpg_ref, wpp_ref, h_ref, base_ref, idx_ref, gate_ref, cnt_ref, *, alpha, cc):
    mix = _dot(c_ref[...], wo_ref[:cc, :]) + _dot(a_ref[...], wo_ref[cc:, :])
    h = _layer_norm(alpha * x_ref[...] + mix, g1_ref[...], b1_ref[...])
    h_ref[...] = h
    hb = h.astype(BF16)
    hl = (h - hb.astype(F32)).astype(BF16)
    wrh = wrh_ref[...]
    logits = _dot(hb, wrh) + _dot(hl, wrh) + _dot(hb, wrl_ref[...]) + br_ref[...]

    tm = logits.shape[0]
    lane = lax.broadcasted_iota(I32, (tm, LANES), 1).astype(F32)
    vals, idxs = [], []
    l = logits
    for _ in range(TOP_K):
        m = jnp.max(l, axis=1, keepdims=True)
        ix = jnp.min(jnp.where(l == m, lane, float(LANES)), axis=1, keepdims=True)
        vals.append(m)
        idxs.append(ix)
        l = jnp.where(lane == ix, -jnp.inf, l)
    es = [jnp.exp(v - vals[0]) for v in vals]
    den = es[0] + es[1] + es[2] + es[3]
    idx_w = jnp.zeros((tm, LANES), F32)
    gate_w = jnp.zeros((tm, LANES), F32)
    sel = jnp.zeros((tm, LANES), F32)
    for r in range(TOP_K):
        idx_w = jnp.where(lane == float(r), idxs[r], idx_w)
        gate_w = jnp.where(lane == float(r), es[r] / den, gate_w)
        sel = sel + (lane == idxs[r]).astype(F32)
    idx_ref[...] = idx_w
    gate_ref[...] = gate_w

    @pl.when(pl.program_id(0) == 0)
    def _():
        cnt_ref[...] = jnp.zeros_like(cnt_ref)

    cnt_ref[...] += jnp.sum(sel, axis=0, keepdims=True)

    ple = _sigmoid(_dot(hb, wpg_ref[...]) + bpg_ref[...]) * _dot(p_ref[...].astype(BF16), wpp_ref[...])
    base_ref[...] = alpha * h + ple


def _mix(x, conv_b, att_b, p, wo_b, g1, b1, wr_hi, wr_lo, br_row, wpg_b, bpg, wpp_b, alpha):
    n, d = x.shape
    cc = conv_b.shape[1]
    aw = att_b.shape[1]
    pd = p.shape[1]
    tm = _pick(n, (256, 128, 64, 32, 16, 8))
    row = lambda i: (i, 0)
    const = lambda i: (0, 0)
    outs = (
        jax.ShapeDtypeStruct((n, d), F32),
        jax.ShapeDtypeStruct((n, d), F32),
        jax.ShapeDtypeStruct((n, LANES), F32),
        jax.ShapeDtypeStruct((n, LANES), F32),
        jax.ShapeDtypeStruct((1, LANES), F32),
    )
    return pl.pallas_call(
        functools.partial(_mix_body, alpha=alpha, cc=cc),
        grid=(n // tm,),
        in_specs=[
            pl.BlockSpec((tm, d), row), pl.BlockSpec((tm, cc), row), pl.BlockSpec((tm, aw), row),
            pl.BlockSpec((tm, pd), row),
            pl.BlockSpec((d, d), const), pl.BlockSpec((1, d), const), pl.BlockSpec((1, d), const),
            pl.BlockSpec((d, LANES), const), pl.BlockSpec((d, LANES), const), pl.BlockSpec((1, LANES), const),
            pl.BlockSpec((d, d), const), pl.BlockSpec((1, d), const), pl.BlockSpec((pd, d), const),
        ],
        out_specs=(
            pl.BlockSpec((tm, d), row), pl.BlockSpec((tm, d), row),
            pl.BlockSpec((tm, LANES), row), pl.BlockSpec((tm, LANES), row),
            pl.BlockSpec((1, LANES), const),
        ),
        out_shape=outs,
        compiler_params=_params(("arbitrary",)),
        name="out_proj_router",
    )(x, conv_b, att_b, p, wo_b, g1, b1, wr_hi, wr_lo, br_row, wpg_b, bpg, wpp_b)


def _positions_body(idx_ref, start_ref, tri_ref, o_ref, carry_ref):
    @pl.when(pl.program_id(0) == 0)
    def _():
        carry_ref[...] = jnp.zeros_like(carry_ref)

    ids = idx_ref[...]
    tm = ids.shape[0]
    lane = lax.broadcasted_iota(I32, (tm, LANES), 1).astype(F32)
    hot = [lane == ids[:, r:r + 1] for r in range(TOP_K)]
    sel = jnp.zeros((tm, LANES), F32)
    for r in range(TOP_K):
        sel = sel + hot[r].astype(F32)
    rank = _dot(tri_ref[...], sel.astype(BF16)) + carry_ref[...]
    pos = rank + start_ref[...]
    out = jnp.zeros((tm, LANES), F32)
    for r in range(TOP_K):
        d = jnp.sum(jnp.where(hot[r], pos, 0.0), axis=1, keepdims=True)
        out = jnp.where(lane == float(r), d, out)
    o_ref[...] = out.astype(I32)
    carry_ref[...] += jnp.sum(sel, axis=0, keepdims=True)


def _positions(idx_w, start_row):
    n = idx_w.shape[0]
    tm = _pick(n, (512, 384, 256, 128, 64, 32, 16, 8))
    tri = jnp.tril(jnp.ones((tm, tm), F32), -1).astype(BF16)
    return pl.pallas_call(
        _positions_body,
        grid=(n // tm,),
        in_specs=[pl.BlockSpec((tm, LANES), lambda i: (i, 0)),
                  pl.BlockSpec((1, LANES), lambda i: (0, 0)),
                  pl.BlockSpec((tm, tm), lambda i: (0, 0))],
        out_specs=pl.BlockSpec((tm, LANES), lambda i: (i, 0)),
        out_shape=jax.ShapeDtypeStruct((n, LANES), I32),
        scratch_shapes=[pltpu.VMEM((1, LANES), F32)],
        compiler_params=_params(("arbitrary",)),
        name="route_positions",
    )(idx_w, start_row, tri)


def _dispatch_body(dest_ref, h_ref, xs_in_ref, xs_ref, sem, *, tb):
    del xs_in_ref
    base = pl.program_id(0) * tb

    def row_copy(t, d):
        return pltpu.make_async_copy(h_ref.at[pl.ds(base + t, 1)], xs_ref.at[pl.ds(d, 1)], sem)

    def issue(t, c):
        for r in range(TOP_K):
            row_copy(t, dest_ref[0, 0, t * TOP_K + r]).start()
        return c

    lax.fori_loop(0, tb, issue, 0)

    def drain(t, c):
        for r in range(TOP_K):
            row_copy(t, dest_ref[0, 0, t * TOP_K + r]).wait()
        return c

    lax.fori_loop(0, tb, drain, 0)


def _dispatch(h, dest, xs):
    n, d = h.shape
    tb = _pick(n, (256, 128, 64, 32, 16, 8))
    dest3 = dest.reshape(n // tb, 1, tb * TOP_K)
    return pl.pallas_call(
        functools.partial(_dispatch_body, tb=tb),
        grid=(n // tb,),
        in_specs=[
            pl.BlockSpec((1, 1, tb * TOP_K), lambda i: (i, 0, 0), memory_space=pltpu.SMEM),
            pl.BlockSpec(memory_space=pl.ANY),
            pl.BlockSpec(memory_space=pl.ANY),
        ],
        out_specs=pl.BlockSpec(memory_space=pl.ANY),
        out_shape=jax.ShapeDtypeStruct(xs.shape, xs.dtype),
        scratch_shapes=[pltpu.SemaphoreType.DMA],
        input_output_aliases={2: 0},
        compiler_params=_params(("arbitrary",)),
        name="moe_dispatch",
    )(dest3, h, xs)


def _expert_body(be_ref, nu_ref, x_ref, wg_ref, bg_ref, wu_ref, bu_ref, wd_ref, bd_ref, o_ref,
                 wgb_ref, wub_ref, wdb_ref):
    i = pl.program_id(0)

    @pl.when(i < nu_ref[0])
    def _():
        @pl.when((i == 0) | (be_ref[i] != be_ref[jnp.maximum(i - 1, 0)]))
        def _():
            wgb_ref[...] = wg_ref[0].astype(BF16)
            wub_ref[...] = wu_ref[0].astype(BF16)
            wdb_ref[...] = wd_ref[0].astype(BF16)

        x = x_ref[...].astype(BF16)
        g = jnp.minimum(_dot(x, wgb_ref[...]) + bg_ref[0], SWIGLU_LIMIT)
        u = jnp.clip(_dot(x, wub_ref[...]) + bu_ref[0], -SWIGLU_LIMIT, SWIGLU_LIMIT)
        hidden = (u + 1.0) * (g * _sigmoid(SWIGLU_ALPHA * g))
        o_ref[...] = _dot(hidden.astype(BF16), wdb_ref[...]) + bd_ref[0]

    @pl.when(i >= nu_ref[0])
    def _():
        o_ref[...] = jnp.zeros_like(o_ref)


def _experts(xs, block_e, n_used, w_gate, b_gate, w_up, b_up, w_down, b_down):
    rows, d = xs.shape
    n_e, _, de = w_gate.shape
    n_blocks = rows // MOE_BLOCK
    last = lambda i, be, nu: (jnp.minimum(i, nu[0] - 1), 0)
    wsel = lambda i, be, nu: (be[i], 0, 0)
    grid_spec = pltpu.PrefetchScalarGridSpec(
        num_scalar_prefetch=2,
        grid=(n_blocks,),
        in_specs=[
            pl.BlockSpec((MOE_BLOCK, d), last),
            pl.BlockSpec((1, d, de), wsel), pl.BlockSpec((1, 1, de), wsel),
            pl.BlockSpec((1, d, de), wsel), pl.BlockSpec((1, 1, de), wsel),
            pl.BlockSpec((1, de, d), wsel), pl.BlockSpec((1, 1, d), wsel),
        ],
        out_specs=pl.BlockSpec((MOE_BLOCK, d), lambda i, be, nu: (i, 0)),
        scratch_shapes=[pltpu.VMEM((d, de), BF16), pltpu.VMEM((d, de), BF16), pltpu.VMEM((de, d), BF16)],
    )
    return pl.pallas_call(
        _expert_body,
        grid_spec=grid_spec,
        out_shape=jax.ShapeDtypeStruct((rows, d), F32),
        compiler_params=_params(("arbitrary",)),
        name="moe_experts",
    )(block_e, n_used, xs, w_gate, b_gate[:, None, :], w_up, b_up[:, None, :], w_down, b_down[:, None, :])


def _combine_body(dest_ref, base_ref, gate_ref, y_ref, g2_ref, b2_ref, o_ref, buf_ref, sem, *, tb):
    def row_copy(t, r, d):
        return pltpu.make_async_copy(y_ref.at[pl.ds(d, 1)], buf_ref.at[r, pl.ds(t, 1)], sem)

    def issue(t, c):
        for r in range(TOP_K):
            row_copy(t, r, dest_ref[0, 0, t * TOP_K + r]).start()
        return c

    lax.fori_loop(0, tb, issue, 0)

    def drain(t, c):
        for r in range(TOP_K):
            row_copy(t, r, dest_ref[0, 0, t * TOP_K + r]).wait()
        return c

    lax.fori_loop(0, tb, drain, 0)

    gate = gate_ref[...]
    y = base_ref[...]
    for r in range(TOP_K):
        y = y + gate[:, r:r + 1] * buf_ref[r]
    o_ref[...] = _layer_norm(y, g2_ref[...], b2_ref[...])


def _combine(base, gate_w, dest, y_rows, g2, b2):
    n, d = base.shape
    tb = _pick(n, (128, 64, 32, 16, 8))
    dest3 = dest.reshape(n // tb, 1, tb * TOP_K)
    row = lambda i: (i, 0)
    const = lambda i: (0, 0)
    return pl.pallas_call(
        functools.partial(_combine_body, tb=tb),
        grid=(n // tb,),
        in_specs=[
            pl.BlockSpec((1, 1, tb * TOP_K), lambda i: (i, 0, 0), memory_space=pltpu.SMEM),
            pl.BlockSpec((tb, d), row),
            pl.BlockSpec((tb, LANES), row),
            pl.BlockSpec(memory_space=pl.ANY),
            pl.BlockSpec((1, d), const), pl.BlockSpec((1, d), const),
        ],
        out_specs=pl.BlockSpec((tb, d), row),
        out_shape=jax.ShapeDtypeStruct((n, d), F32),
        scratch_shapes=[pltpu.VMEM((TOP_K, tb, d), F32), pltpu.SemaphoreType.DMA],
        compiler_params=_params(("arbitrary",)),
        name="moe_combine",
    )(dest3, base, gate_w, y_rows, g2, b2)


def _layer(x_p, x_s, ck, cv, clf, hist_s, page_table, p_p, p_s, lw, alpha):
    batch, seq, d = x_p.shape
    n_s = x_s.shape[0]
    n_p = batch * seq
    cc = lw["w_dw"].shape[1]
    aw = ATT_HEADS * HEAD_DIM
    heads = ATT_HEADS
    nz = 2 * cc + 3 * aw
    row = lambda v: v[None, :]

    w_in_b = lw["w_in"][:, :nz].astype(BF16)
    b_in_row = row(lw["b_in"][:nz])
    wft_b = lw["w_in"][:, nz:].T.astype(BF16)
    bft_col = lw["b_in"][nz:, None]
    b_dw, gg, gb = row(lw["b_dw"]), row(lw["gn_gain"]), row(lw["gn_bias"])
    gsz = cc // CONV_GROUPS
    grp = jnp.arange(cc) // gsz
    gmat = ((grp[:, None] == grp[None, :]).astype(F32) / gsz).astype(BF16)
    wo_b = lw["w_out"].astype(BF16)
    wr = jnp.pad(lw["w_router"], ((0, 0), (0, LANES - N_EXPERTS)))
    wr_hi = wr.astype(BF16)
    wr_lo = (wr - wr_hi.astype(F32)).astype(BF16)
    br_row = row(jnp.pad(lw["b_router"], (0, LANES - N_EXPERTS), constant_values=NEG_INF))
    wpg_b = lw["w_ple_gate"].astype(BF16)
    wpp_b = lw["w_ple_proj"].astype(BF16)

    xp2 = x_p.reshape(n_p, d)
    xs2 = x_s.reshape(n_s, d)

    u_p, q_p, k_p, v_p, kb_p, vb_p, lft_p = _in_proj(xp2, w_in_b, b_in_row, wft_b, bft_col, cc, aw)
    conv_p = _conv_prompt(u_p, batch, seq, lw["w_dw"], b_dw, gg, gb, gmat)
    negd = _neg_cumsum(lft_p, batch, seq).reshape(heads // 2, 2, n_p)
    att_p = _attention_prompt(q_p, kb_p, vb_p, negd, batch, seq)
    h_p, base_p, idx_p, gate_p, cnt_p = _mix(xp2, conv_p, att_p, p_p.reshape(n_p, -1), wo_b,
                                             row(lw["ln1_gain"]), row(lw["ln1_bias"]), wr_hi, wr_lo, br_row,
                                             wpg_b, row(lw["b_ple_gate"]), wpp_b, alpha)

    u_s, q_s, k_s, v_s, _, _, lft_s = _in_proj(xs2, w_in_b, b_in_row, wft_b, bft_col, cc, aw)
    conv_s = _conv_sample(hist_s.transpose(1, 0, 2), u_s, lw["w_dw"], b_dw, gg, gb, gmat)
    lf_s = lft_s.T
    pool, page = ck.shape[0], ck.shape[1]
    att_s = _attention_sample(q_s, k_s, v_s, lf_s, ck.reshape(pool, page, aw), cv.reshape(pool, page, aw),
                              clf.transpose(0, 2, 1), page_table)
    h_s, base_s, idx_s, gate_s, cnt_s = _mix(xs2, conv_s, att_s, p_s.reshape(n_s, -1), wo_b,
                                             row(lw["ln1_gain"]), row(lw["ln1_bias"]), wr_hi, wr_lo, br_row,
                                             wpg_b, row(lw["b_ple_gate"]), wpp_b, alpha)

    n_tok = n_p + n_s
    counts = (cnt_p + cnt_s)[0].astype(I32)
    padded = (counts + MOE_BLOCK - 1) // MOE_BLOCK * MOE_BLOCK
    pad_end = jnp.cumsum(padded)
    pad_start = pad_end - padded
    n_blocks = -(-(n_tok * TOP_K) // MOE_BLOCK) + N_EXPERTS
    n_used = (pad_end[N_EXPERTS - 1] // MOE_BLOCK).astype(I32).reshape(1)
    block_e = jnp.minimum(jnp.searchsorted(pad_end[:N_EXPERTS], jnp.arange(n_blocks, dtype=I32) * MOE_BLOCK,
                                           side="right"), N_EXPERTS - 1).astype(I32)
    dest = _positions(jnp.concatenate([idx_p, idx_s], axis=0), pad_start.astype(F32)[None, :])[:, :TOP_K]
    dest_p, dest_s = dest[:n_p], dest[n_p:]

    xs_rows = jnp.zeros((n_blocks * MOE_BLOCK, d), F32)
    xs_rows = _dispatch(h_p, dest_p, xs_rows)
    xs_rows = _dispatch(h_s, dest_s, xs_rows)
    y_rows = _experts(xs_rows, block_e, n_used, lw["w_gate"], lw["b_gate"], lw["w_up"], lw["b_up"],
                      lw["w_down"], lw["b_down"])
    g2, b2 = row(lw["ln2_gain"]), row(lw["ln2_bias"])
    y_p = _combine(base_p, gate_p, dest_p, y_rows, g2, b2).reshape(batch, seq, d)
    y_s = _combine(base_s, gate_s, dest_s, y_rows, g2, b2).reshape(n_s, 1, d)

    lf_p = lft_p.T.reshape(batch, seq, heads)
    hist_p = u_p.reshape(batch, seq, cc)[:, seq - (DW_WIDTH - 1):]
    hist_new_s = jnp.concatenate([hist_s[:, 1:], u_s[:, None, :]], axis=1)
    return (y_p, y_s,
            k_p.reshape(batch, seq, heads, HEAD_DIM), v_p.reshape(batch, seq, heads, HEAD_DIM), lf_p, hist_p,
            k_s.reshape(n_s, 1, heads, HEAD_DIM), v_s.reshape(n_s, 1, heads, HEAD_DIM),
            lf_s.reshape(n_s, 1, heads), hist_new_s)


def kernel(x_prompt, x_sample, cache_k, cache_v, cache_logf, state_conv, page_table, p_prompt, p_sample,
           w_in, b_in, w_dw, b_dw, gn_gain, gn_bias, w_out, ln1_gain, ln1_bias,
           w_router, b_router, w_gate, b_gate, w_up, b_up, w_down, b_down,
           w_ple_gate, b_ple_gate, w_ple_proj, ln2_gain, ln2_bias):
    depth = w_in.shape[0]
    if depth != 1 or x_sample.shape[1] != 1:
        raise NotImplementedError("one layer and one new token per sample sequence are supported")
    alpha = (2.0 * depth) ** 0.25
    lw = dict(w_in=w_in[0], b_in=b_in[0], w_dw=w_dw[0], b_dw=b_dw[0], gn_gain=gn_gain[0], gn_bias=gn_bias[0],
              w_out=w_out[0], ln1_gain=ln1_gain[0], ln1_bias=ln1_bias[0],
              w_router=w_router[0], b_router=b_router[0], w_gate=w_gate[0], b_gate=b_gate[0],
              w_up=w_up[0], b_up=b_up[0], w_down=w_down[0], b_down=b_down[0],
              w_ple_gate=w_ple_gate[0], b_ple_gate=b_ple_gate[0], w_ple_proj=w_ple_proj[0],
              ln2_gain=ln2_gain[0], ln2_bias=ln2_bias[0])
    outs = _layer(x_prompt, x_sample[:, 0], cache_k[0], cache_v[0], cache_logf[0], state_conv[0], page_table,
                  p_prompt[0], p_sample[0], lw, alpha)
    (y_p, y_s, k_p, v_p, lf_p, c_p, k_s, v_s, lf_s, c_s) = outs
    return (y_p, y_s, k_p[None], v_p[None], lf_p[None], c_p[None], k_s[None], v_s[None], lf_s[None], c_s[None])
```

```python
import functools

import jax
import jax.numpy as jnp
from jax import lax
from jax.experimental import pallas as pl
from jax.experimental.pallas import tpu as pltpu

F32 = jnp.float32
BF16 = jnp.bfloat16
I32 = jnp.int32

ATT_HEADS = 8
HEAD_DIM = 64
CONV_GROUPS = 8
DW_WIDTH = 31
N_EXPERTS = 32
TOP_K = 4
SWIGLU_LIMIT = 7.0
SWIGLU_ALPHA = 1.702
LN_EPS = 1e-5
NEG_INF = -1e30
LOG2_E = 1.4426950408889634

LANES = 128
SUBLANES = 8
CONV_HALO = 32
MOE_BLOCK = 256
VMEM_LIMIT = 56 * 1024 * 1024


def _pick(n, cands):
    for c in cands:
        if n % c == 0:
            return c
    raise ValueError(f"no block size in {cands} divides {n}")


def _params(sem, vmem=VMEM_LIMIT):
    return pltpu.CompilerParams(dimension_semantics=sem, vmem_limit_bytes=vmem)


def _sigmoid(x):
    return 1.0 / (1.0 + jnp.exp(-x))


def _log_sigmoid(x):
    return jnp.minimum(x, 0.0) - jnp.log1p(jnp.exp(-jnp.abs(x)))


def _dot(a, b):
    return jnp.dot(a, b, preferred_element_type=F32)


def _dot_nt(a, b):
    return lax.dot_general(a, b, (((1,), (1,)), ((), ())), preferred_element_type=F32)


def _split3(x):
    x1 = x.astype(BF16)
    r1 = x - x1.astype(F32)
    x2 = r1.astype(BF16)
    x3 = (r1 - x2.astype(F32)).astype(BF16)
    return x1, x2, x3


def _dot3(x, m):
    x1, x2, x3 = _split3(x)
    return _dot(x1, m) + _dot(x2, m) + _dot(x3, m)


def _dot3_left(m, x):
    x1, x2, x3 = _split3(x)
    return _dot(m, x1) + _dot(m, x2) + _dot(m, x3)


def _layer_norm(x, g, b):
    mu = jnp.mean(x, axis=-1, keepdims=True)
    d = x - mu
    var = jnp.mean(d * d, axis=-1, keepdims=True)
    return d * lax.rsqrt(var + LN_EPS) * g + b


def _inproj_body(x_ref, w_ref, b_ref, wft_ref, bft_ref,
                 u_ref, q_ref, k_ref, v_ref, kb_ref, vb_ref, lft_ref, *, cc, aw, scale):
    x = x_ref[...].astype(BF16)

    def seg(lo, hi):
        return _dot(x, w_ref[:, lo:hi]) + b_ref[:, lo:hi]

    a = seg(0, cc)
    g = seg(cc, 2 * cc)
    u_ref[...] = a * _sigmoid(g)
    o = 2 * cc
    q_ref[...] = (seg(o, o + aw) * scale).astype(BF16)
    k = seg(o + aw, o + 2 * aw)
    k_ref[...] = k
    kb_ref[...] = k.astype(BF16)
    v = seg(o + 2 * aw, o + 3 * aw)
    v_ref[...] = v
    vb_ref[...] = v.astype(BF16)
    ft = _dot_nt(wft_ref[...], x) + bft_ref[...]
    lft_ref[...] = _log_sigmoid(ft)


def _in_proj(x, w_b, b_row, wft_b, bft_col, cc, aw, q_scale):
    n, d = x.shape
    tm = _pick(n, (512, 384, 256, 128, 64, 32, 16, 8))
    nw = w_b.shape[1]
    heads = wft_b.shape[0]
    row = lambda i: (i, 0)
    const = lambda i: (0, 0)
    outs = (
        jax.ShapeDtypeStruct((n, cc), F32),
        jax.ShapeDtypeStruct((n, aw), BF16),
        jax.ShapeDtypeStruct((n, aw), F32),
        jax.ShapeDtypeStruct((n, aw), F32),
        jax.ShapeDtypeStruct((n, aw), BF16),
        jax.ShapeDtypeStruct((n, aw), BF16),
        jax.ShapeDtypeStruct((heads, n), F32),
    )
    return pl.pallas_call(
        functools.partial(_inproj_body, cc=cc, aw=aw, scale=q_scale),
        grid=(n // tm,),
        in_specs=[
            pl.BlockSpec((tm, d), row),
            pl.BlockSpec((d, nw), const),
            pl.BlockSpec((1, nw), const),
            pl.BlockSpec((heads, d), const),
            pl.BlockSpec((heads, 1), const),
        ],
        out_specs=(
            pl.BlockSpec((tm, cc), row),
            pl.BlockSpec((tm, aw), row),
            pl.BlockSpec((tm, aw), row),
            pl.BlockSpec((tm, aw), row),
            pl.BlockSpec((tm, aw), row),
            pl.BlockSpec((tm, aw), row),
            pl.BlockSpec((heads, tm), lambda i: (0, i)),
        ),
        out_shape=outs,
        compiler_params=_params(("parallel",)),
        name="in_proj",
    )(x, w_b, b_row, wft_b, bft_col)


def _group_norm_silu(y, m, gg, gb):
    mean = _dot3(y, m)
    d = y - mean
    var = _dot3(d * d, m)
    yn = d * lax.rsqrt(var + LN_EPS) * gg + gb
    return yn * _sigmoid(yn)


def _conv_body(u_ref, halo_ref, w_ref, b_ref, gg_ref, gb_ref, m_ref, o_ref, ext_ref, y_ref, *, ts, rc):
    i = pl.program_id(1)
    halo = halo_ref[...]
    ext_ref[0:CONV_HALO, :] = jnp.where(i > 0, halo, jnp.zeros_like(halo))
    ext_ref[CONV_HALO:, :] = u_ref[...]
    off = CONV_HALO - (DW_WIDTH - 1)
    c = u_ref.shape[1]
    for r0 in range(0, ts, rc):
        acc = jnp.broadcast_to(b_ref[...], (rc, c))
        for j in range(DW_WIDTH):
            s = r0 + off + j
            acc = acc + w_ref[j:j + 1, :] * ext_ref[s:s + rc, :]
        y_ref[r0:r0 + rc, :] = acc
    o_ref[...] = _group_norm_silu(y_ref[...], m_ref[...], gg_ref[...], gb_ref[...]).astype(BF16)


def _conv_prompt(u, batch, seq, w_dw, b_row, gg_row, gb_row, gmat):
    n, c = u.shape
    ts = _pick(seq, (512, 256, 128, 64, 32))
    rc = min(ts, 64)
    nt = seq // ts
    hb = ts // CONV_HALO
    const = lambda b, i: (0, 0)
    return pl.pallas_call(
        functools.partial(_conv_body, ts=ts, rc=rc),
        grid=(batch, nt),
        in_specs=[
            pl.BlockSpec((ts, c), lambda b, i: (b * nt + i, 0)),
            pl.BlockSpec((CONV_HALO, c), lambda b, i: (jnp.maximum((b * nt + i) * hb - 1, 0), 0)),
            pl.BlockSpec((DW_WIDTH, c), const),
            pl.BlockSpec((1, c), const),
            pl.BlockSpec((1, c), const),
            pl.BlockSpec((1, c), const),
            pl.BlockSpec((c, c), const),
        ],
        out_specs=pl.BlockSpec((ts, c), lambda b, i: (b * nt + i, 0)),
        out_shape=jax.ShapeDtypeStruct((n, c), BF16),
        scratch_shapes=[pltpu.VMEM((CONV_HALO + ts, c), F32), pltpu.VMEM((ts, c), F32)],
        compiler_params=_params(("parallel", "parallel")),
        name="conv_prompt",
    )(u, u, w_dw, b_row, gg_row, gb_row, gmat)


def _conv_sample_body(hist_ref, u_ref, w_ref, b_ref, gg_ref, gb_ref, m_ref, o_ref):
    acc = b_ref[...] + w_ref[DW_WIDTH - 1:DW_WIDTH, :] * u_ref[...]
    for j in range(DW_WIDTH - 1):
        acc = acc + w_ref[j:j + 1, :] * hist_ref[j]
    o_ref[...] = _group_norm_silu(acc, m_ref[...], gg_ref[...], gb_ref[...]).astype(BF16)


def _conv_sample(hist_t, u, w_dw, b_row, gg_row, gb_row, gmat):
    n, c = u.shape
    full2 = lambda i: (0, 0)
    return pl.pallas_call(
        _conv_sample_body,
        grid=(1,),
        in_specs=[
            pl.BlockSpec(hist_t.shape, lambda i: (0, 0, 0)),
            pl.BlockSpec((n, c), full2),
            pl.BlockSpec((DW_WIDTH, c), full2),
            pl.BlockSpec((1, c), full2),
            pl.BlockSpec((1, c), full2),
            pl.BlockSpec((1, c), full2),
            pl.BlockSpec((c, c), full2),
        ],
        out_specs=pl.BlockSpec((n, c), full2),
        out_shape=jax.ShapeDtypeStruct((n, c), BF16),
        compiler_params=_params(("arbitrary",)),
        name="conv_sample",
    )(hist_t, u, w_dw, b_row, gg_row, gb_row, gmat)


def _cumsum_body(lf_ref, tri_ref, o_ref, carry_ref):
    @pl.when(pl.program_id(1) == 0)
    def _():
        carry_ref[...] = jnp.zeros_like(carry_ref)

    cl = lf_ref.shape[1]
    cum = _dot3(lf_ref[...], tri_ref[...]) + carry_ref[:, 0:1]
    o_ref[...] = -LOG2_E * cum
    carry_ref[...] = jnp.broadcast_to(cum[:, cl - 1:cl], carry_ref.shape)


def _neg_cumsum(lft, batch, seq):
    heads, n = lft.shape
    cl = _pick(seq, (512, 256, 128))
    nc = seq // cl
    tri = jnp.triu(jnp.ones((cl, cl), F32)).astype(BF16)
    return pl.pallas_call(
        _cumsum_body,
        grid=(batch, nc),
        in_specs=[pl.BlockSpec((heads, cl), lambda b, c: (0, b * nc + c)),
                  pl.BlockSpec((cl, cl), lambda b, c: (0, 0))],
        out_specs=pl.BlockSpec((heads, cl), lambda b, c: (0, b * nc + c)),
        out_shape=jax.ShapeDtypeStruct((heads, n), F32),
        scratch_shapes=[pltpu.VMEM((heads, LANES), F32)],
        compiler_params=_params(("parallel", "arbitrary")),
        name="forget_cumsum",
    )(lft, tri)


def _att_body(qi_ref, ki_ref, q_ref, k_ref, v_ref, nd_ref, o_ref, m_ref, l_ref, acc_ref, *, hd):
    step = pl.program_id(2)
    qi = qi_ref[step]
    ki = ki_ref[step]
    tq = q_ref.shape[0]
    tk = k_ref.shape[0]
    is_a = lax.broadcasted_iota(I32, (1, LANES), 1) < hd

    @pl.when(ki == 0)
    def _():
        m_ref[...] = jnp.full(m_ref.shape, NEG_INF, F32)
        l_ref[...] = jnp.zeros_like(l_ref)
        acc_ref[...] = jnp.zeros_like(acc_ref)

    def process(diagonal):
        q = q_ref[...]
        k = k_ref[...]
        v = v_ref[...]
        nd = nd_ref[0]
        zero = jnp.zeros_like(q)
        pvs = []
        alphas = []
        for h in range(2):
            qh = jnp.where(is_a, q, zero) if h == 0 else jnp.where(is_a, zero, q)
            s = _dot_nt(qh, k) + nd[h:h + 1, :]
            if diagonal:
                row = lax.broadcasted_iota(I32, (tq, tk), 0)
                col = lax.broadcasted_iota(I32, (tq, tk), 1)
                s = jnp.where(col <= row, s, NEG_INF)
            m_old = m_ref[h]
            m_new = jnp.maximum(m_old, jnp.max(s, axis=1, keepdims=True))
            alpha = jnp.exp2(m_old - m_new)
            p = jnp.exp2(s - jnp.tile(m_new, (1, tk // LANES)))
            l_ref[h] = alpha * l_ref[h] + jnp.sum(p, axis=1, keepdims=True)
            m_ref[h] = m_new
            pvs.append(_dot(p.astype(BF16), v))
            alphas.append(alpha)
        acc_ref[...] = (acc_ref[...] * jnp.where(is_a, alphas[0], alphas[1])
                        + jnp.where(is_a, pvs[0], pvs[1]))

    @pl.when(ki < qi)
    def _():
        process(False)

    @pl.when(ki == qi)
    def _():
        process(True)
        o_ref[...] = (acc_ref[...] / jnp.where(is_a, l_ref[0], l_ref[1])).astype(BF16)


def _attention_prompt(q_b, k_b, v_b, negd, batch, seq):
    n, aw = q_b.shape
    pairs = aw // LANES
    t = _pick(seq, (512, 256, 128))
    nq = seq // t
    steps = [(a, b) for a in range(nq) for b in range(a + 1)]
    qi_tab = jnp.asarray([s[0] for s in steps], I32)
    ki_tab = jnp.asarray([s[1] for s in steps], I32)
    grid_spec = pltpu.PrefetchScalarGridSpec(
        num_scalar_prefetch=2,
        grid=(batch, pairs, len(steps)),
        in_specs=[
            pl.BlockSpec((t, LANES), lambda b, hp, s, qt, kt: (b * nq + qt[s], hp)),
            pl.BlockSpec((t, LANES), lambda b, hp, s, qt, kt: (b * nq + kt[s], hp)),
            pl.BlockSpec((t, LANES), lambda b, hp, s, qt, kt: (b * nq + kt[s], hp)),
            pl.BlockSpec((1, 2, t), lambda b, hp, s, qt, kt: (hp, 0, b * nq + kt[s])),
        ],
        out_specs=pl.BlockSpec((t, LANES), lambda b, hp, s, qt, kt: (b * nq + qt[s], hp)),
        scratch_shapes=[pltpu.VMEM((2, t, LANES), F32), pltpu.VMEM((2, t, LANES), F32),
                        pltpu.VMEM((t, LANES), F32)],
    )
    return pl.pallas_call(
        functools.partial(_att_body, hd=HEAD_DIM),
        grid_spec=grid_spec,
        out_shape=jax.ShapeDtypeStruct((n, aw), BF16),
        compiler_params=_params(("parallel", "parallel", "arbitrary")),
        name="attention_prompt",
    )(qi_tab, ki_tab, q_b, k_b, v_b, negd)


def _pool_suffix_body(lf_ref, cw_ref, ct_ref, w_ref, t_ref):
    x = lf_ref[...]
    w_ref[...] = _dot3(x, cw_ref[...])
    t_ref[...] = _dot3(x, ct_ref[...])


def _pool_suffix(lf_flat, heads):
    pool, rows = lf_flat.shape
    tp = _pick(pool, (256, 128, 64, 32, 16, 8, pool))
    r = jnp.arange(rows)
    same = (r[:, None] % heads) == (r[None, :] % heads)
    cw = (same & (r[:, None] // heads > r[None, :] // heads)).astype(BF16)
    ct = same.astype(BF16)
    blk = pl.BlockSpec((tp, rows), lambda i: (i, 0))
    mat = pl.BlockSpec((rows, rows), lambda i: (0, 0))
    return pl.pallas_call(
        _pool_suffix_body,
        grid=(pool // tp,),
        in_specs=[blk, mat, mat],
        out_specs=(blk, blk),
        out_shape=(jax.ShapeDtypeStruct((pool, rows), F32), jax.ShapeDtypeStruct((pool, rows), F32)),
        compiler_params=_params(("parallel",)),
        name="pool_forget_suffix",
    )(lf_flat, cw, ct)


def _satt_body(pt_ref, *refs, n_pages, heads):
    k_refs = refs[:n_pages]
    v_refs = refs[n_pages:2 * n_pages]
    w_refs = refs[2 * n_pages:3 * n_pages]
    t_refs = refs[3 * n_pages:4 * n_pages]
    q_ref, kn_ref, vn_ref, lfn_ref, o_ref, s_ref = refs[4 * n_pages:]
    page, _, hd = k_refs[0].shape[1:]
    rows = page * heads
    q = q_ref[0]
    own = ((lax.broadcasted_iota(I32, (heads, rows), 1) & (heads - 1))
           == lax.broadcasted_iota(I32, (heads, rows), 0))
    s_new = jnp.sum(q.astype(F32) * kn_ref[0], axis=1, keepdims=True)
    m = s_new
    later = lfn_ref[0]
    for j in reversed(range(n_pages)):
        k2 = k_refs[j][0].reshape(rows, hd).astype(BF16)
        s = _dot_nt(q, k2) + (w_refs[j][0] + later)
        s = jnp.where(own, s, NEG_INF)
        s_ref[j] = s
        m = jnp.maximum(m, jnp.max(s, axis=1, keepdims=True))
        later = later + t_refs[j][0]
    p_new = jnp.exp(s_new - m)
    l = p_new
    o = p_new * vn_ref[0]
    for j in range(n_pages):
        p = jnp.exp(s_ref[j] - m)
        l = l + jnp.sum(p, axis=1, keepdims=True)
        o = o + _dot(p.astype(BF16), v_refs[j][0].reshape(rows, hd).astype(BF16))
    o_ref[0] = (o / l).astype(BF16)


def _attention_sample(q_b, k_new, v_new, lf_new, cache_k, cache_v, cache_lf, page_table):
    n, aw = q_b.shape
    n_pages = page_table.shape[1]
    pool, page, heads, hd = cache_k.shape
    if heads & (heads - 1):
        raise NotImplementedError("head count must be a power of two")
    rows = page * heads
    within, totals = _pool_suffix(cache_lf.reshape(pool, rows), heads)
    within = within[:, None, :]
    totals = totals[:, None, :]
    lfn_flat = jnp.tile(lf_new, (1, page))[:, None, :]

    def kv_spec(j):
        return pl.BlockSpec((1, page, heads, hd), lambda i, pt, j=j: (pt[i, j], 0, 0, 0))

    def row_spec(j):
        return pl.BlockSpec((1, 1, rows), lambda i, pt, j=j: (pt[i, j], 0, 0))

    per_seq = lambda shape: pl.BlockSpec((1,) + shape, lambda i, pt: (i, 0, 0))
    grid_spec = pltpu.PrefetchScalarGridSpec(
        num_scalar_prefetch=1,
        grid=(n,),
        in_specs=([kv_spec(j) for j in range(n_pages)] + [kv_spec(j) for j in range(n_pages)]
                  + [row_spec(j) for j in range(n_pages)] + [row_spec(j) for j in range(n_pages)]
                  + [per_seq((heads, hd)), per_seq((heads, hd)), per_seq((heads, hd)), per_seq((1, rows))]),
        out_specs=per_seq((heads, hd)),
        scratch_shapes=[pltpu.VMEM((n_pages, heads, rows), F32)],
    )
    out = pl.pallas_call(
        functools.partial(_satt_body, n_pages=n_pages, heads=heads),
        grid_spec=grid_spec,
        out_shape=jax.ShapeDtypeStruct((n, heads, hd), BF16),
        compiler_params=_params(("arbitrary",)),
        name="attention_sample",
    )(page_table, *([cache_k] * n_pages), *([cache_v] * n_pages), *([within] * n_pages), *([totals] * n_pages),
      q_b.reshape(n, heads, hd), k_new.reshape(n, heads, hd), v_new.reshape(n, heads, hd), lfn_flat)
    return out.reshape(n, aw)


def _mix_body(x_ref, c_ref, a_ref, p_ref, wo_ref, g1_ref, b1_ref, wrh_ref, wrl_ref, br_ref,
              wpg_ref, bpg_ref, wpp_ref, h_ref, base_ref, idx_ref, gate_ref, cnt_ref, *, alpha, cc):
    mix = _dot(c_ref[...], wo_ref[:cc, :]) + _dot(a_ref[...], wo_ref[cc:, :])
    h = _layer_norm(alpha * x_ref[...] + mix, g1_ref[...], b1_ref[...])
    h_ref[...] = h
    hb = h.astype(BF16)
    hl = (h - hb.astype(F32)).astype(BF16)
    wrh = wrh_ref[...]
    logits = _dot(hb, wrh) + _dot(hl, wrh) + _dot(hb, wrl_ref[...]) + br_ref[...]

    tm = logits.shape[0]
    lane = lax.broadcasted_iota(I32, (tm, LANES), 1).astype(F32)
    vals, idxs = [], []
    l = logits
    for _ in range(TOP_K):
        m = jnp.max(l, axis=1, keepdims=True)
        ix = jnp.min(jnp.where(l == m, lane, float(LANES)), axis=1, keepdims=True)
        vals.append(m)
        idxs.append(ix)
        l = jnp.where(lane == ix, -jnp.inf, l)
    es = [jnp.exp(v - vals[0]) for v in vals]
    den = es[0] + es[1] + es[2] + es[3]
    idx_w = jnp.zeros((tm, LANES), F32)
    gate_w = jnp.zeros((tm, LANES), F32)
    sel = jnp.zeros((tm, LANES), F32)
    for r in range(TOP_K):
        idx_w = jnp.where(lane == float(r), idxs[r], idx_w)
        gate_w = jnp.where(lane == float(r), es[r] / den, gate_w)
        sel = sel + (lane == idxs[r]).astype(F32)
    idx_ref[...] = idx_w
    gate_ref[...] = gate_w

    @pl.when(pl.program_id(0) == 0)
    def _():
        cnt_ref[...] = jnp.zeros_like(cnt_ref)

    cnt_ref[...] += jnp.sum(sel, axis=0, keepdims=True)

    ple = _sigmoid(_dot(hb, wpg_ref[...]) + bpg_ref[...]) * _dot(p_ref[...].astype(BF16), wpp_ref[...])
    base_ref[...] = alpha * h + ple


def _mix(x, conv_b, att_b, p, wo_b, g1, b1, wr_hi, wr_lo, br_row, wpg_b, bpg, wpp_b, alpha):
    n, d = x.shape
    cc = conv_b.shape[1]
    aw = att_b.shape[1]
    pd = p.shape[1]
    tm = _pick(n, (256, 128, 64, 32, 16, 8))
    row = lambda i: (i, 0)
    const = lambda i: (0, 0)
    outs = (
        jax.ShapeDtypeStruct((n, d), F32),
        jax.ShapeDtypeStruct((n, d), F32),
        jax.ShapeDtypeStruct((n, LANES), F32),
        jax.ShapeDtypeStruct((n, LANES), F32),
        jax.ShapeDtypeStruct((1, LANES), F32),
    )
    return pl.pallas_call(
        functools.partial(_mix_body, alpha=alpha, cc=cc),
        grid=(n // tm,),
        in_specs=[
            pl.BlockSpec((tm, d), row), pl.BlockSpec((tm, cc), row), pl.BlockSpec((tm, aw), row),
            pl.BlockSpec((tm, pd), row),
            pl.BlockSpec((d, d), const), pl.BlockSpec((1, d), const), pl.BlockSpec((1, d), const),
            pl.BlockSpec((d, LANES), const), pl.BlockSpec((d, LANES), const), pl.BlockSpec((1, LANES), const),
            pl.BlockSpec((d, d), const), pl.BlockSpec((1, d), const), pl.BlockSpec((pd, d), const),
        ],
        out_specs=(
            pl.BlockSpec((tm, d), row), pl.BlockSpec((tm, d), row),
            pl.BlockSpec((tm, LANES), row), pl.BlockSpec((tm, LANES), row),
            pl.BlockSpec((1, LANES), const),
        ),
        out_shape=outs,
        compiler_params=_params(("arbitrary",)),
        name="out_proj_router",
    )(x, conv_b, att_b, p, wo_b, g1, b1, wr_hi, wr_lo, br_row, wpg_b, bpg, wpp_b)


def _positions_body(idx_ref, start_ref, tri_ref, o_ref, carry_ref):
    @pl.when(pl.program_id(0) == 0)
    def _():
        carry_ref[...] = jnp.zeros_like(carry_ref)

    ids = idx_ref[...]
    tm = ids.shape[0]
    lane = lax.broadcasted_iota(I32, (tm, LANES), 1).astype(F32)
    hot = [lane == ids[:, r:r + 1] for r in range(TOP_K)]
    sel = jnp.zeros((tm, LANES), F32)
    for r in range(TOP_K):
        sel = sel + hot[r].astype(F32)
    rank = _dot(tri_ref[...], sel.astype(BF16)) + carry_ref[...]
    pos = rank + start_ref[...]
    out = jnp.zeros((tm, LANES), F32)
    for r in range(TOP_K):
        d = jnp.sum(jnp.where(hot[r], pos, 0.0), axis=1, keepdims=True)
        out = jnp.where(lane == float(r), d, out)
    o_ref[...] = out.astype(I32)
    carry_ref[...] += jnp.sum(sel, axis=0, keepdims=True)


def _positions(idx_w, start_row):
    n = idx_w.shape[0]
    tm = _pick(n, (512, 384, 256, 128, 64, 32, 16, 8))
    tri = jnp.tril(jnp.ones((tm, tm), F32), -1).astype(BF16)
    return pl.pallas_call(
        _positions_body,
        grid=(n // tm,),
        in_specs=[pl.BlockSpec((tm, LANES), lambda i: (i, 0)),
                  pl.BlockSpec((1, LANES), lambda i: (0, 0)),
                  pl.BlockSpec((tm, tm), lambda i: (0, 0))],
        out_specs=pl.BlockSpec((tm, LANES), lambda i: (i, 0)),
        out_shape=jax.ShapeDtypeStruct((n, LANES), I32),
        scratch_shapes=[pltpu.VMEM((1, LANES), F32)],
        compiler_params=_params(("arbitrary",)),
        name="route_positions",
    )(idx_w, start_row, tri)


def _dispatch_body(dest_ref, h_ref, xs_in_ref, xs_ref, sem, *, tb):
    del xs_in_ref

    def row_copy(t, d):
        return pltpu.make_async_copy(h_ref.at[pl.ds(t, 1)], xs_ref.at[pl.ds(d, 1)], sem)

    def issue(t, c):
        for r in range(TOP_K):
            row_copy(t, dest_ref[0, 0, t * TOP_K + r]).start()
        return c

    lax.fori_loop(0, tb, issue, 0)

    def drain(t, c):
        for r in range(TOP_K):
            row_copy(t, dest_ref[0, 0, t * TOP_K + r]).wait()
        return c

    lax.fori_loop(0, tb, drain, 0)


def _dispatch(h, dest, xs):
    n, d = h.shape
    tb = _pick(n, (256, 128, 64, 32, 16, 8))
    dest3 = dest.reshape(n // tb, 1, tb * TOP_K)
    return pl.pallas_call(
        functools.partial(_dispatch_body, tb=tb),
        grid=(n // tb,),
        in_specs=[
            pl.BlockSpec((1, 1, tb * TOP_K), lambda i: (i, 0, 0), memory_space=pltpu.SMEM),
            pl.BlockSpec((tb, d), lambda i: (i, 0)),
            pl.BlockSpec(memory_space=pl.ANY),
        ],
        out_specs=pl.BlockSpec(memory_space=pl.ANY),
        out_shape=jax.ShapeDtypeStruct(xs.shape, xs.dtype),
        scratch_shapes=[pltpu.SemaphoreType.DMA],
        input_output_aliases={2: 0},
        compiler_params=_params(("arbitrary",)),
        name="moe_dispatch",
    )(dest3, h, xs)


def _expert_body(be_ref, nu_ref, x_ref, wg_ref, bg_ref, wu_ref, bu_ref, wd_ref, bd_ref, o_ref,
                 wgb_ref, wub_ref, wdb_ref):
    i = pl.program_id(0)

    @pl.when(i < nu_ref[0])
    def _():
        @pl.when((i == 0) | (be_ref[i] != be_ref[jnp.maximum(i - 1, 0)]))
        def _():
            wgb_ref[...] = wg_ref[0].astype(BF16)
            wub_ref[...] = wu_ref[0].astype(BF16)
            wdb_ref[...] = wd_ref[0].astype(BF16)

        x = x_ref[...].astype(BF16)
        g = jnp.minimum(_dot(x, wgb_ref[...]) + bg_ref[0], SWIGLU_LIMIT)
        u = jnp.clip(_dot(x, wub_ref[...]) + bu_ref[0], -SWIGLU_LIMIT, SWIGLU_LIMIT)
        hidden = (u + 1.0) * (g * _sigmoid(SWIGLU_ALPHA * g))
        o_ref[...] = _dot(hidden.astype(BF16), wdb_ref[...]) + bd_ref[0]

    @pl.when(i >= nu_ref[0])
    def _():
        o_ref[...] = jnp.zeros_like(o_ref)


def _experts(xs, block_e, n_used, w_gate, b_gate, w_up, b_up, w_down, b_down):
    rows, d = xs.shape
    n_e, _, de = w_gate.shape
    n_blocks = rows // MOE_BLOCK
    last = lambda i, be, nu: (jnp.minimum(i, nu[0] - 1), 0)
    wsel = lambda i, be, nu: (be[i], 0, 0)
    grid_spec = pltpu.PrefetchScalarGridSpec(
        num_scalar_prefetch=2,
        grid=(n_blocks,),
        in_specs=[
            pl.BlockSpec((MOE_BLOCK, d), last),
            pl.BlockSpec((1, d, de), wsel), pl.BlockSpec((1, 1, de), wsel),
            pl.BlockSpec((1, d, de), wsel), pl.BlockSpec((1, 1, de), wsel),
            pl.BlockSpec((1, de, d), wsel), pl.BlockSpec((1, 1, d), wsel),
        ],
        out_specs=pl.BlockSpec((MOE_BLOCK, d), lambda i, be, nu: (i, 0)),
        scratch_shapes=[pltpu.VMEM((d, de), BF16), pltpu.VMEM((d, de), BF16), pltpu.VMEM((de, d), BF16)],
    )
    return pl.pallas_call(
        _expert_body,
        grid_spec=grid_spec,
        out_shape=jax.ShapeDtypeStruct((rows, d), F32),
        compiler_params=_params(("arbitrary",)),
        name="moe_experts",
    )(block_e, n_used, xs, w_gate, b_gate[:, None, :], w_up, b_up[:, None, :], w_down, b_down[:, None, :])


def _combine_body(dest_ref, base_ref, gate_ref, y_ref, g2_ref, b2_ref, o_ref, buf_ref, sem, *, tb):
    def row_copy(t, r, d):
        return pltpu.make_async_copy(y_ref.at[pl.ds(d, 1)], buf_ref.at[r, pl.ds(t, 1)], sem)

    def issue(t, c):
        for r in range(TOP_K):
            row_copy(t, r, dest_ref[0, 0, t * TOP_K + r]).start()
        return c

    lax.fori_loop(0, tb, issue, 0)

    def drain(t, c):
        for r in range(TOP_K):
            row_copy(t, r, dest_ref[0, 0, t * TOP_K + r]).wait()
        return c

    lax.fori_loop(0, tb, drain, 0)

    gate = gate_ref[...]
    y = base_ref[...]
    for r in range(TOP_K):
        y = y + gate[:, r:r + 1] * buf_ref[r]
    o_ref[...] = _layer_norm(y, g2_ref[...], b2_ref[...])


def _combine(base, gate_w, dest, y_rows, g2, b2):
    n, d = base.shape
    tb = _pick(n, (128, 64, 32, 16, 8))
    dest3 = dest.reshape(n // tb, 1, tb * TOP_K)
    row = lambda i: (i, 0)
    const = lambda i: (0, 0)
    return pl.pallas_call(
        functools.partial(_combine_body, tb=tb),
        grid=(n // tb,),
        in_specs=[
            pl.BlockSpec((1, 1, tb * TOP_K), lambda i: (i, 0, 0), memory_space=pltpu.SMEM),
            pl.BlockSpec((tb, d), row),
            pl.BlockSpec((tb, LANES), row),
            pl.BlockSpec(memory_space=pl.ANY),
            pl.BlockSpec((1, d), const), pl.BlockSpec((1, d), const),
        ],
        out_specs=pl.BlockSpec((tb, d), row),
        out_shape=jax.ShapeDtypeStruct((n, d), F32),
        scratch_shapes=[pltpu.VMEM((TOP_K, tb, d), F32), pltpu.SemaphoreType.DMA],
        compiler_params=_params(("arbitrary",)),
        name="moe_combine",
    )(dest3, base, gate_w, y_rows, g2, b2)


def _layer(x_p, x_s, ck, cv, clf, hist_s, page_table, p_p, p_s, lw, alpha):
    batch, seq, d = x_p.shape
    n_s = x_s.shape[0]
    n_p = batch * seq
    cc = lw["w_dw"].shape[1]
    aw = ATT_HEADS * HEAD_DIM
    heads = ATT_HEADS
    nz = 2 * cc + 3 * aw
    row = lambda v: v[None, :]

    w_in_b = lw["w_in"][:, :nz].astype(BF16)
    b_in_row = row(lw["b_in"][:nz])
    wft_b = lw["w_in"][:, nz:].T.astype(BF16)
    bft_col = lw["b_in"][nz:, None]
    b_dw, gg, gb = row(lw["b_dw"]), row(lw["gn_gain"]), row(lw["gn_bias"])
    gsz = cc // CONV_GROUPS
    grp = jnp.arange(cc) // gsz
    gmat = ((grp[:, None] == grp[None, :]).astype(F32) / gsz).astype(BF16)
    wo_b = lw["w_out"].astype(BF16)
    wr = jnp.pad(lw["w_router"], ((0, 0), (0, LANES - N_EXPERTS)))
    wr_hi = wr.astype(BF16)
    wr_lo = (wr - wr_hi.astype(F32)).astype(BF16)
    br_row = row(jnp.pad(lw["b_router"], (0, LANES - N_EXPERTS), constant_values=NEG_INF))
    wpg_b = lw["w_ple_gate"].astype(BF16)
    wpp_b = lw["w_ple_proj"].astype(BF16)

    xp2 = x_p.reshape(n_p, d)
    xs2 = x_s.reshape(n_s, d)

    u_p, q_p, k_p, v_p, kb_p, vb_p, lft_p = _in_proj(xp2, w_in_b, b_in_row, wft_b, bft_col, cc, aw,
                                                     HEAD_DIM ** -0.5 * LOG2_E)
    conv_p = _conv_prompt(u_p, batch, seq, lw["w_dw"], b_dw, gg, gb, gmat)
    negd = _neg_cumsum(lft_p, batch, seq).reshape(heads // 2, 2, n_p)
    att_p = _attention_prompt(q_p, kb_p, vb_p, negd, batch, seq)
    h_p, base_p, idx_p, gate_p, cnt_p = _mix(xp2, conv_p, att_p, p_p.reshape(n_p, -1), wo_b,
                                             row(lw["ln1_gain"]), row(lw["ln1_bias"]), wr_hi, wr_lo, br_row,
                                             wpg_b, row(lw["b_ple_gate"]), wpp_b, alpha)

    u_s, q_s, k_s, v_s, _, _, lft_s = _in_proj(xs2, w_in_b, b_in_row, wft_b, bft_col, cc, aw, HEAD_DIM ** -0.5)
    conv_s = _conv_sample(hist_s.transpose(1, 0, 2), u_s, lw["w_dw"], b_dw, gg, gb, gmat)
    lf_s = lft_s.T
    att_s = _attention_sample(q_s, k_s, v_s, lf_s, ck, cv, clf, page_table)
    h_s, base_s, idx_s, gate_s, cnt_s = _mix(xs2, conv_s, att_s, p_s.reshape(n_s, -1), wo_b,
                                             row(lw["ln1_gain"]), row(lw["ln1_bias"]), wr_hi, wr_lo, br_row,
                                             wpg_b, row(lw["b_ple_gate"]), wpp_b, alpha)

    n_tok = n_p + n_s
    counts = (cnt_p + cnt_s)[0].astype(I32)
    padded = (counts + MOE_BLOCK - 1) // MOE_BLOCK * MOE_BLOCK
    pad_end = jnp.cumsum(padded)
    pad_start = pad_end - padded
    n_blocks = -(-(n_tok * TOP_K) // MOE_BLOCK) + N_EXPERTS
    n_used = (pad_end[N_EXPERTS - 1] // MOE_BLOCK).astype(I32).reshape(1)
    block_row0 = jnp.arange(n_blocks, dtype=I32) * MOE_BLOCK
    block_e = jnp.minimum(jnp.sum((pad_end[None, :N_EXPERTS] <= block_row0[:, None]).astype(I32), axis=1),
                          N_EXPERTS - 1)
    dest = _positions(jnp.concatenate([idx_p, idx_s], axis=0), pad_start.astype(F32)[None, :])[:, :TOP_K]
    dest_p, dest_s = dest[:n_p], dest[n_p:]

    xs_rows = jnp.zeros((n_blocks * MOE_BLOCK, d), F32)
    xs_rows = _dispatch(h_p, dest_p, xs_rows)
    xs_rows = _dispatch(h_s, dest_s, xs_rows)
    y_rows = _experts(xs_rows, block_e, n_used, lw["w_gate"], lw["b_gate"], lw["w_up"], lw["b_up"],
                      lw["w_down"], lw["b_down"])
    g2, b2 = row(lw["ln2_gain"]), row(lw["ln2_bias"])
    y_p = _combine(base_p, gate_p, dest_p, y_rows, g2, b2).reshape(batch, seq, d)
    y_s = _combine(base_s, gate_s, dest_s, y_rows, g2, b2).reshape(n_s, 1, d)

    lf_p = lft_p.T.reshape(batch, seq, heads)
    hist_p = u_p.reshape(batch, seq, cc)[:, seq - (DW_WIDTH - 1):]
    hist_new_s = jnp.concatenate([hist_s[:, 1:], u_s[:, None, :]], axis=1)
    return (y_p, y_s,
            k_p.reshape(batch, seq, heads, HEAD_DIM), v_p.reshape(batch, seq, heads, HEAD_DIM), lf_p, hist_p,
            k_s.reshape(n_s, 1, heads, HEAD_DIM), v_s.reshape(n_s, 1, heads, HEAD_DIM),
            lf_s.reshape(n_s, 1, heads), hist_new_s)


def kernel(x_prompt, x_sample, cache_k, cache_v, cache_logf, state_conv, page_table, p_prompt, p_sample,
           w_in, b_in, w_dw, b_dw, gn_gain, gn_bias, w_out, ln1_gain, ln1_bias,
           w_router, b_router, w_gate, b_gate, w_up, b_up, w_down, b_down,
           w_ple_gate, b_ple_gate, w_ple_proj, ln2_gain, ln2_bias):
    depth = w_in.shape[0]
    if depth != 1 or x_sample.shape[1] != 1:
        raise NotImplementedError("one layer and one new token per sample sequence are supported")
    alpha = (2.0 * depth) ** 0.25
    lw = dict(w_in=w_in[0], b_in=b_in[0], w_dw=w_dw[0], b_dw=b_dw[0], gn_gain=gn_gain[0], gn_bias=gn_bias[0],
              w_out=w_out[0], ln1_gain=ln1_gain[0], ln1_bias=ln1_bias[0],
              w_router=w_router[0], b_router=b_router[0], w_gate=w_gate[0], b_gate=b_gate[0],
              w_up=w_up[0], b_up=b_up[0], w_down=w_down[0], b_down=b_down[0],
              w_ple_gate=w_ple_gate[0], b_ple_gate=b_ple_gate[0], w_ple_proj=w_ple_proj[0],
              ln2_gain=ln2_gain[0], ln2_bias=ln2_bias[0])
    outs = _layer(x_prompt, x_sample[:, 0], cache_k[0], cache_v[0], cache_logf[0], state_conv[0], page_table,
                  p_prompt[0], p_sample[0], lw, alpha)
    (y_p, y_s, k_p, v_p, lf_p, c_p, k_s, v_s, lf_s, c_s) = outs
    return (y_p, y_s, k_p[None], v_p[None], lf_p[None], c_p[None], k_s[None], v_s[None], lf_s[None], c_s[None])
```

```python
import functools

import jax
import jax.numpy as jnp
from jax import lax
from jax.experimental import pallas as pl
from jax.experimental.pallas import tpu as pltpu

F32 = jnp.float32
BF16 = jnp.bfloat16
I32 = jnp.int32

ATT_HEADS = 8
HEAD_DIM = 64
CONV_GROUPS = 8
DW_WIDTH = 31
N_EXPERTS = 32
TOP_K = 4
SWIGLU_LIMIT = 7.0
SWIGLU_ALPHA = 1.702
LN_EPS = 1e-5
NEG_INF = -1e30
LOG2_E = 1.4426950408889634

LANES = 128
SUBLANES = 8
CONV_HALO = 32
MOE_BLOCK = 256
VMEM_LIMIT = 56 * 1024 * 1024


def _pick(n, cands):
    for c in cands:
        if n % c == 0:
            return c
    raise ValueError(f"no block size in {cands} divides {n}")


def _params(sem, vmem=VMEM_LIMIT):
    return pltpu.CompilerParams(dimension_semantics=sem, vmem_limit_bytes=vmem)


def _sigmoid(x):
    return 1.0 / (1.0 + jnp.exp(-x))


def _log_sigmoid(x):
    return jnp.minimum(x, 0.0) - jnp.log1p(jnp.exp(-jnp.abs(x)))


def _dot(a, b):
    return jnp.dot(a, b, preferred_element_type=F32)


def _dot_nt(a, b):
    return lax.dot_general(a, b, (((1,), (1,)), ((), ())), preferred_element_type=F32)


def _split3(x):
    x1 = x.astype(BF16)
    r1 = x - x1.astype(F32)
    x2 = r1.astype(BF16)
    x3 = (r1 - x2.astype(F32)).astype(BF16)
    return x1, x2, x3


def _dot3(x, m):
    x1, x2, x3 = _split3(x)
    return _dot(x1, m) + _dot(x2, m) + _dot(x3, m)


def _layer_norm(x, g, b):
    mu = jnp.mean(x, axis=-1, keepdims=True)
    d = x - mu
    var = jnp.mean(d * d, axis=-1, keepdims=True)
    return d * lax.rsqrt(var + LN_EPS) * g + b


def _inproj_common(x_ref, w_ref, b_ref, wft_ref, bft_ref, u_ref, q_ref, lft_ref, *, cc, aw, scale):
    x = x_ref[...].astype(BF16)

    def seg(lo, hi):
        return _dot(x, w_ref[:, lo:hi]) + b_ref[:, lo:hi]

    u_ref[...] = seg(0, cc) * _sigmoid(seg(cc, 2 * cc))
    q_ref[...] = (seg(2 * cc, 2 * cc + aw) * scale).astype(BF16)
    lft_ref[...] = _log_sigmoid(_dot_nt(wft_ref[...], x) + bft_ref[...])
    return x


def _inproj_prompt_body(x_ref, w_ref, b_ref, wkv_ref, bkv_ref, wft_ref, bft_ref,
                        u_ref, q_ref, kt_ref, vt_ref, ktb_ref, vtb_ref, lft_ref, *, cc, aw, scale):
    x = _inproj_common(x_ref, w_ref, b_ref, wft_ref, bft_ref, u_ref, q_ref, lft_ref, cc=cc, aw=aw, scale=scale)
    kt = _dot_nt(wkv_ref[0:aw, :], x) + bkv_ref[0:aw, :]
    kt_ref[0] = kt
    ktb_ref[0] = kt.astype(BF16)
    vt = _dot_nt(wkv_ref[aw:2 * aw, :], x) + bkv_ref[aw:2 * aw, :]
    vt_ref[0] = vt
    vtb_ref[0] = vt.astype(BF16)


def _inproj_sample_body(x_ref, w_ref, b_ref, wkv_ref, bkv_ref, wft_ref, bft_ref,
                        u_ref, q_ref, k_ref, v_ref, lft_ref, *, cc, aw, scale):
    x = _inproj_common(x_ref, w_ref, b_ref, wft_ref, bft_ref, u_ref, q_ref, lft_ref, cc=cc, aw=aw, scale=scale)
    k_ref[...] = _dot_nt(x, wkv_ref[0:aw, :]) + bkv_ref[:, 0:aw]
    v_ref[...] = _dot_nt(x, wkv_ref[aw:2 * aw, :]) + bkv_ref[:, aw:2 * aw]


def _in_proj(x, w_b, b_row, wkv_t, bkv, wft_b, bft_col, cc, aw, q_scale, batch=None):
    n, d = x.shape
    nw = w_b.shape[1]
    heads = wft_b.shape[0]
    const = lambda *_: (0, 0)
    w_specs = [pl.BlockSpec((d, nw), const), pl.BlockSpec((1, nw), const),
               pl.BlockSpec((2 * aw, d), const), pl.BlockSpec(bkv.shape, const),
               pl.BlockSpec((heads, d), const), pl.BlockSpec((heads, 1), const)]
    common_shapes = (jax.ShapeDtypeStruct((n, cc), F32),
                     jax.ShapeDtypeStruct((n, aw), BF16))
    lft_shape = jax.ShapeDtypeStruct((heads, n), F32)
    if batch is None:
        tm = _pick(n, (512, 384, 256, 128, 64, 32, 16, 8))
        row = lambda i: (i, 0)
        return pl.pallas_call(
            functools.partial(_inproj_sample_body, cc=cc, aw=aw, scale=q_scale),
            grid=(n // tm,),
            in_specs=[pl.BlockSpec((tm, d), row)] + w_specs,
            out_specs=(pl.BlockSpec((tm, cc), row), pl.BlockSpec((tm, aw), row),
                       pl.BlockSpec((tm, aw), row), pl.BlockSpec((tm, aw), row),
                       pl.BlockSpec((heads, tm), lambda i: (0, i))),
            out_shape=common_shapes + (jax.ShapeDtypeStruct((n, aw), F32), jax.ShapeDtypeStruct((n, aw), F32),
                                       lft_shape),
            compiler_params=_params(("parallel",)),
            name="in_proj_sample",
        )(x, w_b, b_row, wkv_t, bkv, wft_b, bft_col)
    seq = n // batch
    tm = _pick(seq, (512, 256, 128))
    nt = seq // tm
    row = lambda b, i: (b * nt + i, 0)
    tr = lambda b, i: (b, 0, i)
    t_f32 = jax.ShapeDtypeStruct((batch, aw, seq), F32)
    t_b16 = jax.ShapeDtypeStruct((batch, aw, seq), BF16)
    return pl.pallas_call(
        functools.partial(_inproj_prompt_body, cc=cc, aw=aw, scale=q_scale),
        grid=(batch, nt),
        in_specs=[pl.BlockSpec((tm, d), row)] + w_specs,
        out_specs=(pl.BlockSpec((tm, cc), row), pl.BlockSpec((tm, aw), row),
                   pl.BlockSpec((1, aw, tm), tr), pl.BlockSpec((1, aw, tm), tr),
                   pl.BlockSpec((1, aw, tm), tr), pl.BlockSpec((1, aw, tm), tr),
                   pl.BlockSpec((heads, tm), lambda b, i: (0, b * nt + i))),
        out_shape=common_shapes + (t_f32, t_f32, t_b16, t_b16, lft_shape),
        compiler_params=_params(("parallel", "parallel")),
        name="in_proj_prompt",
    )(x, w_b, b_row, wkv_t, bkv, wft_b, bft_col)


def _group_norm_silu(y, m, gg, gb):
    mean = _dot3(y, m)
    d = y - mean
    var = _dot3(d * d, m)
    yn = d * lax.rsqrt(var + LN_EPS) * gg + gb
    return yn * _sigmoid(yn)


def _conv_body(u_ref, halo_ref, w_ref, b_ref, gg_ref, gb_ref, m_ref, o_ref, ext_ref, sh_ref, y_ref, *, ts, rc):
    i = pl.program_id(1)
    c = u_ref.shape[1]
    rows = CONV_HALO + ts
    halo = halo_ref[...]
    ext_ref[0:CONV_HALO, :] = jnp.where(i > 0, halo, jnp.zeros_like(halo))
    ext_ref[CONV_HALO:rows, :] = u_ref[...]
    ext_ref[rows:rows + SUBLANES, :] = jnp.zeros((SUBLANES, c), F32)
    for r in range(SUBLANES):
        sh_ref[r] = ext_ref[r:r + rows, :]
    off = CONV_HALO - (DW_WIDTH - 1)
    for r0 in range(0, ts, rc):
        acc = jnp.broadcast_to(b_ref[...], (rc, c))
        for j in range(DW_WIDTH):
            s = r0 + off + j
            r = s % SUBLANES
            acc = acc + w_ref[j:j + 1, :] * sh_ref[r, s - r:s - r + rc, :]
        y_ref[r0:r0 + rc, :] = acc
    o_ref[...] = _group_norm_silu(y_ref[...], m_ref[...], gg_ref[...], gb_ref[...]).astype(BF16)


def _conv_prompt(u, batch, seq, w_dw, b_row, gg_row, gb_row, gmat):
    n, c = u.shape
    ts = _pick(seq, (512, 256, 128, 64, 32))
    rc = min(ts, 64)
    nt = seq // ts
    hb = ts // CONV_HALO
    rows = CONV_HALO + ts
    const = lambda b, i: (0, 0)
    return pl.pallas_call(
        functools.partial(_conv_body, ts=ts, rc=rc),
        grid=(batch, nt),
        in_specs=[
            pl.BlockSpec((ts, c), lambda b, i: (b * nt + i, 0)),
            pl.BlockSpec((CONV_HALO, c), lambda b, i: (jnp.maximum((b * nt + i) * hb - 1, 0), 0)),
            pl.BlockSpec((DW_WIDTH, c), const),
            pl.BlockSpec((1, c), const),
            pl.BlockSpec((1, c), const),
            pl.BlockSpec((1, c), const),
            pl.BlockSpec((c, c), const),
        ],
        out_specs=pl.BlockSpec((ts, c), lambda b, i: (b * nt + i, 0)),
        out_shape=jax.ShapeDtypeStruct((n, c), BF16),
        scratch_shapes=[pltpu.VMEM((rows + SUBLANES, c), F32), pltpu.VMEM((SUBLANES, rows, c), F32),
                        pltpu.VMEM((ts, c), F32)],
        compiler_params=_params(("parallel", "parallel")),
        name="conv_prompt",
    )(u, u, w_dw, b_row, gg_row, gb_row, gmat)


def _conv_sample_body(hist_ref, u_ref, w_ref, b_ref, gg_ref, gb_ref, m_ref, o_ref):
    acc = b_ref[...] + w_ref[DW_WIDTH - 1:DW_WIDTH, :] * u_ref[...]
    for j in range(DW_WIDTH - 1):
        acc = acc + w_ref[j:j + 1, :] * hist_ref[j]
    o_ref[...] = _group_norm_silu(acc, m_ref[...], gg_ref[...], gb_ref[...]).astype(BF16)


def _conv_sample(hist_t, u, w_dw, b_row, gg_row, gb_row, gmat):
    n, c = u.shape
    full2 = lambda i: (0, 0)
    return pl.pallas_call(
        _conv_sample_body,
        grid=(1,),
        in_specs=[
            pl.BlockSpec(hist_t.shape, lambda i: (0, 0, 0)),
            pl.BlockSpec((n, c), full2),
            pl.BlockSpec((DW_WIDTH, c), full2),
            pl.BlockSpec((1, c), full2),
            pl.BlockSpec((1, c), full2),
            pl.BlockSpec((1, c), full2),
            pl.BlockSpec((c, c), full2),
        ],
        out_specs=pl.BlockSpec((n, c), full2),
        out_shape=jax.ShapeDtypeStruct((n, c), BF16),
        compiler_params=_params(("arbitrary",)),
        name="conv_sample",
    )(hist_t, u, w_dw, b_row, gg_row, gb_row, gmat)


def _cumsum_body(lf_ref, tri_ref, o_ref, carry_ref):
    @pl.when(pl.program_id(1) == 0)
    def _():
        carry_ref[...] = jnp.zeros_like(carry_ref)

    cl = lf_ref.shape[1]
    cum = _dot3(lf_ref[...], tri_ref[...]) + carry_ref[:, 0:1]
    o_ref[...] = -LOG2_E * cum
    carry_ref[...] = jnp.broadcast_to(cum[:, cl - 1:cl], carry_ref.shape)


def _neg_cumsum(lft, batch, seq):
    heads, n = lft.shape
    cl = _pick(seq, (512, 256, 128))
    nc = seq // cl
    tri = jnp.triu(jnp.ones((cl, cl), F32)).astype(BF16)
    return pl.pallas_call(
        _cumsum_body,
        grid=(batch, nc),
        in_specs=[pl.BlockSpec((heads, cl), lambda b, c: (0, b * nc + c)),
                  pl.BlockSpec((cl, cl), lambda b, c: (0, 0))],
        out_specs=pl.BlockSpec((heads, cl), lambda b, c: (0, b * nc + c)),
        out_shape=jax.ShapeDtypeStruct((heads, n), F32),
        scratch_shapes=[pltpu.VMEM((heads, LANES), F32)],
        compiler_params=_params(("parallel", "arbitrary")),
        name="forget_cumsum",
    )(lft, tri)


def _att_body(qi_ref, ki_ref, q_ref, k_ref, v_ref, nd_ref, o_ref, m_ref, l_ref, acc_ref, *, hd):
    step = pl.program_id(2)
    qi = qi_ref[step]
    ki = ki_ref[step]
    tq = q_ref.shape[0]
    tk = k_ref.shape[2]
    is_a = lax.broadcasted_iota(I32, (1, LANES), 1) < hd

    @pl.when(ki == 0)
    def _():
        m_ref[...] = jnp.full(m_ref.shape, NEG_INF, F32)
        l_ref[...] = jnp.zeros_like(l_ref)
        acc_ref[...] = jnp.zeros_like(acc_ref)

    def process(diagonal):
        q = q_ref[...]
        kt = k_ref[0]
        vt = v_ref[0]
        nd = nd_ref[0]
        zero = jnp.zeros_like(q)
        pvs = []
        alphas = []
        for h in range(2):
            qh = jnp.where(is_a, q, zero) if h == 0 else jnp.where(is_a, zero, q)
            s = _dot(qh, kt) + nd[h:h + 1, :]
            if diagonal:
                row = lax.broadcasted_iota(I32, (tq, tk), 0)
                col = lax.broadcasted_iota(I32, (tq, tk), 1)
                s = jnp.where(col <= row, s, NEG_INF)
            m_old = m_ref[h]
            m_new = jnp.maximum(m_old, jnp.max(s, axis=1, keepdims=True))
            alpha = jnp.exp2(m_old - m_new)
            p = jnp.exp2(s - jnp.tile(m_new, (1, tk // LANES)))
            l_ref[h] = alpha * l_ref[h] + jnp.sum(p, axis=1, keepdims=True)
            m_ref[h] = m_new
            pvs.append(_dot_nt(p.astype(BF16), vt))
            alphas.append(alpha)
        acc_ref[...] = (acc_ref[...] * jnp.where(is_a, alphas[0], alphas[1])
                        + jnp.where(is_a, pvs[0], pvs[1]))

    @pl.when(ki < qi)
    def _():
        process(False)

    @pl.when(ki == qi)
    def _():
        process(True)
        o_ref[...] = (acc_ref[...] / jnp.where(is_a, l_ref[0], l_ref[1])).astype(BF16)


def _attention_prompt(q_b, kt_b, vt_b, negd, batch, seq):
    n, aw = q_b.shape
    pairs = aw // LANES
    t = _pick(seq, (512, 256, 128))
    nq = seq // t
    steps = [(a, b) for a in range(nq) for b in range(a + 1)]
    qi_tab = jnp.asarray([s[0] for s in steps], I32)
    ki_tab = jnp.asarray([s[1] for s in steps], I32)
    grid_spec = pltpu.PrefetchScalarGridSpec(
        num_scalar_prefetch=2,
        grid=(batch, pairs, len(steps)),
        in_specs=[
            pl.BlockSpec((t, LANES), lambda b, hp, s, qt, kt: (b * nq + qt[s], hp)),
            pl.BlockSpec((1, LANES, t), lambda b, hp, s, qt, kt: (b, hp, kt[s])),
            pl.BlockSpec((1, LANES, t), lambda b, hp, s, qt, kt: (b, hp, kt[s])),
            pl.BlockSpec((1, 2, t), lambda b, hp, s, qt, kt: (hp, 0, b * nq + kt[s])),
        ],
        out_specs=pl.BlockSpec((t, LANES), lambda b, hp, s, qt, kt: (b * nq + qt[s], hp)),
        scratch_shapes=[pltpu.VMEM((2, t, LANES), F32), pltpu.VMEM((2, t, LANES), F32),
                        pltpu.VMEM((t, LANES), F32)],
    )
    return pl.pallas_call(
        functools.partial(_att_body, hd=HEAD_DIM),
        grid_spec=grid_spec,
        out_shape=jax.ShapeDtypeStruct((n, aw), BF16),
        compiler_params=_params(("parallel", "parallel", "arbitrary")),
        name="attention_prompt",
    )(qi_tab, ki_tab, q_b, kt_b, vt_b, negd)


def _pool_suffix_body(lf_ref, tri_ref, ones_ref, w_ref, t_ref):
    x = lf_ref[...]
    w_ref[...] = _dot3(x, tri_ref[...])
    t_ref[...] = _dot3(x, ones_ref[...])


def _pool_suffix(lf_rows):
    n, page = lf_rows.shape
    tp = _pick(n, (2048, 1024, 512, 256, 128, 64, 32, 16, 8, n))
    tri = jnp.tril(jnp.ones((page, page), F32), -1).astype(BF16)
    ones = jnp.ones((page, page), BF16)
    blk = pl.BlockSpec((tp, page), lambda i: (i, 0))
    mat = pl.BlockSpec((page, page), lambda i: (0, 0))
    return pl.pallas_call(
        _pool_suffix_body,
        grid=(n // tp,),
        in_specs=[blk, mat, mat],
        out_specs=(blk, blk),
        out_shape=(jax.ShapeDtypeStruct((n, page), F32), jax.ShapeDtypeStruct((n, page), F32)),
        compiler_params=_params(("parallel",)),
        name="pool_forget_suffix",
    )(lf_rows, tri, ones)


def _satt_body(pt_ref, *refs, n_pages, heads):
    k_refs = refs[:n_pages]
    v_refs = refs[n_pages:2 * n_pages]
    w_refs = refs[2 * n_pages:3 * n_pages]
    t_refs = refs[3 * n_pages:4 * n_pages]
    q_ref, kn_ref, vn_ref, lfn_ref, o_ref, s_ref = refs[4 * n_pages:]
    hd, page = k_refs[0].shape[2:]
    q = q_ref[0].astype(F32)
    s_new = jnp.sum(q * kn_ref[0], axis=1, keepdims=True)
    q_cols = [jnp.broadcast_to(q[h:h + 1, :], (page, hd)).T for h in range(heads)]
    later = lfn_ref[0]
    m = s_new
    for j in reversed(range(n_pages)):
        for h in range(heads):
            s_ref[j, h:h + 1, :] = jnp.sum(k_refs[j][0, h] * q_cols[h], axis=0, keepdims=True)
        s = s_ref[j] + (w_refs[j][0] + later)
        s_ref[j] = s
        m = jnp.maximum(m, jnp.max(s, axis=1, keepdims=True))
        later = later + t_refs[j][0]
    p_new = jnp.exp(s_new - m)
    l = p_new
    for j in range(n_pages):
        p = jnp.exp(s_ref[j] - m)
        s_ref[j] = p
        l = l + jnp.sum(p, axis=1, keepdims=True)
    vn = vn_ref[0]
    for h in range(heads):
        acc = jnp.zeros((hd, page), F32)
        for j in range(n_pages):
            acc = acc + v_refs[j][0, h] * s_ref[j, h:h + 1, :]
        o = jnp.sum(acc.T, axis=0, keepdims=True) + p_new[h:h + 1, :] * vn[h:h + 1, :]
        o_ref[0, h:h + 1, :] = (o / l[h:h + 1, :]).astype(BF16)


def _attention_sample(q_b, k_new, v_new, lf_new, cache_kt, cache_vt, cache_lft, page_table):
    n, aw = q_b.shape
    n_pages = page_table.shape[1]
    pool, heads, hd, page = cache_kt.shape
    within, totals = _pool_suffix(cache_lft.reshape(pool * heads, page))
    within = within.reshape(pool, heads, page)
    totals = totals.reshape(pool, heads, page)
    lfn_lanes = jnp.broadcast_to(lf_new[:, :, None], (n, heads, page))

    def kv_spec(j):
        return pl.BlockSpec((1, heads, hd, page), lambda i, pt, j=j: (pt[i, j], 0, 0, 0))

    def row_spec(j):
        return pl.BlockSpec((1, heads, page), lambda i, pt, j=j: (pt[i, j], 0, 0))

    per_seq = lambda shape: pl.BlockSpec((1,) + shape, lambda i, pt: (i, 0, 0))
    grid_spec = pltpu.PrefetchScalarGridSpec(
        num_scalar_prefetch=1,
        grid=(n,),
        in_specs=([kv_spec(j) for j in range(n_pages)] + [kv_spec(j) for j in range(n_pages)]
                  + [row_spec(j) for j in range(n_pages)] + [row_spec(j) for j in range(n_pages)]
                  + [per_seq((heads, hd)), per_seq((heads, hd)), per_seq((heads, hd)), per_seq((heads, page))]),
        out_specs=per_seq((heads, hd)),
        scratch_shapes=[pltpu.VMEM((n_pages, heads, page), F32)],
    )
    out = pl.pallas_call(
        functools.partial(_satt_body, n_pages=n_pages, heads=heads),
        grid_spec=grid_spec,
        out_shape=jax.ShapeDtypeStruct((n, heads, hd), BF16),
        compiler_params=_params(("arbitrary",)),
        name="attention_sample",
    )(page_table, *([cache_kt] * n_pages), *([cache_vt] * n_pages), *([within] * n_pages), *([totals] * n_pages),
      q_b.reshape(n, heads, hd), k_new.reshape(n, heads, hd), v_new.reshape(n, heads, hd), lfn_lanes)
    return out.reshape(n, aw)


def _mix_body(x_ref, c_ref, a_ref, p_ref, wo_ref, g1_ref, b1_ref, wrh_ref, wrl_ref, br_ref,
              wpg_ref, bpg_ref, wpp_ref, h_ref, base_ref, idx_ref, gate_ref, cnt_ref, *, alpha, cc):
    mix = _dot(c_ref[...], wo_ref[:cc, :]) + _dot(a_ref[...], wo_ref[cc:, :])
    h = _layer_norm(alpha * x_ref[...] + mix, g1_ref[...], b1_ref[...])
    h_ref[...] = h
    hb = h.astype(BF16)
    hl = (h - hb.astype(F32)).astype(BF16)
    wrh = wrh_ref[...]
    logits = _dot(hb, wrh) + _dot(hl, wrh) + _dot(hb, wrl_ref[...]) + br_ref[...]

    tm = logits.shape[0]
    lane = lax.broadcasted_iota(I32, (tm, LANES), 1).astype(F32)
    vals, idxs = [], []
    l = logits
    for _ in range(TOP_K):
        m = jnp.max(l, axis=1, keepdims=True)
        ix = jnp.min(jnp.where(l == m, lane, float(LANES)), axis=1, keepdims=True)
        vals.append(m)
        idxs.append(ix)
        l = jnp.where(lane == ix, -jnp.inf, l)
    es = [jnp.exp(v - vals[0]) for v in vals]
    den = es[0] + es[1] + es[2] + es[3]
    idx_w = jnp.zeros((tm, LANES), F32)
    gate_w = jnp.zeros((tm, LANES), F32)
    sel = jnp.zeros((tm, LANES), F32)
    for r in range(TOP_K):
        idx_w = jnp.where(lane == float(r), idxs[r], idx_w)
        gate_w = jnp.where(lane == float(r), es[r] / den, gate_w)
        sel = sel + (lane == idxs[r]).astype(F32)
    idx_ref[...] = idx_w
    gate_ref[...] = gate_w

    @pl.when(pl.program_id(0) == 0)
    def _():
        cnt_ref[...] = jnp.zeros_like(cnt_ref)

    cnt_ref[...] += jnp.sum(sel, axis=0, keepdims=True)

    ple = _sigmoid(_dot(hb, wpg_ref[...]) + bpg_ref[...]) * _dot(p_ref[...].astype(BF16), wpp_ref[...])
    base_ref[...] = alpha * h + ple


def _mix(x, conv_b, att_b, p, wo_b, g1, b1, wr_hi, wr_lo, br_row, wpg_b, bpg, wpp_b, alpha):
    n, d = x.shape
    cc = conv_b.shape[1]
    aw = att_b.shape[1]
    pd = p.shape[1]
    tm = _pick(n, (512, 256, 128, 64, 32, 16, 8))
    row = lambda i: (i, 0)
    const = lambda i: (0, 0)
    outs = (
        jax.ShapeDtypeStruct((n, d), F32),
        jax.ShapeDtypeStruct((n, d), F32),
        jax.ShapeDtypeStruct((n, LANES), F32),
        jax.ShapeDtypeStruct((n, LANES), F32),
        jax.ShapeDtypeStruct((1, LANES), F32),
    )
    return pl.pallas_call(
        functools.partial(_mix_body, alpha=alpha, cc=cc),
        grid=(n // tm,),
        in_specs=[
            pl.BlockSpec((tm, d), row), pl.BlockSpec((tm, cc), row), pl.BlockSpec((tm, aw), row),
            pl.BlockSpec((tm, pd), row),
            pl.BlockSpec((d, d), const), pl.BlockSpec((1, d), const), pl.BlockSpec((1, d), const),
            pl.BlockSpec((d, LANES), const), pl.BlockSpec((d, LANES), const), pl.BlockSpec((1, LANES), const),
            pl.BlockSpec((d, d), const), pl.BlockSpec((1, d), const), pl.BlockSpec((pd, d), const),
        ],
        out_specs=(
            pl.BlockSpec((tm, d), row), pl.BlockSpec((tm, d), row),
            pl.BlockSpec((tm, LANES), row), pl.BlockSpec((tm, LANES), row),
            pl.BlockSpec((1, LANES), const),
        ),
        out_shape=outs,
        compiler_params=_params(("arbitrary",)),
        name="out_proj_router",
    )(x, conv_b, att_b, p, wo_b, g1, b1, wr_hi, wr_lo, br_row, wpg_b, bpg, wpp_b)


def _positions_body(idx_ref, start_ref, tri_ref, o_ref, carry_ref):
    @pl.when(pl.program_id(0) == 0)
    def _():
        carry_ref[...] = jnp.zeros_like(carry_ref)

    ids = idx_ref[...]
    tm = ids.shape[0]
    lane = lax.broadcasted_iota(I32, (tm, LANES), 1).astype(F32)
    hot = [lane == ids[:, r:r + 1] for r in range(TOP_K)]
    sel = jnp.zeros((tm, LANES), F32)
    for r in range(TOP_K):
        sel = sel + hot[r].astype(F32)
    rank = _dot(tri_ref[...], sel.astype(BF16)) + carry_ref[...]
    pos = rank + start_ref[...]
    out = jnp.zeros((tm, LANES), F32)
    for r in range(TOP_K):
        d = jnp.sum(jnp.where(hot[r], pos, 0.0), axis=1, keepdims=True)
        out = jnp.where(lane == float(r), d, out)
    o_ref[...] = out.astype(I32)
    carry_ref[...] += jnp.sum(sel, axis=0, keepdims=True)


def _positions(idx_w, start_row):
    n = idx_w.shape[0]
    tm = _pick(n, (512, 384, 256, 128, 64, 32, 16, 8))
    tri = jnp.tril(jnp.ones((tm, tm), F32), -1).astype(BF16)
    return pl.pallas_call(
        _positions_body,
        grid=(n // tm,),
        in_specs=[pl.BlockSpec((tm, LANES), lambda i: (i, 0)),
                  pl.BlockSpec((1, LANES), lambda i: (0, 0)),
                  pl.BlockSpec((tm, tm), lambda i: (0, 0))],
        out_specs=pl.BlockSpec((tm, LANES), lambda i: (i, 0)),
        out_shape=jax.ShapeDtypeStruct((n, LANES), I32),
        scratch_shapes=[pltpu.VMEM((1, LANES), F32)],
        compiler_params=_params(("arbitrary",)),
        name="route_positions",
    )(idx_w, start_row, tri)


def _dispatch_body(dest_ref, h_ref, xs_in_ref, xs_ref, sem, *, tb):
    del xs_in_ref

    def row_copy(t, d):
        return pltpu.make_async_copy(h_ref.at[pl.ds(t, 1)], xs_ref.at[pl.ds(d, 1)], sem)

    def issue(t, c):
        for r in range(TOP_K):
            row_copy(t, dest_ref[0, 0, t * TOP_K + r]).start()
        return c

    lax.fori_loop(0, tb, issue, 0)

    def drain(t, c):
        for r in range(TOP_K):
            row_copy(t, dest_ref[0, 0, t * TOP_K + r]).wait()
        return c

    lax.fori_loop(0, tb, drain, 0)


def _dispatch(h, dest, xs):
    n, d = h.shape
    tb = _pick(n, (256, 128, 64, 32, 16, 8))
    dest3 = dest.reshape(n // tb, 1, tb * TOP_K)
    return pl.pallas_call(
        functools.partial(_dispatch_body, tb=tb),
        grid=(n // tb,),
        in_specs=[
            pl.BlockSpec((1, 1, tb * TOP_K), lambda i: (i, 0, 0), memory_space=pltpu.SMEM),
            pl.BlockSpec((tb, d), lambda i: (i, 0)),
            pl.BlockSpec(memory_space=pl.ANY),
        ],
        out_specs=pl.BlockSpec(memory_space=pl.ANY),
        out_shape=jax.ShapeDtypeStruct(xs.shape, xs.dtype),
        scratch_shapes=[pltpu.SemaphoreType.DMA],
        input_output_aliases={2: 0},
        compiler_params=_params(("arbitrary",)),
        name="moe_dispatch",
    )(dest3, h, xs)


def _expert_body(be_ref, nu_ref, x_ref, wg_ref, bg_ref, wu_ref, bu_ref, wd_ref, bd_ref, o_ref,
                 wgb_ref, wub_ref, wdb_ref):
    i = pl.program_id(0)

    @pl.when(i < nu_ref[0])
    def _():
        @pl.when((i == 0) | (be_ref[i] != be_ref[jnp.maximum(i - 1, 0)]))
        def _():
            wgb_ref[...] = wg_ref[0].astype(BF16)
            wub_ref[...] = wu_ref[0].astype(BF16)
            wdb_ref[...] = wd_ref[0].astype(BF16)

        x = x_ref[...].astype(BF16)
        g = jnp.minimum(_dot(x, wgb_ref[...]) + bg_ref[0], SWIGLU_LIMIT)
        u = jnp.clip(_dot(x, wub_ref[...]) + bu_ref[0], -SWIGLU_LIMIT, SWIGLU_LIMIT)
        hidden = (u + 1.0) * (g * _sigmoid(SWIGLU_ALPHA * g))
        o_ref[...] = _dot(hidden.astype(BF16), wdb_ref[...]) + bd_ref[0]

    @pl.when(i >= nu_ref[0])
    def _():
        o_ref[...] = jnp.zeros_like(o_ref)


def _experts(xs, block_e, n_used, w_gate, b_gate, w_up, b_up, w_down, b_down):
    rows, d = xs.shape
    n_e, _, de = w_gate.shape
    n_blocks = rows // MOE_BLOCK
    last = lambda i, be, nu: (jnp.minimum(i, nu[0] - 1), 0)
    wsel = lambda i, be, nu: (be[i], 0, 0)
    grid_spec = pltpu.PrefetchScalarGridSpec(
        num_scalar_prefetch=2,
        grid=(n_blocks,),
        in_specs=[
            pl.BlockSpec((MOE_BLOCK, d), last),
            pl.BlockSpec((1, d, de), wsel), pl.BlockSpec((1, 1, de), wsel),
            pl.BlockSpec((1, d, de), wsel), pl.BlockSpec((1, 1, de), wsel),
            pl.BlockSpec((1, de, d), wsel), pl.BlockSpec((1, 1, d), wsel),
        ],
        out_specs=pl.BlockSpec((MOE_BLOCK, d), lambda i, be, nu: (i, 0)),
        scratch_shapes=[pltpu.VMEM((d, de), BF16), pltpu.VMEM((d, de), BF16), pltpu.VMEM((de, d), BF16)],
    )
    return pl.pallas_call(
        _expert_body,
        grid_spec=grid_spec,
        out_shape=jax.ShapeDtypeStruct((rows, d), F32),
        compiler_params=_params(("arbitrary",)),
        name="moe_experts",
    )(block_e, n_used, xs, w_gate, b_gate[:, None, :], w_up, b_up[:, None, :], w_down, b_down[:, None, :])


def _combine_body(dest_ref, base_ref, gate_ref, y_ref, g2_ref, b2_ref, o_ref, buf_ref, sem, *, tb):
    def row_copy(t, r, d):
        return pltpu.make_async_copy(y_ref.at[pl.ds(d, 1)], buf_ref.at[r, pl.ds(t, 1)], sem)

    def issue(t, c):
        for r in range(TOP_K):
            row_copy(t, r, dest_ref[0, 0, t * TOP_K + r]).start()
        return c

    lax.fori_loop(0, tb, issue, 0)

    def drain(t, c):
        for r in range(TOP_K):
            row_copy(t, r, dest_ref[0, 0, t * TOP_K + r]).wait()
        return c

    lax.fori_loop(0, tb, drain, 0)

    gate = gate_ref[...]
    y = base_ref[...]
    for r in range(TOP_K):
        y = y + gate[:, r:r + 1] * buf_ref[r]
    o_ref[...] = _layer_norm(y, g2_ref[...], b2_ref[...])


def _combine(base, gate_w, dest, y_rows, g2, b2):
    n, d = base.shape
    tb = _pick(n, (128, 64, 32, 16, 8))
    dest3 = dest.reshape(n // tb, 1, tb * TOP_K)
    row = lambda i: (i, 0)
    const = lambda i: (0, 0)
    return pl.pallas_call(
        functools.partial(_combine_body, tb=tb),
        grid=(n // tb,),
        in_specs=[
            pl.BlockSpec((1, 1, tb * TOP_K), lambda i: (i, 0, 0), memory_space=pltpu.SMEM),
            pl.BlockSpec((tb, d), row),
            pl.BlockSpec((tb, LANES), row),
            pl.BlockSpec(memory_space=pl.ANY),
            pl.BlockSpec((1, d), const), pl.BlockSpec((1, d), const),
        ],
        out_specs=pl.BlockSpec((tb, d), row),
        out_shape=jax.ShapeDtypeStruct((n, d), F32),
        scratch_shapes=[pltpu.VMEM((TOP_K, tb, d), F32), pltpu.SemaphoreType.DMA],
        compiler_params=_params(("arbitrary",)),
        name="moe_combine",
    )(dest3, base, gate_w, y_rows, g2, b2)


def _layer(x_p, x_s, ck, cv, clf, hist_s, page_table, p_p, p_s, lw, alpha):
    batch, seq, d = x_p.shape
    n_s = x_s.shape[0]
    n_p = batch * seq
    cc = lw["w_dw"].shape[1]
    aw = ATT_HEADS * HEAD_DIM
    heads = ATT_HEADS
    nz = 2 * cc + 3 * aw
    row = lambda v: v[None, :]

    nq_cols = 2 * cc + aw
    w_in_t = lw["w_in"].T
    w_in_b = lw["w_in"][:, :nq_cols].astype(BF16)
    b_in_row = row(lw["b_in"][:nq_cols])
    wkv_t = w_in_t[nq_cols:nz].astype(BF16)
    bkv = lw["b_in"][nq_cols:nz]
    wft_b = w_in_t[nz:].astype(BF16)
    bft_col = lw["b_in"][nz:, None]
    b_dw, gg, gb = row(lw["b_dw"]), row(lw["gn_gain"]), row(lw["gn_bias"])
    gsz = cc // CONV_GROUPS
    grp = jnp.arange(cc) // gsz
    gmat = ((grp[:, None] == grp[None, :]).astype(F32) / gsz).astype(BF16)
    wo_b = lw["w_out"].astype(BF16)
    wr = jnp.pad(lw["w_router"], ((0, 0), (0, LANES - N_EXPERTS)))
    wr_hi = wr.astype(BF16)
    wr_lo = (wr - wr_hi.astype(F32)).astype(BF16)
    br_row = row(jnp.pad(lw["b_router"], (0, LANES - N_EXPERTS), constant_values=NEG_INF))
    wpg_b = lw["w_ple_gate"].astype(BF16)
    wpp_b = lw["w_ple_proj"].astype(BF16)

    xp2 = x_p.reshape(n_p, d)
    xs2 = x_s.reshape(n_s, d)

    u_p, q_p, kt_p, vt_p, ktb_p, vtb_p, lft_p = _in_proj(xp2, w_in_b, b_in_row, wkv_t, bkv[:, None], wft_b, bft_col,
                                                         cc, aw, HEAD_DIM ** -0.5 * LOG2_E, batch=batch)
    conv_p = _conv_prompt(u_p, batch, seq, lw["w_dw"], b_dw, gg, gb, gmat)
    negd = _neg_cumsum(lft_p, batch, seq).reshape(heads // 2, 2, n_p)
    att_p = _attention_prompt(q_p, ktb_p, vtb_p, negd, batch, seq)
    h_p, base_p, idx_p, gate_p, cnt_p = _mix(xp2, conv_p, att_p, p_p.reshape(n_p, -1), wo_b,
                                             row(lw["ln1_gain"]), row(lw["ln1_bias"]), wr_hi, wr_lo, br_row,
                                             wpg_b, row(lw["b_ple_gate"]), wpp_b, alpha)

    u_s, q_s, k_s, v_s, lft_s = _in_proj(xs2, w_in_b, b_in_row, wkv_t, bkv[None, :], wft_b, bft_col,
                                         cc, aw, HEAD_DIM ** -0.5)
    conv_s = _conv_sample(hist_s.transpose(1, 0, 2), u_s, lw["w_dw"], b_dw, gg, gb, gmat)
    lf_s = lft_s.T
    att_s = _attention_sample(q_s, k_s, v_s, lf_s, ck.transpose(0, 2, 3, 1), cv.transpose(0, 2, 3, 1),
                              clf.transpose(0, 2, 1), page_table)
    h_s, base_s, idx_s, gate_s, cnt_s = _mix(xs2, conv_s, att_s, p_s.reshape(n_s, -1), wo_b,
                                             row(lw["ln1_gain"]), row(lw["ln1_bias"]), wr_hi, wr_lo, br_row,
                                             wpg_b, row(lw["b_ple_gate"]), wpp_b, alpha)

    n_tok = n_p + n_s
    counts = (cnt_p + cnt_s)[0].astype(I32)
    padded = (counts + MOE_BLOCK - 1) // MOE_BLOCK * MOE_BLOCK
    pad_end = jnp.cumsum(padded)
    pad_start = pad_end - padded
    n_blocks = -(-(n_tok * TOP_K) // MOE_BLOCK) + N_EXPERTS
    n_used = (pad_end[N_EXPERTS - 1] // MOE_BLOCK).astype(I32).reshape(1)
    block_row0 = jnp.arange(n_blocks, dtype=I32) * MOE_BLOCK
    block_e = jnp.minimum(jnp.sum((pad_end[None, :N_EXPERTS] <= block_row0[:, None]).astype(I32), axis=1),
                          N_EXPERTS - 1)
    dest = _positions(jnp.concatenate([idx_p, idx_s], axis=0), pad_start.astype(F32)[None, :])[:, :TOP_K]
    dest_p, dest_s = dest[:n_p], dest[n_p:]

    xs_rows = jnp.zeros((n_blocks * MOE_BLOCK, d), F32)
    xs_rows = _dispatch(h_p, dest_p, xs_rows)
    xs_rows = _dispatch(h_s, dest_s, xs_rows)
    y_rows = _experts(xs_rows, block_e, n_used, lw["w_gate"], lw["b_gate"], lw["w_up"], lw["b_up"],
                      lw["w_down"], lw["b_down"])
    g2, b2 = row(lw["ln2_gain"]), row(lw["ln2_bias"])
    y_p = _combine(base_p, gate_p, dest_p, y_rows, g2, b2).reshape(batch, seq, d)
    y_s = _combine(base_s, gate_s, dest_s, y_rows, g2, b2).reshape(n_s, 1, d)

    lf_p = lft_p.T.reshape(batch, seq, heads)
    hist_p = u_p.reshape(batch, seq, cc)[:, seq - (DW_WIDTH - 1):]
    hist_new_s = jnp.concatenate([hist_s[:, 1:], u_s[:, None, :]], axis=1)
    return (y_p, y_s,
            kt_p.reshape(batch, heads, HEAD_DIM, seq).transpose(0, 3, 1, 2),
            vt_p.reshape(batch, heads, HEAD_DIM, seq).transpose(0, 3, 1, 2), lf_p, hist_p,
            k_s.reshape(n_s, 1, heads, HEAD_DIM), v_s.reshape(n_s, 1, heads, HEAD_DIM),
            lf_s.reshape(n_s, 1, heads), hist_new_s)


def kernel(x_prompt, x_sample, cache_k, cache_v, cache_logf, state_conv, page_table, p_prompt, p_sample,
           w_in, b_in, w_dw, b_dw, gn_gain, gn_bias, w_out, ln1_gain, ln1_bias,
           w_router, b_router, w_gate, b_gate, w_up, b_up, w_down, b_down,
           w_ple_gate, b_ple_gate, w_ple_proj, ln2_gain, ln2_bias):
    depth = w_in.shape[0]
    if depth != 1 or x_sample.shape[1] != 1:
        raise NotImplementedError("one layer and one new token per sample sequence are supported")
    alpha = (2.0 * depth) ** 0.25
    lw = dict(w_in=w_in[0], b_in=b_in[0], w_dw=w_dw[0], b_dw=b_dw[0], gn_gain=gn_gain[0], gn_bias=gn_bias[0],
              w_out=w_out[0], ln1_gain=ln1_gain[0], ln1_bias=ln1_bias[0],
              w_router=w_router[0], b_router=b_router[0], w_gate=w_gate[0], b_gate=b_gate[0],
              w_up=w_up[0], b_up=b_up[0], w_down=w_down[0], b_down=b_down[0],
              w_ple_gate=w_ple_gate[0], b_ple_gate=b_ple_gate[0], w_ple_proj=w_ple_proj[0],
              ln2_gain=ln2_gain[0], ln2_bias=ln2_bias[0])
    outs = _layer(x_prompt, x_sample[:, 0], cache_k[0], cache_v[0], cache_logf[0], state_conv[0], page_table,
                  p_prompt[0], p_sample[0], lw, alpha)
    (y_p, y_s, k_p, v_p, lf_p, c_p, k_s, v_s, lf_s, c_s) = outs
    return (y_p, y_s, k_p[None], v_p[None], lf_p[None], c_p[None], k_s[None], v_s[None], lf_s[None], c_s[None])
```

```python
import functools

import jax
import jax.numpy as jnp
from jax import lax
from jax.experimental import pallas as pl
from jax.experimental.pallas import tpu as pltpu

F32 = jnp.float32
BF16 = jnp.bfloat16
I32 = jnp.int32

ATT_HEADS = 8
HEAD_DIM = 64
CONV_GROUPS = 8
DW_WIDTH = 31
N_EXPERTS = 32
TOP_K = 4
SWIGLU_LIMIT = 7.0
SWIGLU_ALPHA = 1.702
LN_EPS = 1e-5
NEG_INF = -1e30
LOG2_E = 1.4426950408889634

LANES = 128
SUBLANES = 8
CONV_HALO = 32
MOE_BLOCK = 256
ATT_PAIRS_PER_STEP = 4
VMEM_LIMIT = 56 * 1024 * 1024


def _pick(n, cands):
    for c in cands:
        if n % c == 0:
            return c
    raise ValueError(f"no block size in {cands} divides {n}")


def _params(sem, vmem=VMEM_LIMIT):
    return pltpu.CompilerParams(dimension_semantics=sem, vmem_limit_bytes=vmem)


def _sigmoid(x):
    return 1.0 / (1.0 + jnp.exp(-x))


def _log_sigmoid(x):
    return jnp.minimum(x, 0.0) - jnp.log1p(jnp.exp(-jnp.abs(x)))


def _dot(a, b):
    return jnp.dot(a, b, preferred_element_type=F32)


def _dot_nt(a, b):
    return lax.dot_general(a, b, (((1,), (1,)), ((), ())), preferred_element_type=F32)


def _split3(x):
    x1 = x.astype(BF16)
    r1 = x - x1.astype(F32)
    x2 = r1.astype(BF16)
    x3 = (r1 - x2.astype(F32)).astype(BF16)
    return x1, x2, x3


def _dot3(x, m):
    x1, x2, x3 = _split3(x)
    return _dot(x1, m) + _dot(x2, m) + _dot(x3, m)


def _layer_norm(x, g, b):
    mu = jnp.mean(x, axis=-1, keepdims=True)
    d = x - mu
    var = jnp.mean(d * d, axis=-1, keepdims=True)
    return d * lax.rsqrt(var + LN_EPS) * g + b


def _inproj_common(x_ref, w_ref, b_ref, wft_ref, bft_ref, u_ref, q_ref, lft_ref, *, cc, aw, scale):
    x = x_ref[...].astype(BF16)

    def seg(lo, hi):
        return _dot(x, w_ref[:, lo:hi]) + b_ref[:, lo:hi]

    u_ref[...] = seg(0, cc) * _sigmoid(seg(cc, 2 * cc))
    q_ref[...] = (seg(2 * cc, 2 * cc + aw) * scale).astype(BF16)
    lft_ref[...] = _log_sigmoid(_dot_nt(wft_ref[...], x) + bft_ref[...])
    return x


def _inproj_prompt_body(x_ref, w_ref, b_ref, wkv_ref, bkv_ref, wft_ref, bft_ref,
                        u_ref, q_ref, kt_ref, vt_ref, ktb_ref, vtb_ref, lft_ref, *, cc, aw, scale):
    x = _inproj_common(x_ref, w_ref, b_ref, wft_ref, bft_ref, u_ref, q_ref, lft_ref, cc=cc, aw=aw, scale=scale)
    kt = _dot_nt(wkv_ref[0:aw, :], x) + bkv_ref[0:aw, :]
    kt_ref[0] = kt
    ktb_ref[0] = kt.astype(BF16)
    vt = _dot_nt(wkv_ref[aw:2 * aw, :], x) + bkv_ref[aw:2 * aw, :]
    vt_ref[0] = vt
    vtb_ref[0] = vt.astype(BF16)


def _inproj_sample_body(x_ref, w_ref, b_ref, wkv_ref, bkv_ref, wft_ref, bft_ref,
                        u_ref, q_ref, k_ref, v_ref, lft_ref, *, cc, aw, scale):
    x = _inproj_common(x_ref, w_ref, b_ref, wft_ref, bft_ref, u_ref, q_ref, lft_ref, cc=cc, aw=aw, scale=scale)
    k_ref[...] = _dot_nt(x, wkv_ref[0:aw, :]) + bkv_ref[:, 0:aw]
    v_ref[...] = _dot_nt(x, wkv_ref[aw:2 * aw, :]) + bkv_ref[:, aw:2 * aw]


def _in_proj(x, w_b, b_row, wkv_t, bkv, wft_b, bft_col, cc, aw, q_scale, batch=None):
    n, d = x.shape
    nw = w_b.shape[1]
    heads = wft_b.shape[0]
    const = lambda *_: (0, 0)
    w_specs = [pl.BlockSpec((d, nw), const), pl.BlockSpec((1, nw), const),
               pl.BlockSpec((2 * aw, d), const), pl.BlockSpec(bkv.shape, const),
               pl.BlockSpec((heads, d), const), pl.BlockSpec((heads, 1), const)]
    common_shapes = (jax.ShapeDtypeStruct((n, cc), F32),
                     jax.ShapeDtypeStruct((n, aw), BF16))
    lft_shape = jax.ShapeDtypeStruct((heads, n), F32)
    if batch is None:
        tm = _pick(n, (512, 384, 256, 128, 64, 32, 16, 8))
        row = lambda i: (i, 0)
        return pl.pallas_call(
            functools.partial(_inproj_sample_body, cc=cc, aw=aw, scale=q_scale),
            grid=(n // tm,),
            in_specs=[pl.BlockSpec((tm, d), row)] + w_specs,
            out_specs=(pl.BlockSpec((tm, cc), row), pl.BlockSpec((tm, aw), row),
                       pl.BlockSpec((tm, aw), row), pl.BlockSpec((tm, aw), row),
                       pl.BlockSpec((heads, tm), lambda i: (0, i))),
            out_shape=common_shapes + (jax.ShapeDtypeStruct((n, aw), F32), jax.ShapeDtypeStruct((n, aw), F32),
                                       lft_shape),
            compiler_params=_params(("parallel",)),
            name="in_proj_sample",
        )(x, w_b, b_row, wkv_t, bkv, wft_b, bft_col)
    seq = n // batch
    tm = _pick(seq, (512, 256, 128))
    nt = seq // tm
    row = lambda b, i: (b * nt + i, 0)
    tr = lambda b, i: (b, 0, i)
    t_f32 = jax.ShapeDtypeStruct((batch, aw, seq), F32)
    t_b16 = jax.ShapeDtypeStruct((batch, aw, seq), BF16)
    return pl.pallas_call(
        functools.partial(_inproj_prompt_body, cc=cc, aw=aw, scale=q_scale),
        grid=(batch, nt),
        in_specs=[pl.BlockSpec((tm, d), row)] + w_specs,
        out_specs=(pl.BlockSpec((tm, cc), row), pl.BlockSpec((tm, aw), row),
                   pl.BlockSpec((1, aw, tm), tr), pl.BlockSpec((1, aw, tm), tr),
                   pl.BlockSpec((1, aw, tm), tr), pl.BlockSpec((1, aw, tm), tr),
                   pl.BlockSpec((heads, tm), lambda b, i: (0, b * nt + i))),
        out_shape=common_shapes + (t_f32, t_f32, t_b16, t_b16, lft_shape),
        compiler_params=_params(("parallel", "parallel")),
        name="in_proj_prompt",
    )(x, w_b, b_row, wkv_t, bkv, wft_b, bft_col)


def _group_norm_silu(y, m, gg, gb):
    mean = _dot3(y, m)
    d = y - mean
    var = _dot3(d * d, m)
    yn = d * lax.rsqrt(var + LN_EPS) * gg + gb
    return yn * _sigmoid(yn)


def _conv_body(u_ref, halo_ref, w_ref, b_ref, gg_ref, gb_ref, m_ref, o_ref, ext_ref, sh_ref, y_ref, *, ts, rc):
    i = pl.program_id(1)
    c = u_ref.shape[1]
    rows = CONV_HALO + ts
    halo = halo_ref[...]
    ext_ref[0:CONV_HALO, :] = jnp.where(i > 0, halo, jnp.zeros_like(halo))
    ext_ref[CONV_HALO:rows, :] = u_ref[...]
    ext_ref[rows:rows + SUBLANES, :] = jnp.zeros((SUBLANES, c), F32)
    for r in range(SUBLANES):
        sh_ref[r] = ext_ref[r:r + rows, :]
    off = CONV_HALO - (DW_WIDTH - 1)
    for r0 in range(0, ts, rc):
        acc = jnp.broadcast_to(b_ref[...], (rc, c))
        for j in range(DW_WIDTH):
            s = r0 + off + j
            r = s % SUBLANES
            acc = acc + w_ref[j:j + 1, :] * sh_ref[r, s - r:s - r + rc, :]
        y_ref[r0:r0 + rc, :] = acc
    o_ref[...] = _group_norm_silu(y_ref[...], m_ref[...], gg_ref[...], gb_ref[...]).astype(BF16)


def _conv_prompt(u, batch, seq, w_dw, b_row, gg_row, gb_row, gmat):
    n, c = u.shape
    ts = _pick(seq, (512, 256, 128, 64, 32))
    rc = min(ts, 64)
    nt = seq // ts
    hb = ts // CONV_HALO
    rows = CONV_HALO + ts
    const = lambda b, i: (0, 0)
    return pl.pallas_call(
        functools.partial(_conv_body, ts=ts, rc=rc),
        grid=(batch, nt),
        in_specs=[
            pl.BlockSpec((ts, c), lambda b, i: (b * nt + i, 0)),
            pl.BlockSpec((CONV_HALO, c), lambda b, i: (jnp.maximum((b * nt + i) * hb - 1, 0), 0)),
            pl.BlockSpec((DW_WIDTH, c), const),
            pl.BlockSpec((1, c), const),
            pl.BlockSpec((1, c), const),
            pl.BlockSpec((1, c), const),
            pl.BlockSpec((c, c), const),
        ],
        out_specs=pl.BlockSpec((ts, c), lambda b, i: (b * nt + i, 0)),
        out_shape=jax.ShapeDtypeStruct((n, c), BF16),
        scratch_shapes=[pltpu.VMEM((rows + SUBLANES, c), F32), pltpu.VMEM((SUBLANES, rows, c), F32),
                        pltpu.VMEM((ts, c), F32)],
        compiler_params=_params(("parallel", "parallel")),
        name="conv_prompt",
    )(u, u, w_dw, b_row, gg_row, gb_row, gmat)


def _conv_sample_body(hist_ref, u_ref, w_ref, b_ref, gg_ref, gb_ref, m_ref, o_ref):
    acc = b_ref[...] + w_ref[DW_WIDTH - 1:DW_WIDTH, :] * u_ref[...]
    for j in range(DW_WIDTH - 1):
        acc = acc + w_ref[j:j + 1, :] * hist_ref[j]
    o_ref[...] = _group_norm_silu(acc, m_ref[...], gg_ref[...], gb_ref[...]).astype(BF16)


def _conv_sample(hist_t, u, w_dw, b_row, gg_row, gb_row, gmat):
    n, c = u.shape
    full2 = lambda i: (0, 0)
    return pl.pallas_call(
        _conv_sample_body,
        grid=(1,),
        in_specs=[
            pl.BlockSpec(hist_t.shape, lambda i: (0, 0, 0)),
            pl.BlockSpec((n, c), full2),
            pl.BlockSpec((DW_WIDTH, c), full2),
            pl.BlockSpec((1, c), full2),
            pl.BlockSpec((1, c), full2),
            pl.BlockSpec((1, c), full2),
            pl.BlockSpec((c, c), full2),
        ],
        out_specs=pl.BlockSpec((n, c), full2),
        out_shape=jax.ShapeDtypeStruct((n, c), BF16),
        compiler_params=_params(("arbitrary",)),
        name="conv_sample",
    )(hist_t, u, w_dw, b_row, gg_row, gb_row, gmat)


def _cumsum_body(lf_ref, tri_ref, o_ref, carry_ref):
    @pl.when(pl.program_id(1) == 0)
    def _():
        carry_ref[...] = jnp.zeros_like(carry_ref)

    cl = lf_ref.shape[1]
    cum = _dot3(lf_ref[...], tri_ref[...]) + carry_ref[:, 0:1]
    o_ref[...] = -LOG2_E * cum
    carry_ref[...] = jnp.broadcast_to(cum[:, cl - 1:cl], carry_ref.shape)


def _neg_cumsum(lft, batch, seq):
    heads, n = lft.shape
    cl = _pick(seq, (512, 256, 128))
    nc = seq // cl
    tri = jnp.triu(jnp.ones((cl, cl), F32)).astype(BF16)
    return pl.pallas_call(
        _cumsum_body,
        grid=(batch, nc),
        in_specs=[pl.BlockSpec((heads, cl), lambda b, c: (0, b * nc + c)),
                  pl.BlockSpec((cl, cl), lambda b, c: (0, 0))],
        out_specs=pl.BlockSpec((heads, cl), lambda b, c: (0, b * nc + c)),
        out_shape=jax.ShapeDtypeStruct((heads, n), F32),
        scratch_shapes=[pltpu.VMEM((heads, LANES), F32)],
        compiler_params=_params(("parallel", "arbitrary")),
        name="forget_cumsum",
    )(lft, tri)


def _att_body(qi_ref, ki_ref, q_ref, k_ref, v_ref, nd_ref, o_ref, m_ref, l_ref, acc_ref, *, hd, pairs):
    step = pl.program_id(2)
    qi = qi_ref[step]
    ki = ki_ref[step]
    tq = q_ref.shape[0]
    tk = k_ref.shape[2]
    is_a = lax.broadcasted_iota(I32, (1, LANES), 1) < hd

    @pl.when(ki == 0)
    def _():
        m_ref[...] = jnp.full(m_ref.shape, NEG_INF, F32)
        l_ref[...] = jnp.zeros_like(l_ref)
        acc_ref[...] = jnp.zeros_like(acc_ref)

    def process(diagonal):
        if diagonal:
            causal = (lax.broadcasted_iota(I32, (tq, tk), 1) <= lax.broadcasted_iota(I32, (tq, tk), 0))
        for g in range(pairs):
            lanes = slice(g * LANES, (g + 1) * LANES)
            q = q_ref[:, lanes]
            kt = k_ref[0, lanes, :]
            vt = v_ref[0, lanes, :]
            nd = nd_ref[g]
            zero = jnp.zeros_like(q)
            pvs = []
            alphas = []
            for h in range(2):
                qh = jnp.where(is_a, q, zero) if h == 0 else jnp.where(is_a, zero, q)
                s = _dot(qh, kt) + nd[h:h + 1, :]
                if diagonal:
                    s = jnp.where(causal, s, NEG_INF)
                m_old = m_ref[2 * g + h]
                m_new = jnp.maximum(m_old, jnp.max(s, axis=1, keepdims=True))
                alpha = jnp.exp2(m_old - m_new)
                p = jnp.exp2(s - jnp.tile(m_new, (1, tk // LANES)))
                l_ref[2 * g + h] = alpha * l_ref[2 * g + h] + jnp.sum(p, axis=1, keepdims=True)
                m_ref[2 * g + h] = m_new
                pvs.append(_dot_nt(p.astype(BF16), vt))
                alphas.append(alpha)
            acc_ref[:, lanes] = (acc_ref[:, lanes] * jnp.where(is_a, alphas[0], alphas[1])
                                 + jnp.where(is_a, pvs[0], pvs[1]))

    @pl.when(ki < qi)
    def _():
        process(False)

    @pl.when(ki == qi)
    def _():
        process(True)
        for g in range(pairs):
            lanes = slice(g * LANES, (g + 1) * LANES)
            o_ref[:, lanes] = (acc_ref[:, lanes]
                               / jnp.where(is_a, l_ref[2 * g], l_ref[2 * g + 1])).astype(BF16)


def _attention_prompt(q_b, kt_b, vt_b, negd, batch, seq):
    n, aw = q_b.shape
    pairs = ATT_PAIRS_PER_STEP
    groups = aw // (pairs * LANES)
    w = pairs * LANES
    t = _pick(seq, (512, 256, 128))
    nq = seq // t
    steps = [(a, b) for a in range(nq) for b in range(a + 1)]
    qi_tab = jnp.asarray([s[0] for s in steps], I32)
    ki_tab = jnp.asarray([s[1] for s in steps], I32)
    grid_spec = pltpu.PrefetchScalarGridSpec(
        num_scalar_prefetch=2,
        grid=(batch, groups, len(steps)),
        in_specs=[
            pl.BlockSpec((t, w), lambda b, g, s, qt, kt: (b * nq + qt[s], g)),
            pl.BlockSpec((1, w, t), lambda b, g, s, qt, kt: (b, g, kt[s])),
            pl.BlockSpec((1, w, t), lambda b, g, s, qt, kt: (b, g, kt[s])),
            pl.BlockSpec((pairs, 2, t), lambda b, g, s, qt, kt: (g, 0, b * nq + kt[s])),
        ],
        out_specs=pl.BlockSpec((t, w), lambda b, g, s, qt, kt: (b * nq + qt[s], g)),
        scratch_shapes=[pltpu.VMEM((2 * pairs, t, LANES), F32), pltpu.VMEM((2 * pairs, t, LANES), F32),
                        pltpu.VMEM((t, w), F32)],
    )
    return pl.pallas_call(
        functools.partial(_att_body, hd=HEAD_DIM, pairs=pairs),
        grid_spec=grid_spec,
        out_shape=jax.ShapeDtypeStruct((n, aw), BF16),
        compiler_params=_params(("parallel", "parallel", "arbitrary")),
        name="attention_prompt",
    )(qi_tab, ki_tab, q_b, kt_b, vt_b, negd)


def _pool_suffix_body(lf_ref, tri_ref, ones_ref, w_ref, t_ref):
    x = lf_ref[...]
    w_ref[...] = _dot3(x, tri_ref[...])
    t_ref[...] = _dot3(x, ones_ref[...])


def _pool_suffix(lf_rows):
    n, page = lf_rows.shape
    tp = _pick(n, (2048, 1024, 512, 256, 128, 64, 32, 16, 8, n))
    tri = jnp.tril(jnp.ones((page, page), F32), -1).astype(BF16)
    ones = jnp.ones((page, page), BF16)
    blk = pl.BlockSpec((tp, page), lambda i: (i, 0))
    mat = pl.BlockSpec((page, page), lambda i: (0, 0))
    return pl.pallas_call(
        _pool_suffix_body,
        grid=(n // tp,),
        in_specs=[blk, mat, mat],
        out_specs=(blk, blk),
        out_shape=(jax.ShapeDtypeStruct((n, page), F32), jax.ShapeDtypeStruct((n, page), F32)),
        compiler_params=_params(("parallel",)),
        name="pool_forget_suffix",
    )(lf_rows, tri, ones)


def _satt_body(pt_ref, *refs, n_pages, heads):
    k_refs = refs[:n_pages]
    v_refs = refs[n_pages:2 * n_pages]
    w_refs = refs[2 * n_pages:3 * n_pages]
    t_refs = refs[3 * n_pages:4 * n_pages]
    q_ref, kn_ref, vn_ref, lfn_ref, o_ref, s_ref = refs[4 * n_pages:]
    hd, page = k_refs[0].shape[2:]
    q = q_ref[0].astype(F32)
    s_new = jnp.sum(q * kn_ref[0], axis=1, keepdims=True)
    q_cols = [jnp.broadcast_to(q[h:h + 1, :], (page, hd)).T for h in range(heads)]
    later = lfn_ref[0]
    m = s_new
    for j in reversed(range(n_pages)):
        for h in range(heads):
            s_ref[j, h:h + 1, :] = jnp.sum(k_refs[j][0, h] * q_cols[h], axis=0, keepdims=True)
        s = s_ref[j] + (w_refs[j][0] + later)
        s_ref[j] = s
        m = jnp.maximum(m, jnp.max(s, axis=1, keepdims=True))
        later = later + t_refs[j][0]
    p_new = jnp.exp(s_new - m)
    l = p_new
    for j in range(n_pages):
        p = jnp.exp(s_ref[j] - m)
        s_ref[j] = p
        l = l + jnp.sum(p, axis=1, keepdims=True)
    vn = vn_ref[0]
    for h in range(heads):
        acc = jnp.zeros((hd, page), F32)
        for j in range(n_pages):
            acc = acc + v_refs[j][0, h] * s_ref[j, h:h + 1, :]
        o = jnp.sum(acc.T, axis=0, keepdims=True) + p_new[h:h + 1, :] * vn[h:h + 1, :]
        o_ref[0, h:h + 1, :] = (o / l[h:h + 1, :]).astype(BF16)


def _attention_sample(q_b, k_new, v_new, lf_new, cache_kt, cache_vt, cache_lft, page_table):
    n, aw = q_b.shape
    n_pages = page_table.shape[1]
    pool, heads, hd, page = cache_kt.shape
    within, totals = _pool_suffix(cache_lft.reshape(pool * heads, page))
    within = within.reshape(pool, heads, page)
    totals = totals.reshape(pool, heads, page)
    lfn_lanes = jnp.broadcast_to(lf_new[:, :, None], (n, heads, page))

    def kv_spec(j):
        return pl.BlockSpec((1, heads, hd, page), lambda i, pt, j=j: (pt[i, j], 0, 0, 0))

    def row_spec(j):
        return pl.BlockSpec((1, heads, page), lambda i, pt, j=j: (pt[i, j], 0, 0))

    per_seq = lambda shape: pl.BlockSpec((1,) + shape, lambda i, pt: (i, 0, 0))
    grid_spec = pltpu.PrefetchScalarGridSpec(
        num_scalar_prefetch=1,
        grid=(n,),
        in_specs=([kv_spec(j) for j in range(n_pages)] + [kv_spec(j) for j in range(n_pages)]
                  + [row_spec(j) for j in range(n_pages)] + [row_spec(j) for j in range(n_pages)]
                  + [per_seq((heads, hd)), per_seq((heads, hd)), per_seq((heads, hd)), per_seq((heads, page))]),
        out_specs=per_seq((heads, hd)),
        scratch_shapes=[pltpu.VMEM((n_pages, heads, page), F32)],
    )
    out = pl.pallas_call(
        functools.partial(_satt_body, n_pages=n_pages, heads=heads),
        grid_spec=grid_spec,
        out_shape=jax.ShapeDtypeStruct((n, heads, hd), BF16),
        compiler_params=_params(("arbitrary",)),
        name="attention_sample",
    )(page_table, *([cache_kt] * n_pages), *([cache_vt] * n_pages), *([within] * n_pages), *([totals] * n_pages),
      q_b.reshape(n, heads, hd), k_new.reshape(n, heads, hd), v_new.reshape(n, heads, hd), lfn_lanes)
    return out.reshape(n, aw)


def _mix_body(x_ref, c_ref, a_ref, p_ref, wo_ref, g1_ref, b1_ref, wrh_ref, wrl_ref, br_ref,
              wpg_ref, bpg_ref, wpp_ref, h_ref, base_ref, idx_ref, gate_ref, cnt_ref, *, alpha, cc):
    mix = _dot(c_ref[...], wo_ref[:cc, :]) + _dot(a_ref[...], wo_ref[cc:, :])
    h = _layer_norm(alpha * x_ref[...] + mix, g1_ref[...], b1_ref[...])
    h_ref[...] = h
    hb = h.astype(BF16)
    hl = (h - hb.astype(F32)).astype(BF16)
    wrh = wrh_ref[...]
    logits = _dot(hb, wrh) + _dot(hl, wrh) + _dot(hb, wrl_ref[...]) + br_ref[...]

    tm = logits.shape[0]
    lane = lax.broadcasted_iota(I32, (tm, LANES), 1).astype(F32)
    vals, idxs = [], []
    l = logits
    for _ in range(TOP_K):
        m = jnp.max(l, axis=1, keepdims=True)
        ix = jnp.min(jnp.where(l == m, lane, float(LANES)), axis=1, keepdims=True)
        vals.append(m)
        idxs.append(ix)
        l = jnp.where(lane == ix, -jnp.inf, l)
    es = [jnp.exp(v - vals[0]) for v in vals]
    den = es[0] + es[1] + es[2] + es[3]
    idx_w = jnp.zeros((tm, LANES), F32)
    gate_w = jnp.zeros((tm, LANES), F32)
    sel = jnp.zeros((tm, LANES), F32)
    for r in range(TOP_K):
        idx_w = jnp.where(lane == float(r), idxs[r], idx_w)
        gate_w = jnp.where(lane == float(r), es[r] / den, gate_w)
        sel = sel + (lane == idxs[r]).astype(F32)
    idx_ref[...] = idx_w
    gate_ref[...] = gate_w

    @pl.when(pl.program_id(0) == 0)
    def _():
        cnt_ref[...] = jnp.zeros_like(cnt_ref)

    cnt_ref[...] += jnp.sum(sel, axis=0, keepdims=True)

    ple = _sigmoid(_dot(hb, wpg_ref[...]) + bpg_ref[...]) * _dot(p_ref[...].astype(BF16), wpp_ref[...])
    base_ref[...] = alpha * h + ple


def _mix(x, conv_b, att_b, p, wo_b, g1, b1, wr_hi, wr_lo, br_row, wpg_b, bpg, wpp_b, alpha):
    n, d = x.shape
    cc = conv_b.shape[1]
    aw = att_b.shape[1]
    pd = p.shape[1]
    tm = _pick(n, (512, 256, 128, 64, 32, 16, 8))
    row = lambda i: (i, 0)
    const = lambda i: (0, 0)
    outs = (
        jax.ShapeDtypeStruct((n, d), F32),
        jax.ShapeDtypeStruct((n, d), F32),
        jax.ShapeDtypeStruct((n, LANES), F32),
        jax.ShapeDtypeStruct((n, LANES), F32),
        jax.ShapeDtypeStruct((1, LANES), F32),
    )
    return pl.pallas_call(
        functools.partial(_mix_body, alpha=alpha, cc=cc),
        grid=(n // tm,),
        in_specs=[
            pl.BlockSpec((tm, d), row), pl.BlockSpec((tm, cc), row), pl.BlockSpec((tm, aw), row),
            pl.BlockSpec((tm, pd), row),
            pl.BlockSpec((d, d), const), pl.BlockSpec((1, d), const), pl.BlockSpec((1, d), const),
            pl.BlockSpec((d, LANES), const), pl.BlockSpec((d, LANES), const), pl.BlockSpec((1, LANES), const),
            pl.BlockSpec((d, d), const), pl.BlockSpec((1, d), const), pl.BlockSpec((pd, d), const),
        ],
        out_specs=(
            pl.BlockSpec((tm, d), row), pl.BlockSpec((tm, d), row),
            pl.BlockSpec((tm, LANES), row), pl.BlockSpec((tm, LANES), row),
            pl.BlockSpec((1, LANES), const),
        ),
        out_shape=outs,
        compiler_params=_params(("arbitrary",)),
        name="out_proj_router",
    )(x, conv_b, att_b, p, wo_b, g1, b1, wr_hi, wr_lo, br_row, wpg_b, bpg, wpp_b)


def _positions_body(idx_ref, start_ref, tri_ref, o_ref, carry_ref):
    @pl.when(pl.program_id(0) == 0)
    def _():
        carry_ref[...] = jnp.zeros_like(carry_ref)

    ids = idx_ref[...]
    tm = ids.shape[0]
    lane = lax.broadcasted_iota(I32, (tm, LANES), 1).astype(F32)
    hot = [lane == ids[:, r:r + 1] for r in range(TOP_K)]
    sel = jnp.zeros((tm, LANES), F32)
    for r in range(TOP_K):
        sel = sel + hot[r].astype(F32)
    rank = _dot(tri_ref[...], sel.astype(BF16)) + carry_ref[...]
    pos = rank + start_ref[...]
    out = jnp.zeros((tm, LANES), F32)
    for r in range(TOP_K):
        d = jnp.sum(jnp.where(hot[r], pos, 0.0), axis=1, keepdims=True)
        out = jnp.where(lane == float(r), d, out)
    o_ref[...] = out.astype(I32)
    carry_ref[...] += jnp.sum(sel, axis=0, keepdims=True)


def _positions(idx_w, start_row):
    n = idx_w.shape[0]
    tm = _pick(n, (512, 384, 256, 128, 64, 32, 16, 8))
    tri = jnp.tril(jnp.ones((tm, tm), F32), -1).astype(BF16)
    return pl.pallas_call(
        _positions_body,
        grid=(n // tm,),
        in_specs=[pl.BlockSpec((tm, LANES), lambda i: (i, 0)),
                  pl.BlockSpec((1, LANES), lambda i: (0, 0)),
                  pl.BlockSpec((tm, tm), lambda i: (0, 0))],
        out_specs=pl.BlockSpec((tm, LANES), lambda i: (i, 0)),
        out_shape=jax.ShapeDtypeStruct((n, LANES), I32),
        scratch_shapes=[pltpu.VMEM((1, LANES), F32)],
        compiler_params=_params(("arbitrary",)),
        name="route_positions",
    )(idx_w, start_row, tri)


def _dispatch_body(dest_ref, h_ref, xs_in_ref, xs_ref, sem, *, tb):
    del xs_in_ref

    def issue(t, c):
        for r in range(TOP_K):
            d = dest_ref[0, 0, t * TOP_K + r]
            pltpu.make_async_copy(h_ref.at[pl.ds(t, 1)], xs_ref.at[pl.ds(d, 1)], sem).start(priority=r % 2)
        return c

    lax.fori_loop(0, tb, issue, 0)
    for r in range(TOP_K):
        pltpu.make_async_copy(h_ref, xs_ref.at[pl.ds(0, tb)], sem).wait()


def _dispatch(h, dest, xs):
    n, d = h.shape
    tb = _pick(n, (256, 128, 64, 32, 16, 8))
    dest3 = dest.reshape(n // tb, 1, tb * TOP_K)
    return pl.pallas_call(
        functools.partial(_dispatch_body, tb=tb),
        grid=(n // tb,),
        in_specs=[
            pl.BlockSpec((1, 1, tb * TOP_K), lambda i: (i, 0, 0), memory_space=pltpu.SMEM),
            pl.BlockSpec((tb, d), lambda i: (i, 0)),
            pl.BlockSpec(memory_space=pl.ANY),
        ],
        out_specs=pl.BlockSpec(memory_space=pl.ANY),
        out_shape=jax.ShapeDtypeStruct(xs.shape, xs.dtype),
        scratch_shapes=[pltpu.SemaphoreType.DMA],
        input_output_aliases={2: 0},
        compiler_params=_params(("arbitrary",)),
        name="moe_dispatch",
    )(dest3, h, xs)


def _expert_body(be_ref, nu_ref, x_ref, wg_ref, bg_ref, wu_ref, bu_ref, wd_ref, bd_ref, o_ref,
                 wgb_ref, wub_ref, wdb_ref):
    i = pl.program_id(0)

    @pl.when(i < nu_ref[0])
    def _():
        @pl.when((i == 0) | (be_ref[i] != be_ref[jnp.maximum(i - 1, 0)]))
        def _():
            wgb_ref[...] = wg_ref[0].astype(BF16)
            wub_ref[...] = wu_ref[0].astype(BF16)
            wdb_ref[...] = wd_ref[0].astype(BF16)

        x = x_ref[...].astype(BF16)
        g = jnp.minimum(_dot(x, wgb_ref[...]) + bg_ref[0], SWIGLU_LIMIT)
        u = jnp.clip(_dot(x, wub_ref[...]) + bu_ref[0], -SWIGLU_LIMIT, SWIGLU_LIMIT)
        hidden = (u + 1.0) * (g * _sigmoid(SWIGLU_ALPHA * g))
        o_ref[...] = _dot(hidden.astype(BF16), wdb_ref[...]) + bd_ref[0]

    @pl.when(i >= nu_ref[0])
    def _():
        o_ref[...] = jnp.zeros_like(o_ref)


def _experts(xs, block_e, n_used, w_gate, b_gate, w_up, b_up, w_down, b_down):
    rows, d = xs.shape
    n_e, _, de = w_gate.shape
    n_blocks = rows // MOE_BLOCK
    last = lambda i, be, nu: (jnp.minimum(i, nu[0] - 1), 0)
    wsel = lambda i, be, nu: (be[i], 0, 0)
    grid_spec = pltpu.PrefetchScalarGridSpec(
        num_scalar_prefetch=2,
        grid=(n_blocks,),
        in_specs=[
            pl.BlockSpec((MOE_BLOCK, d), last),
            pl.BlockSpec((1, d, de), wsel), pl.BlockSpec((1, 1, de), wsel),
            pl.BlockSpec((1, d, de), wsel), pl.BlockSpec((1, 1, de), wsel),
            pl.BlockSpec((1, de, d), wsel), pl.BlockSpec((1, 1, d), wsel),
        ],
        out_specs=pl.BlockSpec((MOE_BLOCK, d), lambda i, be, nu: (i, 0)),
        scratch_shapes=[pltpu.VMEM((d, de), BF16), pltpu.VMEM((d, de), BF16), pltpu.VMEM((de, d), BF16)],
    )
    return pl.pallas_call(
        _expert_body,
        grid_spec=grid_spec,
        out_shape=jax.ShapeDtypeStruct((rows, d), F32),
        compiler_params=_params(("arbitrary",)),
        name="moe_experts",
    )(block_e, n_used, xs, w_gate, b_gate[:, None, :], w_up, b_up[:, None, :], w_down, b_down[:, None, :])


def _combine_body(dest_ref, base_ref, gate_ref, y_ref, g2_ref, b2_ref, o_ref, buf_ref, sem, *, tb):
    def issue(t, c):
        for r in range(TOP_K):
            d = dest_ref[0, 0, t * TOP_K + r]
            pltpu.make_async_copy(y_ref.at[pl.ds(d, 1)], buf_ref.at[r, pl.ds(t, 1)], sem).start(priority=r % 2)
        return c

    lax.fori_loop(0, tb, issue, 0)
    for r in range(TOP_K):
        pltpu.make_async_copy(y_ref.at[pl.ds(0, tb)], buf_ref.at[r], sem).wait()

    gate = gate_ref[...]
    y = base_ref[...]
    for r in range(TOP_K):
        y = y + gate[:, r:r + 1] * buf_ref[r]
    o_ref[...] = _layer_norm(y, g2_ref[...], b2_ref[...])


def _combine(base, gate_w, dest, y_rows, g2, b2):
    n, d = base.shape
    tb = _pick(n, (128, 64, 32, 16, 8))
    dest3 = dest.reshape(n // tb, 1, tb * TOP_K)
    row = lambda i: (i, 0)
    const = lambda i: (0, 0)
    return pl.pallas_call(
        functools.partial(_combine_body, tb=tb),
        grid=(n // tb,),
        in_specs=[
            pl.BlockSpec((1, 1, tb * TOP_K), lambda i: (i, 0, 0), memory_space=pltpu.SMEM),
            pl.BlockSpec((tb, d), row),
            pl.BlockSpec((tb, LANES), row),
            pl.BlockSpec(memory_space=pl.ANY),
            pl.BlockSpec((1, d), const), pl.BlockSpec((1, d), const),
        ],
        out_specs=pl.BlockSpec((tb, d), row),
        out_shape=jax.ShapeDtypeStruct((n, d), F32),
        scratch_shapes=[pltpu.VMEM((TOP_K, tb, d), F32), pltpu.SemaphoreType.DMA],
        compiler_params=_params(("arbitrary",)),
        name="moe_combine",
    )(dest3, base, gate_w, y_rows, g2, b2)


def _layer(x_p, x_s, ck, cv, clf, hist_s, page_table, p_p, p_s, lw, alpha):
    batch, seq, d = x_p.shape
    n_s = x_s.shape[0]
    n_p = batch * seq
    cc = lw["w_dw"].shape[1]
    aw = ATT_HEADS * HEAD_DIM
    heads = ATT_HEADS
    nz = 2 * cc + 3 * aw
    row = lambda v: v[None, :]

    nq_cols = 2 * cc + aw
    w_in_t = lw["w_in"].T
    w_in_b = lw["w_in"][:, :nq_cols].astype(BF16)
    b_in_row = row(lw["b_in"][:nq_cols])
    wkv_t = w_in_t[nq_cols:nz].astype(BF16)
    bkv = lw["b_in"][nq_cols:nz]
    wft_b = w_in_t[nz:].astype(BF16)
    bft_col = lw["b_in"][nz:, None]
    b_dw, gg, gb = row(lw["b_dw"]), row(lw["gn_gain"]), row(lw["gn_bias"])
    gsz = cc // CONV_GROUPS
    grp = jnp.arange(cc) // gsz
    gmat = ((grp[:, None] == grp[None, :]).astype(F32) / gsz).astype(BF16)
    wo_b = lw["w_out"].astype(BF16)
    wr = jnp.pad(lw["w_router"], ((0, 0), (0, LANES - N_EXPERTS)))
    wr_hi = wr.astype(BF16)
    wr_lo = (wr - wr_hi.astype(F32)).astype(BF16)
    br_row = row(jnp.pad(lw["b_router"], (0, LANES - N_EXPERTS), constant_values=NEG_INF))
    wpg_b = lw["w_ple_gate"].astype(BF16)
    wpp_b = lw["w_ple_proj"].astype(BF16)

    xp2 = x_p.reshape(n_p, d)
    xs2 = x_s.reshape(n_s, d)

    u_p, q_p, kt_p, vt_p, ktb_p, vtb_p, lft_p = _in_proj(xp2, w_in_b, b_in_row, wkv_t, bkv[:, None], wft_b, bft_col,
                                                         cc, aw, HEAD_DIM ** -0.5 * LOG2_E, batch=batch)
    conv_p = _conv_prompt(u_p, batch, seq, lw["w_dw"], b_dw, gg, gb, gmat)
    negd = _neg_cumsum(lft_p, batch, seq).reshape(heads // 2, 2, n_p)
    att_p = _attention_prompt(q_p, ktb_p, vtb_p, negd, batch, seq)
    h_p, base_p, idx_p, gate_p, cnt_p = _mix(xp2, conv_p, att_p, p_p.reshape(n_p, -1), wo_b,
                                             row(lw["ln1_gain"]), row(lw["ln1_bias"]), wr_hi, wr_lo, br_row,
                                             wpg_b, row(lw["b_ple_gate"]), wpp_b, alpha)

    u_s, q_s, k_s, v_s, lft_s = _in_proj(xs2, w_in_b, b_in_row, wkv_t, bkv[None, :], wft_b, bft_col,
                                         cc, aw, HEAD_DIM ** -0.5)
    conv_s = _conv_sample(hist_s.transpose(1, 0, 2), u_s, lw["w_dw"], b_dw, gg, gb, gmat)
    lf_s = lft_s.T
    att_s = _attention_sample(q_s, k_s, v_s, lf_s, ck.transpose(0, 2, 3, 1), cv.transpose(0, 2, 3, 1),
                              clf.transpose(0, 2, 1), page_table)
    h_s, base_s, idx_s, gate_s, cnt_s = _mix(xs2, conv_s, att_s, p_s.reshape(n_s, -1), wo_b,
                                             row(lw["ln1_gain"]), row(lw["ln1_bias"]), wr_hi, wr_lo, br_row,
                                             wpg_b, row(lw["b_ple_gate"]), wpp_b, alpha)

    n_tok = n_p + n_s
    counts = (cnt_p + cnt_s)[0].astype(I32)
    padded = (counts + MOE_BLOCK - 1) // MOE_BLOCK * MOE_BLOCK
    pad_end = jnp.cumsum(padded)
    pad_start = pad_end - padded
    n_blocks = -(-(n_tok * TOP_K) // MOE_BLOCK) + N_EXPERTS
    n_used = (pad_end[N_EXPERTS - 1] // MOE_BLOCK).astype(I32).reshape(1)
    block_row0 = jnp.arange(n_blocks, dtype=I32) * MOE_BLOCK
    block_e = jnp.minimum(jnp.sum((pad_end[None, :N_EXPERTS] <= block_row0[:, None]).astype(I32), axis=1),
                          N_EXPERTS - 1)
    dest = _positions(jnp.concatenate([idx_p, idx_s], axis=0), pad_start.astype(F32)[None, :])[:, :TOP_K]
    dest_p, dest_s = dest[:n_p], dest[n_p:]

    xs_rows = jnp.zeros((n_blocks * MOE_BLOCK, d), F32)
    xs_rows = _dispatch(h_p, dest_p, xs_rows)
    xs_rows = _dispatch(h_s, dest_s, xs_rows)
    y_rows = _experts(xs_rows, block_e, n_used, lw["w_gate"], lw["b_gate"], lw["w_up"], lw["b_up"],
                      lw["w_down"], lw["b_down"])
    g2, b2 = row(lw["ln2_gain"]), row(lw["ln2_bias"])
    y_p = _combine(base_p, gate_p, dest_p, y_rows, g2, b2).reshape(batch, seq, d)
    y_s = _combine(base_s, gate_s, dest_s, y_rows, g2, b2).reshape(n_s, 1, d)

    lf_p = lft_p.T.reshape(batch, seq, heads)
    hist_p = u_p.reshape(batch, seq, cc)[:, seq - (DW_WIDTH - 1):]
    hist_new_s = jnp.concatenate([hist_s[:, 1:], u_s[:, None, :]], axis=1)
    return (y_p, y_s,
            kt_p.reshape(batch, heads, HEAD_DIM, seq).transpose(0, 3, 1, 2),
            vt_p.reshape(batch, heads, HEAD_DIM, seq).transpose(0, 3, 1, 2), lf_p, hist_p,
            k_s.reshape(n_s, 1, heads, HEAD_DIM), v_s.reshape(n_s, 1, heads, HEAD_DIM),
            lf_s.reshape(n_s, 1, heads), hist_new_s)


def kernel(x_prompt, x_sample, cache_k, cache_v, cache_logf, state_conv, page_table, p_prompt, p_sample,
           w_in, b_in, w_dw, b_dw, gn_gain, gn_bias, w_out, ln1_gain, ln1_bias,
           w_router, b_router, w_gate, b_gate, w_up, b_up, w_down, b_down,
           w_ple_gate, b_ple_gate, w_ple_proj, ln2_gain, ln2_bias):
    depth = w_in.shape[0]
    if depth != 1 or x_sample.shape[1] != 1:
        raise NotImplementedError("one layer and one new token per sample sequence are supported")
    alpha = (2.0 * depth) ** 0.25
    lw = dict(w_in=w_in[0], b_in=b_in[0], w_dw=w_dw[0], b_dw=b_dw[0], gn_gain=gn_gain[0], gn_bias=gn_bias[0],
              w_out=w_out[0], ln1_gain=ln1_gain[0], ln1_bias=ln1_bias[0],
              w_router=w_router[0], b_router=b_router[0], w_gate=w_gate[0], b_gate=b_gate[0],
              w_up=w_up[0], b_up=b_up[0], w_down=w_down[0], b_down=b_down[0],
              w_ple_gate=w_ple_gate[0], b_ple_gate=b_ple_gate[0], w_ple_proj=w_ple_proj[0],
              ln2_gain=ln2_gain[0], ln2_bias=ln2_bias[0])
    outs = _layer(x_prompt, x_sample[:, 0], cache_k[0], cache_v[0], cache_logf[0], state_conv[0], page_table,
                  p_prompt[0], p_sample[0], lw, alpha)
    (y_p, y_s, k_p, v_p, lf_p, c_p, k_s, v_s, lf_s, c_s) = outs
    return (y_p, y_s, k_p[None], v_p[None], lf_p[None], c_p[None], k_s[None], v_s[None], lf_s[None], c_s[None])
```

```python
import functools

import jax
import jax.numpy as jnp
from jax import lax
from jax.experimental import pallas as pl
from jax.experimental.pallas import tpu as pltpu

F32 = jnp.float32
BF16 = jnp.bfloat16
I32 = jnp.int32

ATT_HEADS = 8
HEAD_DIM = 64
CONV_GROUPS = 8
DW_WIDTH = 31
N_EXPERTS = 32
TOP_K = 4
SWIGLU_LIMIT = 7.0
SWIGLU_ALPHA = 1.702
LN_EPS = 1e-5
NEG_INF = -1e30
LOG2_E = 1.4426950408889634

LANES = 128
SUBLANES = 8
CONV_HALO = 32
MOE_BLOCK = 512
ATT_PAIRS_PER_STEP = 4
VMEM_LIMIT = 56 * 1024 * 1024


def _pick(n, cands):
    for c in cands:
        if n % c == 0:
            return c
    raise ValueError(f"no block size in {cands} divides {n}")


def _params(sem, vmem=VMEM_LIMIT):
    return pltpu.CompilerParams(dimension_semantics=sem, vmem_limit_bytes=vmem)


def _sigmoid(x):
    return 1.0 / (1.0 + jnp.exp(-x))


def _log_sigmoid(x):
    return jnp.minimum(x, 0.0) - jnp.log1p(jnp.exp(-jnp.abs(x)))


def _dot(a, b):
    return jnp.dot(a, b, preferred_element_type=F32)


def _dot_nt(a, b):
    return lax.dot_general(a, b, (((1,), (1,)), ((), ())), preferred_element_type=F32)


def _split3(x):
    x1 = x.astype(BF16)
    r1 = x - x1.astype(F32)
    x2 = r1.astype(BF16)
    x3 = (r1 - x2.astype(F32)).astype(BF16)
    return x1, x2, x3


def _dot3(x, m):
    x1, x2, x3 = _split3(x)
    return _dot(x1, m) + _dot(x2, m) + _dot(x3, m)


def _layer_norm(x, g, b):
    mu = jnp.mean(x, axis=-1, keepdims=True)
    d = x - mu
    var = jnp.mean(d * d, axis=-1, keepdims=True)
    return d * lax.rsqrt(var + LN_EPS) * g + b


def _inproj_common(x_ref, w_ref, b_ref, wft_ref, bft_ref, u_ref, q_ref, lft_ref, *, cc, aw, scale):
    x = x_ref[...].astype(BF16)

    def seg(lo, hi):
        return _dot(x, w_ref[:, lo:hi]) + b_ref[:, lo:hi]

    u_ref[...] = seg(0, cc) * _sigmoid(seg(cc, 2 * cc))
    q_ref[...] = (seg(2 * cc, 2 * cc + aw) * scale).astype(BF16)
    lft_ref[...] = _log_sigmoid(_dot_nt(wft_ref[...], x) + bft_ref[...])
    return x


def _inproj_prompt_body(x_ref, w_ref, b_ref, wkv_ref, bkv_ref, wft_ref, bft_ref,
                        u_ref, q_ref, kt_ref, vt_ref, ktb_ref, vtb_ref, lft_ref, *, cc, aw, scale):
    x = _inproj_common(x_ref, w_ref, b_ref, wft_ref, bft_ref, u_ref, q_ref, lft_ref, cc=cc, aw=aw, scale=scale)
    kt = _dot_nt(wkv_ref[0:aw, :], x) + bkv_ref[0:aw, :]
    kt_ref[0] = kt
    ktb_ref[0] = kt.astype(BF16)
    vt = _dot_nt(wkv_ref[aw:2 * aw, :], x) + bkv_ref[aw:2 * aw, :]
    vt_ref[0] = vt
    vtb_ref[0] = vt.astype(BF16)


def _inproj_sample_body(x_ref, w_ref, b_ref, wkv_ref, bkv_ref, wft_ref, bft_ref,
                        u_ref, q_ref, k_ref, v_ref, lft_ref, *, cc, aw, scale):
    x = _inproj_common(x_ref, w_ref, b_ref, wft_ref, bft_ref, u_ref, q_ref, lft_ref, cc=cc, aw=aw, scale=scale)
    k_ref[...] = _dot_nt(x, wkv_ref[0:aw, :]) + bkv_ref[:, 0:aw]
    v_ref[...] = _dot_nt(x, wkv_ref[aw:2 * aw, :]) + bkv_ref[:, aw:2 * aw]


def _in_proj(x, w_b, b_row, wkv_t, bkv, wft_b, bft_col, cc, aw, q_scale, batch=None):
    n, d = x.shape
    nw = w_b.shape[1]
    heads = wft_b.shape[0]
    const = lambda *_: (0, 0)
    w_specs = [pl.BlockSpec((d, nw), const), pl.BlockSpec((1, nw), const),
               pl.BlockSpec((2 * aw, d), const), pl.BlockSpec(bkv.shape, const),
               pl.BlockSpec((heads, d), const), pl.BlockSpec((heads, 1), const)]
    common_shapes = (jax.ShapeDtypeStruct((n, cc), F32),
                     jax.ShapeDtypeStruct((n, aw), BF16))
    lft_shape = jax.ShapeDtypeStruct((heads, n), F32)
    if batch is None:
        tm = _pick(n, (512, 384, 256, 128, 64, 32, 16, 8))
        row = lambda i: (i, 0)
        return pl.pallas_call(
            functools.partial(_inproj_sample_body, cc=cc, aw=aw, scale=q_scale),
            grid=(n // tm,),
            in_specs=[pl.BlockSpec((tm, d), row)] + w_specs,
            out_specs=(pl.BlockSpec((tm, cc), row), pl.BlockSpec((tm, aw), row),
                       pl.BlockSpec((tm, aw), row), pl.BlockSpec((tm, aw), row),
                       pl.BlockSpec((heads, tm), lambda i: (0, i))),
            out_shape=common_shapes + (jax.ShapeDtypeStruct((n, aw), F32), jax.ShapeDtypeStruct((n, aw), F32),
                                       lft_shape),
            compiler_params=_params(("parallel",)),
            name="in_proj_sample",
        )(x, w_b, b_row, wkv_t, bkv, wft_b, bft_col)
    seq = n // batch
    tm = _pick(seq, (512, 256, 128))
    nt = seq // tm
    row = lambda b, i: (b * nt + i, 0)
    tr = lambda b, i: (b, 0, i)
    t_f32 = jax.ShapeDtypeStruct((batch, aw, seq), F32)
    t_b16 = jax.ShapeDtypeStruct((batch, aw, seq), BF16)
    return pl.pallas_call(
        functools.partial(_inproj_prompt_body, cc=cc, aw=aw, scale=q_scale),
        grid=(batch, nt),
        in_specs=[pl.BlockSpec((tm, d), row)] + w_specs,
        out_specs=(pl.BlockSpec((tm, cc), row), pl.BlockSpec((tm, aw), row),
                   pl.BlockSpec((1, aw, tm), tr), pl.BlockSpec((1, aw, tm), tr),
                   pl.BlockSpec((1, aw, tm), tr), pl.BlockSpec((1, aw, tm), tr),
                   pl.BlockSpec((heads, tm), lambda b, i: (0, b * nt + i))),
        out_shape=common_shapes + (t_f32, t_f32, t_b16, t_b16, lft_shape),
        compiler_params=_params(("parallel", "parallel")),
        name="in_proj_prompt",
    )(x, w_b, b_row, wkv_t, bkv, wft_b, bft_col)


def _group_norm_silu(y, m, gg, gb):
    mean = _dot3(y, m)
    d = y - mean
    var = _dot3(d * d, m)
    yn = d * lax.rsqrt(var + LN_EPS) * gg + gb
    return yn * _sigmoid(yn)


def _conv_body(u_ref, halo_ref, w_ref, b_ref, gg_ref, gb_ref, m_ref, o_ref, ext_ref, sh_ref, y_ref, *, ts, rc):
    i = pl.program_id(1)
    c = u_ref.shape[1]
    rows = CONV_HALO + ts
    halo = halo_ref[...]
    ext_ref[0:CONV_HALO, :] = jnp.where(i > 0, halo, jnp.zeros_like(halo))
    ext_ref[CONV_HALO:rows, :] = u_ref[...]
    ext_ref[rows:rows + SUBLANES, :] = jnp.zeros((SUBLANES, c), F32)
    for r in range(SUBLANES):
        sh_ref[r] = ext_ref[r:r + rows, :]
    off = CONV_HALO - (DW_WIDTH - 1)
    for r0 in range(0, ts, rc):
        acc = jnp.broadcast_to(b_ref[...], (rc, c))
        for j in range(DW_WIDTH):
            s = r0 + off + j
            r = s % SUBLANES
            acc = acc + w_ref[j:j + 1, :] * sh_ref[r, s - r:s - r + rc, :]
        y_ref[r0:r0 + rc, :] = acc
    o_ref[...] = _group_norm_silu(y_ref[...], m_ref[...], gg_ref[...], gb_ref[...]).astype(BF16)


def _conv_prompt(u, batch, seq, w_dw, b_row, gg_row, gb_row, gmat):
    n, c = u.shape
    ts = _pick(seq, (512, 256, 128, 64, 32))
    rc = min(ts, 64)
    nt = seq // ts
    hb = ts // CONV_HALO
    rows = CONV_HALO + ts
    const = lambda b, i: (0, 0)
    return pl.pallas_call(
        functools.partial(_conv_body, ts=ts, rc=rc),
        grid=(batch, nt),
        in_specs=[
            pl.BlockSpec((ts, c), lambda b, i: (b * nt + i, 0)),
            pl.BlockSpec((CONV_HALO, c), lambda b, i: (jnp.maximum((b * nt + i) * hb - 1, 0), 0)),
            pl.BlockSpec((DW_WIDTH, c), const),
            pl.BlockSpec((1, c), const),
            pl.BlockSpec((1, c), const),
            pl.BlockSpec((1, c), const),
            pl.BlockSpec((c, c), const),
        ],
        out_specs=pl.BlockSpec((ts, c), lambda b, i: (b * nt + i, 0)),
        out_shape=jax.ShapeDtypeStruct((n, c), BF16),
        scratch_shapes=[pltpu.VMEM((rows + SUBLANES, c), F32), pltpu.VMEM((SUBLANES, rows, c), F32),
                        pltpu.VMEM((ts, c), F32)],
        compiler_params=_params(("parallel", "parallel")),
        name="conv_prompt",
    )(u, u, w_dw, b_row, gg_row, gb_row, gmat)


def _conv_sample_body(hist_ref, u_ref, w_ref, b_ref, gg_ref, gb_ref, m_ref, o_ref):
    acc = b_ref[...] + w_ref[DW_WIDTH - 1:DW_WIDTH, :] * u_ref[...]
    for j in range(DW_WIDTH - 1):
        acc = acc + w_ref[j:j + 1, :] * hist_ref[j]
    o_ref[...] = _group_norm_silu(acc, m_ref[...], gg_ref[...], gb_ref[...]).astype(BF16)


def _conv_sample(hist_t, u, w_dw, b_row, gg_row, gb_row, gmat):
    n, c = u.shape
    full2 = lambda i: (0, 0)
    return pl.pallas_call(
        _conv_sample_body,
        grid=(1,),
        in_specs=[
            pl.BlockSpec(hist_t.shape, lambda i: (0, 0, 0)),
            pl.BlockSpec((n, c), full2),
            pl.BlockSpec((DW_WIDTH, c), full2),
            pl.BlockSpec((1, c), full2),
            pl.BlockSpec((1, c), full2),
            pl.BlockSpec((1, c), full2),
            pl.BlockSpec((c, c), full2),
        ],
        out_specs=pl.BlockSpec((n, c), full2),
        out_shape=jax.ShapeDtypeStruct((n, c), BF16),
        compiler_params=_params(("arbitrary",)),
        name="conv_sample",
    )(hist_t, u, w_dw, b_row, gg_row, gb_row, gmat)


def _cumsum_body(lf_ref, tri_ref, o_ref, carry_ref):
    @pl.when(pl.program_id(1) == 0)
    def _():
        carry_ref[...] = jnp.zeros_like(carry_ref)

    cl = lf_ref.shape[1]
    cum = _dot3(lf_ref[...], tri_ref[...]) + carry_ref[:, 0:1]
    o_ref[...] = -LOG2_E * cum
    carry_ref[...] = jnp.broadcast_to(cum[:, cl - 1:cl], carry_ref.shape)


def _neg_cumsum(lft, batch, seq):
    heads, n = lft.shape
    cl = _pick(seq, (512, 256, 128))
    nc = seq // cl
    tri = jnp.triu(jnp.ones((cl, cl), F32)).astype(BF16)
    return pl.pallas_call(
        _cumsum_body,
        grid=(batch, nc),
        in_specs=[pl.BlockSpec((heads, cl), lambda b, c: (0, b * nc + c)),
                  pl.BlockSpec((cl, cl), lambda b, c: (0, 0))],
        out_specs=pl.BlockSpec((heads, cl), lambda b, c: (0, b * nc + c)),
        out_shape=jax.ShapeDtypeStruct((heads, n), F32),
        scratch_shapes=[pltpu.VMEM((heads, LANES), F32)],
        compiler_params=_params(("parallel", "arbitrary")),
        name="forget_cumsum",
    )(lft, tri)


def _att_body(qi_ref, ki_ref, q_ref, k_ref, v_ref, nd_ref, o_ref, m_ref, l_ref, acc_ref, *, hd, pairs):
    step = pl.program_id(2)
    qi = qi_ref[step]
    ki = ki_ref[step]
    tq = q_ref.shape[0]
    tk = k_ref.shape[2]
    is_a = lax.broadcasted_iota(I32, (1, LANES), 1) < hd

    @pl.when(ki == 0)
    def _():
        m_ref[...] = jnp.full(m_ref.shape, NEG_INF, F32)
        l_ref[...] = jnp.zeros_like(l_ref)
        acc_ref[...] = jnp.zeros_like(acc_ref)

    def process(diagonal):
        if diagonal:
            causal = (lax.broadcasted_iota(I32, (tq, tk), 1) <= lax.broadcasted_iota(I32, (tq, tk), 0))
        for g in range(pairs):
            lanes = slice(g * LANES, (g + 1) * LANES)
            q = q_ref[:, lanes]
            kt = k_ref[0, lanes, :]
            vt = v_ref[0, lanes, :]
            nd = nd_ref[g]
            zero = jnp.zeros_like(q)
            pvs = []
            alphas = []
            for h in range(2):
                qh = jnp.where(is_a, q, zero) if h == 0 else jnp.where(is_a, zero, q)
                s = _dot(qh, kt) + nd[h:h + 1, :]
                if diagonal:
                    s = jnp.where(causal, s, NEG_INF)
                m_old = m_ref[2 * g + h]
                m_new = jnp.maximum(m_old, jnp.max(s, axis=1, keepdims=True))
                alpha = jnp.exp2(m_old - m_new)
                p = jnp.exp2(s - jnp.tile(m_new, (1, tk // LANES)))
                l_ref[2 * g + h] = alpha * l_ref[2 * g + h] + jnp.sum(p, axis=1, keepdims=True)
                m_ref[2 * g + h] = m_new
                pvs.append(_dot_nt(p.astype(BF16), vt))
                alphas.append(alpha)
            acc_ref[:, lanes] = (acc_ref[:, lanes] * jnp.where(is_a, alphas[0], alphas[1])
                                 + jnp.where(is_a, pvs[0], pvs[1]))

    @pl.when(ki < qi)
    def _():
        process(False)

    @pl.when(ki == qi)
    def _():
        process(True)
        for g in range(pairs):
            lanes = slice(g * LANES, (g + 1) * LANES)
            o_ref[:, lanes] = (acc_ref[:, lanes]
                               / jnp.where(is_a, l_ref[2 * g], l_ref[2 * g + 1])).astype(BF16)


def _attention_prompt(q_b, kt_b, vt_b, negd, batch, seq):
    n, aw = q_b.shape
    pairs = ATT_PAIRS_PER_STEP
    groups = aw // (pairs * LANES)
    w = pairs * LANES
    t = _pick(seq, (512, 256, 128))
    nq = seq // t
    steps = [(a, b) for a in range(nq) for b in range(a + 1)]
    qi_tab = jnp.asarray([s[0] for s in steps], I32)
    ki_tab = jnp.asarray([s[1] for s in steps], I32)
    grid_spec = pltpu.PrefetchScalarGridSpec(
        num_scalar_prefetch=2,
        grid=(batch, groups, len(steps)),
        in_specs=[
            pl.BlockSpec((t, w), lambda b, g, s, qt, kt: (b * nq + qt[s], g)),
            pl.BlockSpec((1, w, t), lambda b, g, s, qt, kt: (b, g, kt[s])),
            pl.BlockSpec((1, w, t), lambda b, g, s, qt, kt: (b, g, kt[s])),
            pl.BlockSpec((pairs, 2, t), lambda b, g, s, qt, kt: (g, 0, b * nq + kt[s])),
        ],
        out_specs=pl.BlockSpec((t, w), lambda b, g, s, qt, kt: (b * nq + qt[s], g)),
        scratch_shapes=[pltpu.VMEM((2 * pairs, t, LANES), F32), pltpu.VMEM((2 * pairs, t, LANES), F32),
                        pltpu.VMEM((t, w), F32)],
    )
    return pl.pallas_call(
        functools.partial(_att_body, hd=HEAD_DIM, pairs=pairs),
        grid_spec=grid_spec,
        out_shape=jax.ShapeDtypeStruct((n, aw), BF16),
        compiler_params=_params(("parallel", "parallel", "arbitrary")),
        name="attention_prompt",
    )(qi_tab, ki_tab, q_b, kt_b, vt_b, negd)


def _pool_suffix_body(lf_ref, tri_ref, ones_ref, w_ref, t_ref):
    x = lf_ref[...]
    w_ref[...] = _dot3(x, tri_ref[...])
    t_ref[...] = _dot3(x, ones_ref[...])


def _pool_suffix(lf_rows):
    n, page = lf_rows.shape
    tp = _pick(n, (2048, 1024, 512, 256, 128, 64, 32, 16, 8, n))
    tri = jnp.tril(jnp.ones((page, page), F32), -1).astype(BF16)
    ones = jnp.ones((page, page), BF16)
    blk = pl.BlockSpec((tp, page), lambda i: (i, 0))
    mat = pl.BlockSpec((page, page), lambda i: (0, 0))
    return pl.pallas_call(
        _pool_suffix_body,
        grid=(n // tp,),
        in_specs=[blk, mat, mat],
        out_specs=(blk, blk),
        out_shape=(jax.ShapeDtypeStruct((n, page), F32), jax.ShapeDtypeStruct((n, page), F32)),
        compiler_params=_params(("parallel",)),
        name="pool_forget_suffix",
    )(lf_rows, tri, ones)


def _satt_body(pt_ref, *refs, n_pages, heads):
    k_refs = refs[:n_pages]
    v_refs = refs[n_pages:2 * n_pages]
    w_refs = refs[2 * n_pages:3 * n_pages]
    t_refs = refs[3 * n_pages:4 * n_pages]
    q_ref, kn_ref, vn_ref, lfn_ref, o_ref, s_ref = refs[4 * n_pages:]
    hd, page = k_refs[0].shape[2:]
    q = q_ref[0].astype(F32)
    s_new = jnp.sum(q * kn_ref[0], axis=1, keepdims=True)
    q_cols = [jnp.broadcast_to(q[h:h + 1, :], (page, hd)).T for h in range(heads)]
    later = lfn_ref[0]
    m = s_new
    for j in reversed(range(n_pages)):
        for h in range(heads):
            s_ref[j, h:h + 1, :] = jnp.sum(k_refs[j][0, h] * q_cols[h], axis=0, keepdims=True)
        s = s_ref[j] + (w_refs[j][0] + later)
        s_ref[j] = s
        m = jnp.maximum(m, jnp.max(s, axis=1, keepdims=True))
        later = later + t_refs[j][0]
    p_new = jnp.exp(s_new - m)
    l = p_new
    for j in range(n_pages):
        p = jnp.exp(s_ref[j] - m)
        s_ref[j] = p
        l = l + jnp.sum(p, axis=1, keepdims=True)
    vn = vn_ref[0]
    for h in range(heads):
        acc = jnp.zeros((hd, page), F32)
        for j in range(n_pages):
            acc = acc + v_refs[j][0, h] * s_ref[j, h:h + 1, :]
        o = jnp.sum(acc.T, axis=0, keepdims=True) + p_new[h:h + 1, :] * vn[h:h + 1, :]
        o_ref[0, h:h + 1, :] = (o / l[h:h + 1, :]).astype(BF16)


def _attention_sample(q_b, k_new, v_new, lf_new, cache_kt, cache_vt, cache_lft, page_table):
    n, aw = q_b.shape
    n_pages = page_table.shape[1]
    pool, heads, hd, page = cache_kt.shape
    within, totals = _pool_suffix(cache_lft.reshape(pool * heads, page))
    within = within.reshape(pool, heads, page)
    totals = totals.reshape(pool, heads, page)
    lfn_lanes = jnp.broadcast_to(lf_new[:, :, None], (n, heads, page))

    def kv_spec(j):
        return pl.BlockSpec((1, heads, hd, page), lambda i, pt, j=j: (pt[i, j], 0, 0, 0))

    def row_spec(j):
        return pl.BlockSpec((1, heads, page), lambda i, pt, j=j: (pt[i, j], 0, 0))

    per_seq = lambda shape: pl.BlockSpec((1,) + shape, lambda i, pt: (i, 0, 0))
    grid_spec = pltpu.PrefetchScalarGridSpec(
        num_scalar_prefetch=1,
        grid=(n,),
        in_specs=([kv_spec(j) for j in range(n_pages)] + [kv_spec(j) for j in range(n_pages)]
                  + [row_spec(j) for j in range(n_pages)] + [row_spec(j) for j in range(n_pages)]
                  + [per_seq((heads, hd)), per_seq((heads, hd)), per_seq((heads, hd)), per_seq((heads, page))]),
        out_specs=per_seq((heads, hd)),
        scratch_shapes=[pltpu.VMEM((n_pages, heads, page), F32)],
    )
    out = pl.pallas_call(
        functools.partial(_satt_body, n_pages=n_pages, heads=heads),
        grid_spec=grid_spec,
        out_shape=jax.ShapeDtypeStruct((n, heads, hd), BF16),
        compiler_params=_params(("arbitrary",)),
        name="attention_sample",
    )(page_table, *([cache_kt] * n_pages), *([cache_vt] * n_pages), *([within] * n_pages), *([totals] * n_pages),
      q_b.reshape(n, heads, hd), k_new.reshape(n, heads, hd), v_new.reshape(n, heads, hd), lfn_lanes)
    return out.reshape(n, aw)


def _mix_body(x_ref, c_ref, a_ref, p_ref, wo_ref, g1_ref, b1_ref, wr_ref, br_ref,
              wpg_ref, bpg_ref, wpp_ref, h_ref, base_ref, idx_ref, gate_ref, cnt_ref, *, alpha, cc):
    mix = _dot(c_ref[...], wo_ref[:cc, :]) + _dot(a_ref[...], wo_ref[cc:, :])
    h = _layer_norm(alpha * x_ref[...] + mix, g1_ref[...], b1_ref[...])
    h_ref[...] = h
    hb = h.astype(BF16)
    hl = (h - hb.astype(F32)).astype(BF16)
    both = _dot(hb, wr_ref[...])
    logits = both[:, :LANES] + both[:, LANES:] + _dot(hl, wr_ref[:, :LANES]) + br_ref[...]

    tm = logits.shape[0]
    lane = lax.broadcasted_iota(I32, (tm, LANES), 1).astype(F32)
    vals, idxs = [], []
    l = logits
    for _ in range(TOP_K):
        m = jnp.max(l, axis=1, keepdims=True)
        ix = jnp.min(jnp.where(l == m, lane, float(LANES)), axis=1, keepdims=True)
        vals.append(m)
        idxs.append(ix)
        l = jnp.where(lane == ix, -jnp.inf, l)
    es = [jnp.exp(v - vals[0]) for v in vals]
    den = es[0] + es[1] + es[2] + es[3]
    idx_w = jnp.zeros((tm, LANES), F32)
    gate_w = jnp.zeros((tm, LANES), F32)
    sel = jnp.zeros((tm, LANES), F32)
    for r in range(TOP_K):
        idx_w = jnp.where(lane == float(r), idxs[r], idx_w)
        gate_w = jnp.where(lane == float(r), es[r] / den, gate_w)
        sel = sel + (lane == idxs[r]).astype(F32)
    idx_ref[...] = idx_w
    gate_ref[...] = gate_w

    @pl.when(pl.program_id(0) == 0)
    def _():
        cnt_ref[...] = jnp.zeros_like(cnt_ref)

    cnt_ref[...] += jnp.sum(sel, axis=0, keepdims=True)

    ple = _sigmoid(_dot(hb, wpg_ref[...]) + bpg_ref[...]) * _dot(p_ref[...].astype(BF16), wpp_ref[...])
    base_ref[...] = alpha * h + ple


def _mix(x, conv_b, att_b, p, wo_b, g1, b1, wr_pair, br_row, wpg_b, bpg, wpp_b, alpha):
    n, d = x.shape
    cc = conv_b.shape[1]
    aw = att_b.shape[1]
    pd = p.shape[1]
    tm = _pick(n, (512, 256, 128, 64, 32, 16, 8))
    row = lambda i: (i, 0)
    const = lambda i: (0, 0)
    outs = (
        jax.ShapeDtypeStruct((n, d), F32),
        jax.ShapeDtypeStruct((n, d), F32),
        jax.ShapeDtypeStruct((n, LANES), F32),
        jax.ShapeDtypeStruct((n, LANES), F32),
        jax.ShapeDtypeStruct((1, LANES), F32),
    )
    return pl.pallas_call(
        functools.partial(_mix_body, alpha=alpha, cc=cc),
        grid=(n // tm,),
        in_specs=[
            pl.BlockSpec((tm, d), row), pl.BlockSpec((tm, cc), row), pl.BlockSpec((tm, aw), row),
            pl.BlockSpec((tm, pd), row),
            pl.BlockSpec((d, d), const), pl.BlockSpec((1, d), const), pl.BlockSpec((1, d), const),
            pl.BlockSpec((d, 2 * LANES), const), pl.BlockSpec((1, LANES), const),
            pl.BlockSpec((d, d), const), pl.BlockSpec((1, d), const), pl.BlockSpec((pd, d), const),
        ],
        out_specs=(
            pl.BlockSpec((tm, d), row), pl.BlockSpec((tm, d), row),
            pl.BlockSpec((tm, LANES), row), pl.BlockSpec((tm, LANES), row),
            pl.BlockSpec((1, LANES), const),
        ),
        out_shape=outs,
        compiler_params=_params(("arbitrary",)),
        name="out_proj_router",
    )(x, conv_b, att_b, p, wo_b, g1, b1, wr_pair, br_row, wpg_b, bpg, wpp_b)


def _positions_body(idx_ref, start_ref, tri_ref, o_ref, carry_ref):
    @pl.when(pl.program_id(0) == 0)
    def _():
        carry_ref[...] = jnp.zeros_like(carry_ref)

    ids = idx_ref[...]
    tm = ids.shape[0]
    lane = lax.broadcasted_iota(I32, (tm, LANES), 1).astype(F32)
    hot = [lane == ids[:, r:r + 1] for r in range(TOP_K)]
    sel = jnp.zeros((tm, LANES), F32)
    for r in range(TOP_K):
        sel = sel + hot[r].astype(F32)
    rank = _dot(tri_ref[...], sel.astype(BF16)) + carry_ref[...]
    pos = rank + start_ref[...]
    out = jnp.zeros((tm, LANES), F32)
    for r in range(TOP_K):
        d = jnp.sum(jnp.where(hot[r], pos, 0.0), axis=1, keepdims=True)
        out = jnp.where(lane == float(r), d, out)
    o_ref[...] = out.astype(I32)
    carry_ref[...] += jnp.sum(sel, axis=0, keepdims=True)


def _positions(idx_w, start_row):
    n = idx_w.shape[0]
    tm = _pick(n, (512, 384, 256, 128, 64, 32, 16, 8))
    tri = jnp.tril(jnp.ones((tm, tm), F32), -1).astype(BF16)
    return pl.pallas_call(
        _positions_body,
        grid=(n // tm,),
        in_specs=[pl.BlockSpec((tm, LANES), lambda i: (i, 0)),
                  pl.BlockSpec((1, LANES), lambda i: (0, 0)),
                  pl.BlockSpec((tm, tm), lambda i: (0, 0))],
        out_specs=pl.BlockSpec((tm, LANES), lambda i: (i, 0)),
        out_shape=jax.ShapeDtypeStruct((n, LANES), I32),
        scratch_shapes=[pltpu.VMEM((1, LANES), F32)],
        compiler_params=_params(("arbitrary",)),
        name="route_positions",
    )(idx_w, start_row, tri)


def _dispatch_body(dest_ref, h_ref, xs_in_ref, xs_ref, sem, *, tb):
    del xs_in_ref

    def issue(t, c):
        for r in range(TOP_K):
            d = dest_ref[0, 0, t * TOP_K + r]
            pltpu.make_async_copy(h_ref.at[pl.ds(t, 1)], xs_ref.at[pl.ds(d, 1)], sem).start(priority=r % 2)
        return c

    lax.fori_loop(0, tb, issue, 0)
    for r in range(TOP_K):
        pltpu.make_async_copy(h_ref, xs_ref.at[pl.ds(0, tb)], sem).wait()


def _dispatch(h, dest, xs):
    n, d = h.shape
    tb = _pick(n, (512, 256, 128, 64, 32, 16, 8))
    dest3 = dest.reshape(n // tb, 1, tb * TOP_K)
    return pl.pallas_call(
        functools.partial(_dispatch_body, tb=tb),
        grid=(n // tb,),
        in_specs=[
            pl.BlockSpec((1, 1, tb * TOP_K), lambda i: (i, 0, 0), memory_space=pltpu.SMEM),
            pl.BlockSpec((tb, d), lambda i: (i, 0)),
            pl.BlockSpec(memory_space=pl.ANY),
        ],
        out_specs=pl.BlockSpec(memory_space=pl.ANY),
        out_shape=jax.ShapeDtypeStruct(xs.shape, xs.dtype),
        scratch_shapes=[pltpu.SemaphoreType.DMA],
        input_output_aliases={2: 0},
        compiler_params=_params(("arbitrary",)),
        name="moe_dispatch",
    )(dest3, h, xs)


def _expert_body(be_ref, nu_ref, x_ref, wg_ref, bg_ref, wu_ref, bu_ref, wd_ref, bd_ref, o_ref,
                 wgb_ref, wub_ref, wdb_ref):
    i = pl.program_id(0)

    @pl.when(i < nu_ref[0])
    def _():
        @pl.when((i == 0) | (be_ref[i] != be_ref[jnp.maximum(i - 1, 0)]))
        def _():
            wgb_ref[...] = wg_ref[0].astype(BF16)
            wub_ref[...] = wu_ref[0].astype(BF16)
            wdb_ref[...] = wd_ref[0].astype(BF16)

        x = x_ref[...].astype(BF16)
        g = jnp.minimum(_dot(x, wgb_ref[...]) + bg_ref[0], SWIGLU_LIMIT)
        u = jnp.clip(_dot(x, wub_ref[...]) + bu_ref[0], -SWIGLU_LIMIT, SWIGLU_LIMIT)
        hidden = (u + 1.0) * (g * _sigmoid(SWIGLU_ALPHA * g))
        o_ref[...] = _dot(hidden.astype(BF16), wdb_ref[...]) + bd_ref[0]

    @pl.when(i >= nu_ref[0])
    def _():
        o_ref[...] = jnp.zeros_like(o_ref)


def _experts(xs, block_e, n_used, w_gate, b_gate, w_up, b_up, w_down, b_down):
    rows, d = xs.shape
    n_e, _, de = w_gate.shape
    n_blocks = rows // MOE_BLOCK
    last = lambda i, be, nu: (jnp.minimum(i, nu[0] - 1), 0)
    wsel = lambda i, be, nu: (be[i], 0, 0)
    grid_spec = pltpu.PrefetchScalarGridSpec(
        num_scalar_prefetch=2,
        grid=(n_blocks,),
        in_specs=[
            pl.BlockSpec((MOE_BLOCK, d), last),
            pl.BlockSpec((1, d, de), wsel), pl.BlockSpec((1, 1, de), wsel),
            pl.BlockSpec((1, d, de), wsel), pl.BlockSpec((1, 1, de), wsel),
            pl.BlockSpec((1, de, d), wsel), pl.BlockSpec((1, 1, d), wsel),
        ],
        out_specs=pl.BlockSpec((MOE_BLOCK, d), lambda i, be, nu: (i, 0)),
        scratch_shapes=[pltpu.VMEM((d, de), BF16), pltpu.VMEM((d, de), BF16), pltpu.VMEM((de, d), BF16)],
    )
    return pl.pallas_call(
        _expert_body,
        grid_spec=grid_spec,
        out_shape=jax.ShapeDtypeStruct((rows, d), F32),
        compiler_params=_params(("arbitrary",)),
        name="moe_experts",
    )(block_e, n_used, xs, w_gate, b_gate[:, None, :], w_up, b_up[:, None, :], w_down, b_down[:, None, :])


def _combine_body(dest_ref, base_ref, gate_ref, y_ref, g2_ref, b2_ref, o_ref, buf_ref, sem, *, tb):
    def issue(t, c):
        for r in range(TOP_K):
            d = dest_ref[0, 0, t * TOP_K + r]
            pltpu.make_async_copy(y_ref.at[pl.ds(d, 1)], buf_ref.at[r, pl.ds(t, 1)], sem).start(priority=r % 2)
        return c

    lax.fori_loop(0, tb, issue, 0)
    for r in range(TOP_K):
        pltpu.make_async_copy(y_ref.at[pl.ds(0, tb)], buf_ref.at[r], sem).wait()

    gate = gate_ref[...]
    y = base_ref[...]
    for r in range(TOP_K):
        y = y + gate[:, r:r + 1] * buf_ref[r]
    o_ref[...] = _layer_norm(y, g2_ref[...], b2_ref[...])


def _combine(base, gate_w, dest, y_rows, g2, b2):
    n, d = base.shape
    tb = _pick(n, (512, 256, 128, 64, 32, 16, 8))
    dest3 = dest.reshape(n // tb, 1, tb * TOP_K)
    row = lambda i: (i, 0)
    const = lambda i: (0, 0)
    return pl.pallas_call(
        functools.partial(_combine_body, tb=tb),
        grid=(n // tb,),
        in_specs=[
            pl.BlockSpec((1, 1, tb * TOP_K), lambda i: (i, 0, 0), memory_space=pltpu.SMEM),
            pl.BlockSpec((tb, d), row),
            pl.BlockSpec((tb, LANES), row),
            pl.BlockSpec(memory_space=pl.ANY),
            pl.BlockSpec((1, d), const), pl.BlockSpec((1, d), const),
        ],
        out_specs=pl.BlockSpec((tb, d), row),
        out_shape=jax.ShapeDtypeStruct((n, d), F32),
        scratch_shapes=[pltpu.VMEM((TOP_K, tb, d), F32), pltpu.SemaphoreType.DMA],
        compiler_params=_params(("arbitrary",)),
        name="moe_combine",
    )(dest3, base, gate_w, y_rows, g2, b2)


def _layer(x_p, x_s, ck, cv, clf, hist_s, page_table, p_p, p_s, lw, alpha):
    batch, seq, d = x_p.shape
    n_s = x_s.shape[0]
    n_p = batch * seq
    cc = lw["w_dw"].shape[1]
    aw = ATT_HEADS * HEAD_DIM
    heads = ATT_HEADS
    nz = 2 * cc + 3 * aw
    row = lambda v: v[None, :]

    nq_cols = 2 * cc + aw
    w_in_t = lw["w_in"].T
    w_in_b = lw["w_in"][:, :nq_cols].astype(BF16)
    b_in_row = row(lw["b_in"][:nq_cols])
    wkv_t = w_in_t[nq_cols:nz].astype(BF16)
    bkv = lw["b_in"][nq_cols:nz]
    wft_b = w_in_t[nz:].astype(BF16)
    bft_col = lw["b_in"][nz:, None]
    b_dw, gg, gb = row(lw["b_dw"]), row(lw["gn_gain"]), row(lw["gn_bias"])
    gsz = cc // CONV_GROUPS
    grp = jnp.arange(cc) // gsz
    gmat = ((grp[:, None] == grp[None, :]).astype(F32) / gsz).astype(BF16)
    wo_b = lw["w_out"].astype(BF16)
    wr = jnp.pad(lw["w_router"], ((0, 0), (0, LANES - N_EXPERTS)))
    wr_hi = wr.astype(BF16)
    wr_lo = (wr - wr_hi.astype(F32)).astype(BF16)
    wr_pair = jnp.concatenate([wr_hi, wr_lo], axis=1)
    br_row = row(jnp.pad(lw["b_router"], (0, LANES - N_EXPERTS), constant_values=NEG_INF))
    wpg_b = lw["w_ple_gate"].astype(BF16)
    wpp_b = lw["w_ple_proj"].astype(BF16)

    xp2 = x_p.reshape(n_p, d)
    xs2 = x_s.reshape(n_s, d)

    u_p, q_p, kt_p, vt_p, ktb_p, vtb_p, lft_p = _in_proj(xp2, w_in_b, b_in_row, wkv_t, bkv[:, None], wft_b, bft_col,
                                                         cc, aw, HEAD_DIM ** -0.5 * LOG2_E, batch=batch)
    conv_p = _conv_prompt(u_p, batch, seq, lw["w_dw"], b_dw, gg, gb, gmat)
    negd = _neg_cumsum(lft_p, batch, seq).reshape(heads // 2, 2, n_p)
    att_p = _attention_prompt(q_p, ktb_p, vtb_p, negd, batch, seq)
    h_p, base_p, idx_p, gate_p, cnt_p = _mix(xp2, conv_p, att_p, p_p.reshape(n_p, -1), wo_b,
                                             row(lw["ln1_gain"]), row(lw["ln1_bias"]), wr_pair, br_row,
                                             wpg_b, row(lw["b_ple_gate"]), wpp_b, alpha)

    u_s, q_s, k_s, v_s, lft_s = _in_proj(xs2, w_in_b, b_in_row, wkv_t, bkv[None, :], wft_b, bft_col,
                                         cc, aw, HEAD_DIM ** -0.5)
    conv_s = _conv_sample(hist_s.transpose(1, 0, 2), u_s, lw["w_dw"], b_dw, gg, gb, gmat)
    lf_s = lft_s.T
    att_s = _attention_sample(q_s, k_s, v_s, lf_s, ck.transpose(0, 2, 3, 1), cv.transpose(0, 2, 3, 1),
                              clf.transpose(0, 2, 1), page_table)
    h_s, base_s, idx_s, gate_s, cnt_s = _mix(xs2, conv_s, att_s, p_s.reshape(n_s, -1), wo_b,
                                             row(lw["ln1_gain"]), row(lw["ln1_bias"]), wr_pair, br_row,
                                             wpg_b, row(lw["b_ple_gate"]), wpp_b, alpha)

    n_tok = n_p + n_s
    counts = (cnt_p + cnt_s)[0].astype(I32)
    padded = (counts + MOE_BLOCK - 1) // MOE_BLOCK * MOE_BLOCK
    pad_end = jnp.cumsum(padded)
    pad_start = pad_end - padded
    n_blocks = -(-(n_tok * TOP_K) // MOE_BLOCK) + N_EXPERTS
    n_used = (pad_end[N_EXPERTS - 1] // MOE_BLOCK).astype(I32).reshape(1)
    block_row0 = jnp.arange(n_blocks, dtype=I32) * MOE_BLOCK
    block_e = jnp.minimum(jnp.sum((pad_end[None, :N_EXPERTS] <= block_row0[:, None]).astype(I32), axis=1),
                          N_EXPERTS - 1)
    dest = _positions(jnp.concatenate([idx_p, idx_s], axis=0), pad_start.astype(F32)[None, :])[:, :TOP_K]
    dest_p, dest_s = dest[:n_p], dest[n_p:]

    xs_rows = jnp.zeros((n_blocks * MOE_BLOCK, d), F32)
    xs_rows = _dispatch(h_p, dest_p, xs_rows)
    xs_rows = _dispatch(h_s, dest_s, xs_rows)
    y_rows = _experts(xs_rows, block_e, n_used, lw["w_gate"], lw["b_gate"], lw["w_up"], lw["b_up"],
                      lw["w_down"], lw["b_down"])
    g2, b2 = row(lw["ln2_gain"]), row(lw["ln2_bias"])
    y_p = _combine(base_p, gate_p, dest_p, y_rows, g2, b2).reshape(batch, seq, d)
    y_s = _combine(base_s, gate_s, dest_s, y_rows, g2, b2).reshape(n_s, 1, d)

    lf_p = lft_p.T.reshape(batch, seq, heads)
    hist_p = u_p.reshape(batch, seq, cc)[:, seq - (DW_WIDTH - 1):]
    hist_new_s = jnp.concatenate([hist_s[:, 1:], u_s[:, None, :]], axis=1)
    return (y_p, y_s,
            kt_p.reshape(batch, heads, HEAD_DIM, seq).transpose(0, 3, 1, 2),
            vt_p.reshape(batch, heads, HEAD_DIM, seq).transpose(0, 3, 1, 2), lf_p, hist_p,
            k_s.reshape(n_s, 1, heads, HEAD_DIM), v_s.reshape(n_s, 1, heads, HEAD_DIM),
            lf_s.reshape(n_s, 1, heads), hist_new_s)


def kernel(x_prompt, x_sample, cache_k, cache_v, cache_logf, state_conv, page_table, p_prompt, p_sample,
           w_in, b_in, w_dw, b_dw, gn_gain, gn_bias, w_out, ln1_gain, ln1_bias,
           w_router, b_router, w_gate, b_gate, w_up, b_up, w_down, b_down,
           w_ple_gate, b_ple_gate, w_ple_proj, ln2_gain, ln2_bias):
    depth = w_in.shape[0]
    if depth != 1 or x_sample.shape[1] != 1:
        raise NotImplementedError("one layer and one new token per sample sequence are supported")
    alpha = (2.0 * depth) ** 0.25
    lw = dict(w_in=w_in[0], b_in=b_in[0], w_dw=w_dw[0], b_dw=b_dw[0], gn_gain=gn_gain[0], gn_bias=gn_bias[0],
              w_out=w_out[0], ln1_gain=ln1_gain[0], ln1_bias=ln1_bias[0],
              w_router=w_router[0], b_router=b_router[0], w_gate=w_gate[0], b_gate=b_gate[0],
              w_up=w_up[0], b_up=b_up[0], w_down=w_down[0], b_down=b_down[0],
              w_ple_gate=w_ple_gate[0], b_ple_gate=b_ple_gate[0], w_ple_proj=w_ple_proj[0],
              ln2_gain=ln2_gain[0], ln2_bias=ln2_bias[0])
    outs = _layer(x_prompt, x_sample[:, 0], cache_k[0], cache_v[0], cache_logf[0], state_conv[0], page_table,
                  p_prompt[0], p_sample[0], lw, alpha)
    (y_p, y_s, k_p, v_p, lf_p, c_p, k_s, v_s, lf_s, c_s) = outs
    return (y_p, y_s, k_p[None], v_p[None], lf_p[None], c_p[None], k_s[None], v_s[None], lf_s[None], c_s[None])
```

```python
import functools
import math

import jax
import jax.numpy as jnp
from jax import lax
from jax.experimental import pallas as pl
from jax.experimental.pallas import tpu as pltpu

F32 = jnp.float32
BF16 = jnp.bfloat16
I32 = jnp.int32

ATT_HEADS = 8
HEAD_DIM = 64
CONV_GROUPS = 8
DW_WIDTH = 31
N_EXPERTS = 32
TOP_K = 4
SWIGLU_LIMIT = 7.0
SWIGLU_ALPHA = 1.702
LN_EPS = 1e-5
NEG_INF = -1e30
LOG2_E = 1.4426950408889634

LANES = 128
SUBLANES = 8
CONV_HALO = 32
MOE_BLOCK = 512
ATT_PAIRS_PER_STEP = 4
VMEM_LIMIT = 56 * 1024 * 1024


def _pick(n, cands):
    for c in cands:
        if n % c == 0:
            return c
    raise ValueError(f"no block size in {cands} divides {n}")


def _params(sem, vmem=VMEM_LIMIT):
    return pltpu.CompilerParams(dimension_semantics=sem, vmem_limit_bytes=vmem)


def _sigmoid(x):
    return 1.0 / (1.0 + jnp.exp(-x))


def _log_sigmoid(x):
    return jnp.minimum(x, 0.0) - jnp.log1p(jnp.exp(-jnp.abs(x)))


def _dot(a, b):
    return jnp.dot(a, b, preferred_element_type=F32)


def _dot_nt(a, b):
    return lax.dot_general(a, b, (((1,), (1,)), ((), ())), preferred_element_type=F32)


def _split3(x):
    x1 = x.astype(BF16)
    r1 = x - x1.astype(F32)
    x2 = r1.astype(BF16)
    x3 = (r1 - x2.astype(F32)).astype(BF16)
    return x1, x2, x3


def _dot3(x, m):
    x1, x2, x3 = _split3(x)
    return _dot(x1, m) + _dot(x2, m) + _dot(x3, m)


def _layer_norm(x, g, b):
    mu = jnp.mean(x, axis=-1, keepdims=True)
    d = x - mu
    var = jnp.mean(d * d, axis=-1, keepdims=True)
    return d * lax.rsqrt(var + LN_EPS) * g + b


def _inproj_common(x_ref, w_ref, b_ref, wft_ref, bft_ref, u_ref, q_ref, lft_ref, *, cc, aw, scale):
    x = x_ref[...].astype(BF16)

    def seg(lo, hi):
        return _dot(x, w_ref[:, lo:hi]) + b_ref[:, lo:hi]

    u_ref[...] = seg(0, cc) * _sigmoid(seg(cc, 2 * cc))
    q_ref[...] = (seg(2 * cc, 2 * cc + aw) * scale).astype(BF16)
    lft_ref[...] = _log_sigmoid(_dot_nt(wft_ref[...], x) + bft_ref[...])
    return x


def _inproj_prompt_body(x_ref, w_ref, b_ref, wkv_ref, bkv_ref, wft_ref, bft_ref,
                        u_ref, q_ref, kt_ref, vt_ref, ktb_ref, vtb_ref, lft_ref, *, cc, aw, scale):
    x = _inproj_common(x_ref, w_ref, b_ref, wft_ref, bft_ref, u_ref, q_ref, lft_ref, cc=cc, aw=aw, scale=scale)
    kt = _dot_nt(wkv_ref[0:aw, :], x) + bkv_ref[0:aw, :]
    kt_ref[0] = kt
    ktb_ref[0] = kt.astype(BF16)
    vt = _dot_nt(wkv_ref[aw:2 * aw, :], x) + bkv_ref[aw:2 * aw, :]
    vt_ref[0] = vt
    vtb_ref[0] = vt.astype(BF16)


def _inproj_sample_body(x_ref, w_ref, b_ref, wkv_ref, bkv_ref, wft_ref, bft_ref,
                        u_ref, q_ref, k_ref, v_ref, lft_ref, *, cc, aw, scale):
    x = _inproj_common(x_ref, w_ref, b_ref, wft_ref, bft_ref, u_ref, q_ref, lft_ref, cc=cc, aw=aw, scale=scale)
    k_ref[...] = _dot_nt(x, wkv_ref[0:aw, :]) + bkv_ref[:, 0:aw]
    v_ref[...] = _dot_nt(x, wkv_ref[aw:2 * aw, :]) + bkv_ref[:, aw:2 * aw]


def _in_proj(x, w_b, b_row, wkv_t, bkv, wft_b, bft_col, cc, aw, q_scale, batch=None):
    n, d = x.shape
    nw = w_b.shape[1]
    heads = wft_b.shape[0]
    const = lambda *_: (0, 0)
    w_specs = [pl.BlockSpec((d, nw), const), pl.BlockSpec((1, nw), const),
               pl.BlockSpec((2 * aw, d), const), pl.BlockSpec(bkv.shape, const),
               pl.BlockSpec((heads, d), const), pl.BlockSpec((heads, 1), const)]
    common_shapes = (jax.ShapeDtypeStruct((n, cc), F32),
                     jax.ShapeDtypeStruct((n, aw), BF16))
    lft_shape = jax.ShapeDtypeStruct((heads, n), F32)
    if batch is None:
        tm = _pick(n, (512, 384, 256, 128, 64, 32, 16, 8))
        row = lambda i: (i, 0)
        return pl.pallas_call(
            functools.partial(_inproj_sample_body, cc=cc, aw=aw, scale=q_scale),
            grid=(n // tm,),
            in_specs=[pl.BlockSpec((tm, d), row)] + w_specs,
            out_specs=(pl.BlockSpec((tm, cc), row), pl.BlockSpec((tm, aw), row),
                       pl.BlockSpec((tm, aw), row), pl.BlockSpec((tm, aw), row),
                       pl.BlockSpec((heads, tm), lambda i: (0, i))),
            out_shape=common_shapes + (jax.ShapeDtypeStruct((n, aw), F32), jax.ShapeDtypeStruct((n, aw), F32),
                                       lft_shape),
            compiler_params=_params(("parallel",)),
            name="in_proj_sample",
        )(x, w_b, b_row, wkv_t, bkv, wft_b, bft_col)
    seq = n // batch
    tm = _pick(seq, (512, 256, 128))
    nt = seq // tm
    row = lambda b, i: (b * nt + i, 0)
    tr = lambda b, i: (b, 0, i)
    t_f32 = jax.ShapeDtypeStruct((batch, aw, seq), F32)
    t_b16 = jax.ShapeDtypeStruct((batch, aw, seq), BF16)
    return pl.pallas_call(
        functools.partial(_inproj_prompt_body, cc=cc, aw=aw, scale=q_scale),
        grid=(batch, nt),
        in_specs=[pl.BlockSpec((tm, d), row)] + w_specs,
        out_specs=(pl.BlockSpec((tm, cc), row), pl.BlockSpec((tm, aw), row),
                   pl.BlockSpec((1, aw, tm), tr), pl.BlockSpec((1, aw, tm), tr),
                   pl.BlockSpec((1, aw, tm), tr), pl.BlockSpec((1, aw, tm), tr),
                   pl.BlockSpec((heads, tm), lambda b, i: (0, b * nt + i))),
        out_shape=common_shapes + (t_f32, t_f32, t_b16, t_b16, lft_shape),
        compiler_params=_params(("parallel", "parallel")),
        name="in_proj_prompt",
    )(x, w_b, b_row, wkv_t, bkv, wft_b, bft_col)


def _group_norm_silu(y, m, gg, gb):
    mean = _dot3(y, m)
    d = y - mean
    var = _dot3(d * d, m)
    yn = d * lax.rsqrt(var + LN_EPS) * gg + gb
    return yn * _sigmoid(yn)


def _conv_body(u_ref, halo_ref, w_ref, b_ref, gg_ref, gb_ref, m_ref, o_ref, ext_ref, sh_ref, y_ref, *, ts, rc):
    i = pl.program_id(1)
    c = u_ref.shape[1]
    rows = CONV_HALO + ts
    halo = halo_ref[...]
    ext_ref[0:CONV_HALO, :] = jnp.where(i > 0, halo, jnp.zeros_like(halo))
    ext_ref[CONV_HALO:rows, :] = u_ref[...]
    ext_ref[rows:rows + SUBLANES, :] = jnp.zeros((SUBLANES, c), F32)
    for r in range(SUBLANES):
        sh_ref[r] = ext_ref[r:r + rows, :]
    off = CONV_HALO - (DW_WIDTH - 1)
    for r0 in range(0, ts, rc):
        acc = jnp.broadcast_to(b_ref[...], (rc, c))
        for j in range(DW_WIDTH):
            s = r0 + off + j
            r = s % SUBLANES
            acc = acc + w_ref[j:j + 1, :] * sh_ref[r, s - r:s - r + rc, :]
        y_ref[r0:r0 + rc, :] = acc
    o_ref[...] = _group_norm_silu(y_ref[...], m_ref[...], gg_ref[...], gb_ref[...]).astype(BF16)


def _conv_prompt(u, batch, seq, w_dw, b_row, gg_row, gb_row, gmat):
    n, c = u.shape
    ts = _pick(seq, (512, 256, 128, 64, 32))
    rc = min(ts, 64)
    nt = seq // ts
    hb = ts // CONV_HALO
    rows = CONV_HALO + ts
    const = lambda b, i: (0, 0)
    return pl.pallas_call(
        functools.partial(_conv_body, ts=ts, rc=rc),
        grid=(batch, nt),
        in_specs=[
            pl.BlockSpec((ts, c), lambda b, i: (b * nt + i, 0)),
            pl.BlockSpec((CONV_HALO, c), lambda b, i: (jnp.maximum((b * nt + i) * hb - 1, 0), 0)),
            pl.BlockSpec((DW_WIDTH, c), const),
            pl.BlockSpec((1, c), const),
            pl.BlockSpec((1, c), const),
            pl.BlockSpec((1, c), const),
            pl.BlockSpec((c, c), const),
        ],
        out_specs=pl.BlockSpec((ts, c), lambda b, i: (b * nt + i, 0)),
        out_shape=jax.ShapeDtypeStruct((n, c), BF16),
        scratch_shapes=[pltpu.VMEM((rows + SUBLANES, c), F32), pltpu.VMEM((SUBLANES, rows, c), F32),
                        pltpu.VMEM((ts, c), F32)],
        compiler_params=_params(("parallel", "parallel")),
        name="conv_prompt",
    )(u, u, w_dw, b_row, gg_row, gb_row, gmat)


def _conv_sample_body(hist_ref, u_ref, w_ref, b_ref, gg_ref, gb_ref, m_ref, o_ref):
    acc = b_ref[...] + w_ref[DW_WIDTH - 1:DW_WIDTH, :] * u_ref[...]
    for j in range(DW_WIDTH - 1):
        acc = acc + w_ref[j:j + 1, :] * hist_ref[j]
    o_ref[...] = _group_norm_silu(acc, m_ref[...], gg_ref[...], gb_ref[...]).astype(BF16)


def _conv_sample(hist_t, u, w_dw, b_row, gg_row, gb_row, gmat):
    n, c = u.shape
    full2 = lambda i: (0, 0)
    return pl.pallas_call(
        _conv_sample_body,
        grid=(1,),
        in_specs=[
            pl.BlockSpec(hist_t.shape, lambda i: (0, 0, 0)),
            pl.BlockSpec((n, c), full2),
            pl.BlockSpec((DW_WIDTH, c), full2),
            pl.BlockSpec((1, c), full2),
            pl.BlockSpec((1, c), full2),
            pl.BlockSpec((1, c), full2),
            pl.BlockSpec((c, c), full2),
        ],
        out_specs=pl.BlockSpec((n, c), full2),
        out_shape=jax.ShapeDtypeStruct((n, c), BF16),
        compiler_params=_params(("arbitrary",)),
        name="conv_sample",
    )(hist_t, u, w_dw, b_row, gg_row, gb_row, gmat)


def _cumsum_body(lf_ref, tri_ref, o_ref, carry_ref):
    @pl.when(pl.program_id(1) == 0)
    def _():
        carry_ref[...] = jnp.zeros_like(carry_ref)

    cl = lf_ref.shape[1]
    cum = _dot3(lf_ref[...], tri_ref[...]) + carry_ref[:, 0:1]
    o_ref[...] = -LOG2_E * cum
    carry_ref[...] = jnp.broadcast_to(cum[:, cl - 1:cl], carry_ref.shape)


def _neg_cumsum(lft, batch, seq):
    heads, n = lft.shape
    cl = _pick(seq, (512, 256, 128))
    nc = seq // cl
    tri = jnp.triu(jnp.ones((cl, cl), F32)).astype(BF16)
    return pl.pallas_call(
        _cumsum_body,
        grid=(batch, nc),
        in_specs=[pl.BlockSpec((heads, cl), lambda b, c: (0, b * nc + c)),
                  pl.BlockSpec((cl, cl), lambda b, c: (0, 0))],
        out_specs=pl.BlockSpec((heads, cl), lambda b, c: (0, b * nc + c)),
        out_shape=jax.ShapeDtypeStruct((heads, n), F32),
        scratch_shapes=[pltpu.VMEM((heads, LANES), F32)],
        compiler_params=_params(("parallel", "arbitrary")),
        name="forget_cumsum",
    )(lft, tri)


def _att_body(qi_ref, ki_ref, q_ref, k_ref, v_ref, nd_ref, o_ref, m_ref, l_ref, acc_ref, *, hd, pairs):
    step = pl.program_id(2)
    qi = qi_ref[step]
    ki = ki_ref[step]
    tq = q_ref.shape[0]
    tk = k_ref.shape[2]
    is_a = lax.broadcasted_iota(I32, (1, LANES), 1) < hd

    @pl.when(ki == 0)
    def _():
        m_ref[...] = jnp.full(m_ref.shape, NEG_INF, F32)
        l_ref[...] = jnp.zeros_like(l_ref)
        acc_ref[...] = jnp.zeros_like(acc_ref)

    def process(diagonal):
        if diagonal:
            causal = (lax.broadcasted_iota(I32, (tq, tk), 1) <= lax.broadcasted_iota(I32, (tq, tk), 0))
        for g in range(pairs):
            lanes = slice(g * LANES, (g + 1) * LANES)
            q = q_ref[:, lanes]
            kt = k_ref[0, lanes, :]
            vt = v_ref[0, lanes, :]
            nd = nd_ref[g]
            zero = jnp.zeros_like(q)
            pvs = []
            alphas = []
            for h in range(2):
                qh = jnp.where(is_a, q, zero) if h == 0 else jnp.where(is_a, zero, q)
                s = _dot(qh, kt) + nd[h:h + 1, :]
                if diagonal:
                    s = jnp.where(causal, s, NEG_INF)
                m_old = m_ref[2 * g + h]
                m_new = jnp.maximum(m_old, jnp.max(s, axis=1, keepdims=True))
                alpha = jnp.exp2(m_old - m_new)
                p = jnp.exp2(s - jnp.tile(m_new, (1, tk // LANES)))
                l_ref[2 * g + h] = alpha * l_ref[2 * g + h] + jnp.sum(p, axis=1, keepdims=True)
                m_ref[2 * g + h] = m_new
                pvs.append(_dot_nt(p.astype(BF16), vt))
                alphas.append(alpha)
            acc_ref[:, lanes] = (acc_ref[:, lanes] * jnp.where(is_a, alphas[0], alphas[1])
                                 + jnp.where(is_a, pvs[0], pvs[1]))

    @pl.when(ki < qi)
    def _():
        process(False)

    @pl.when(ki == qi)
    def _():
        process(True)
        for g in range(pairs):
            lanes = slice(g * LANES, (g + 1) * LANES)
            o_ref[:, lanes] = (acc_ref[:, lanes]
                               / jnp.where(is_a, l_ref[2 * g], l_ref[2 * g + 1])).astype(BF16)


def _attention_prompt(q_b, kt_b, vt_b, negd, batch, seq):
    n, aw = q_b.shape
    pairs = ATT_PAIRS_PER_STEP
    groups = aw // (pairs * LANES)
    w = pairs * LANES
    t = _pick(seq, (512, 256, 128))
    nq = seq // t
    steps = [(a, b) for a in range(nq) for b in range(a + 1)]
    qi_tab = jnp.asarray([s[0] for s in steps], I32)
    ki_tab = jnp.asarray([s[1] for s in steps], I32)
    grid_spec = pltpu.PrefetchScalarGridSpec(
        num_scalar_prefetch=2,
        grid=(batch, groups, len(steps)),
        in_specs=[
            pl.BlockSpec((t, w), lambda b, g, s, qt, kt: (b * nq + qt[s], g)),
            pl.BlockSpec((1, w, t), lambda b, g, s, qt, kt: (b, g, kt[s])),
            pl.BlockSpec((1, w, t), lambda b, g, s, qt, kt: (b, g, kt[s])),
            pl.BlockSpec((pairs, 2, t), lambda b, g, s, qt, kt: (g, 0, b * nq + kt[s])),
        ],
        out_specs=pl.BlockSpec((t, w), lambda b, g, s, qt, kt: (b * nq + qt[s], g)),
        scratch_shapes=[pltpu.VMEM((2 * pairs, t, LANES), F32), pltpu.VMEM((2 * pairs, t, LANES), F32),
                        pltpu.VMEM((t, w), F32)],
    )
    return pl.pallas_call(
        functools.partial(_att_body, hd=HEAD_DIM, pairs=pairs),
        grid_spec=grid_spec,
        out_shape=jax.ShapeDtypeStruct((n, aw), BF16),
        compiler_params=_params(("parallel", "parallel", "arbitrary")),
        name="attention_prompt",
    )(qi_tab, ki_tab, q_b, kt_b, vt_b, negd)


def _pool_suffix_body(lf_ref, tri_ref, ones_ref, w_ref, t_ref):
    x = lf_ref[...]
    w_ref[...] = _dot3(x, tri_ref[...])
    t_ref[...] = _dot3(x, ones_ref[...])


def _pool_suffix(lf_rows):
    n, page = lf_rows.shape
    tp = _pick(n, (2048, 1024, 512, 256, 128, 64, 32, 16, 8, n))
    tri = jnp.tril(jnp.ones((page, page), F32), -1).astype(BF16)
    ones = jnp.ones((page, page), BF16)
    blk = pl.BlockSpec((tp, page), lambda i: (i, 0))
    mat = pl.BlockSpec((page, page), lambda i: (0, 0))
    return pl.pallas_call(
        _pool_suffix_body,
        grid=(n // tp,),
        in_specs=[blk, mat, mat],
        out_specs=(blk, blk),
        out_shape=(jax.ShapeDtypeStruct((n, page), F32), jax.ShapeDtypeStruct((n, page), F32)),
        compiler_params=_params(("parallel",)),
        name="pool_forget_suffix",
    )(lf_rows, tri, ones)


def _satt_body(pt_ref, *refs, n_pages, heads):
    k_refs = refs[:n_pages]
    v_refs = refs[n_pages:2 * n_pages]
    w_refs = refs[2 * n_pages:3 * n_pages]
    t_refs = refs[3 * n_pages:4 * n_pages]
    q_ref, kn_ref, vn_ref, lfn_ref, o_ref, s_ref = refs[4 * n_pages:]
    hd, page = k_refs[0].shape[2:]
    q = q_ref[0].astype(F32)
    s_new = jnp.sum(q * kn_ref[0], axis=1, keepdims=True)
    q_cols = [jnp.broadcast_to(q[h:h + 1, :], (page, hd)).T for h in range(heads)]
    later = lfn_ref[0]
    m = s_new
    for j in reversed(range(n_pages)):
        for h in range(heads):
            s_ref[j, h:h + 1, :] = jnp.sum(k_refs[j][0, h] * q_cols[h], axis=0, keepdims=True)
        s = s_ref[j] + (w_refs[j][0] + later)
        s_ref[j] = s
        m = jnp.maximum(m, jnp.max(s, axis=1, keepdims=True))
        later = later + t_refs[j][0]
    p_new = jnp.exp(s_new - m)
    l = p_new
    for j in range(n_pages):
        p = jnp.exp(s_ref[j] - m)
        s_ref[j] = p
        l = l + jnp.sum(p, axis=1, keepdims=True)
    vn = vn_ref[0]
    for h in range(heads):
        acc = jnp.zeros((hd, page), F32)
        for j in range(n_pages):
            acc = acc + v_refs[j][0, h] * s_ref[j, h:h + 1, :]
        o = jnp.sum(acc.T, axis=0, keepdims=True) + p_new[h:h + 1, :] * vn[h:h + 1, :]
        o_ref[0, h:h + 1, :] = (o / l[h:h + 1, :]).astype(BF16)


def _attention_sample(q_b, k_new, v_new, lf_new, cache_kt, cache_vt, cache_lft, page_table):
    n, aw = q_b.shape
    n_pages = page_table.shape[1]
    pool, heads, hd, page = cache_kt.shape
    within, totals = _pool_suffix(cache_lft.reshape(pool * heads, page))
    within = within.reshape(pool, heads, page)
    totals = totals.reshape(pool, heads, page)
    lfn_lanes = jnp.broadcast_to(lf_new[:, :, None], (n, heads, page))

    def kv_spec(j):
        return pl.BlockSpec((1, heads, hd, page), lambda i, pt, j=j: (pt[i, j], 0, 0, 0))

    def row_spec(j):
        return pl.BlockSpec((1, heads, page), lambda i, pt, j=j: (pt[i, j], 0, 0))

    per_seq = lambda shape: pl.BlockSpec((1,) + shape, lambda i, pt: (i, 0, 0))
    grid_spec = pltpu.PrefetchScalarGridSpec(
        num_scalar_prefetch=1,
        grid=(n,),
        in_specs=([kv_spec(j) for j in range(n_pages)] + [kv_spec(j) for j in range(n_pages)]
                  + [row_spec(j) for j in range(n_pages)] + [row_spec(j) for j in range(n_pages)]
                  + [per_seq((heads, hd)), per_seq((heads, hd)), per_seq((heads, hd)), per_seq((heads, page))]),
        out_specs=per_seq((heads, hd)),
        scratch_shapes=[pltpu.VMEM((n_pages, heads, page), F32)],
    )
    out = pl.pallas_call(
        functools.partial(_satt_body, n_pages=n_pages, heads=heads),
        grid_spec=grid_spec,
        out_shape=jax.ShapeDtypeStruct((n, heads, hd), BF16),
        compiler_params=_params(("arbitrary",)),
        name="attention_sample",
    )(page_table, *([cache_kt] * n_pages), *([cache_vt] * n_pages), *([within] * n_pages), *([totals] * n_pages),
      q_b.reshape(n, heads, hd), k_new.reshape(n, heads, hd), v_new.reshape(n, heads, hd), lfn_lanes)
    return out.reshape(n, aw)


def _mix_body(x_ref, c_ref, a_ref, p_ref, wo_ref, g1_ref, b1_ref, wr_ref, br_ref,
              wpg_ref, bpg_ref, wpp_ref, h_ref, base_ref, idx_ref, gate_ref, cnt_ref, *, alpha, cc):
    mix = _dot(c_ref[...], wo_ref[:cc, :]) + _dot(a_ref[...], wo_ref[cc:, :])
    h = _layer_norm(alpha * x_ref[...] + mix, g1_ref[...], b1_ref[...])
    h_ref[...] = h
    hb = h.astype(BF16)
    hl = (h - hb.astype(F32)).astype(BF16)
    both = _dot(hb, wr_ref[...])
    logits = both[:, :LANES] + both[:, LANES:] + _dot(hl, wr_ref[:, :LANES]) + br_ref[...]

    tm = logits.shape[0]
    lane = lax.broadcasted_iota(I32, (tm, LANES), 1).astype(F32)
    vals, idxs = [], []
    l = logits
    for _ in range(TOP_K):
        m = jnp.max(l, axis=1, keepdims=True)
        ix = jnp.min(jnp.where(l == m, lane, float(LANES)), axis=1, keepdims=True)
        vals.append(m)
        idxs.append(ix)
        l = jnp.where(lane == ix, -jnp.inf, l)
    es = [jnp.exp(v - vals[0]) for v in vals]
    den = es[0] + es[1] + es[2] + es[3]
    idx_w = jnp.zeros((tm, LANES), F32)
    gate_w = jnp.zeros((tm, LANES), F32)
    sel = jnp.zeros((tm, LANES), F32)
    for r in range(TOP_K):
        idx_w = jnp.where(lane == float(r), idxs[r], idx_w)
        gate_w = jnp.where(lane == float(r), es[r] / den, gate_w)
        sel = sel + (lane == idxs[r]).astype(F32)
    idx_ref[...] = idx_w
    gate_ref[...] = gate_w

    @pl.when(pl.program_id(0) == 0)
    def _():
        cnt_ref[...] = jnp.zeros_like(cnt_ref)

    cnt_ref[...] += jnp.sum(sel, axis=0, keepdims=True)

    ple = _sigmoid(_dot(hb, wpg_ref[...]) + bpg_ref[...]) * _dot(p_ref[...].astype(BF16), wpp_ref[...])
    base_ref[...] = alpha * h + ple


def _mix(x, conv_b, att_b, p, wo_b, g1, b1, wr_pair, br_row, wpg_b, bpg, wpp_b, alpha):
    n, d = x.shape
    cc = conv_b.shape[1]
    aw = att_b.shape[1]
    pd = p.shape[1]
    tm = _pick(n, (512, 256, 128, 64, 32, 16, 8))
    row = lambda i: (i, 0)
    const = lambda i: (0, 0)
    outs = (
        jax.ShapeDtypeStruct((n, d), F32),
        jax.ShapeDtypeStruct((n, d), F32),
        jax.ShapeDtypeStruct((n, LANES), F32),
        jax.ShapeDtypeStruct((n, LANES), F32),
        jax.ShapeDtypeStruct((1, LANES), F32),
    )
    return pl.pallas_call(
        functools.partial(_mix_body, alpha=alpha, cc=cc),
        grid=(n // tm,),
        in_specs=[
            pl.BlockSpec((tm, d), row), pl.BlockSpec((tm, cc), row), pl.BlockSpec((tm, aw), row),
            pl.BlockSpec((tm, pd), row),
            pl.BlockSpec((d, d), const), pl.BlockSpec((1, d), const), pl.BlockSpec((1, d), const),
            pl.BlockSpec((d, 2 * LANES), const), pl.BlockSpec((1, LANES), const),
            pl.BlockSpec((d, d), const), pl.BlockSpec((1, d), const), pl.BlockSpec((pd, d), const),
        ],
        out_specs=(
            pl.BlockSpec((tm, d), row), pl.BlockSpec((tm, d), row),
            pl.BlockSpec((tm, LANES), row), pl.BlockSpec((tm, LANES), row),
            pl.BlockSpec((1, LANES), const),
        ),
        out_shape=outs,
        compiler_params=_params(("arbitrary",)),
        name="out_proj_router",
    )(x, conv_b, att_b, p, wo_b, g1, b1, wr_pair, br_row, wpg_b, bpg, wpp_b)


def _positions_body(idx_ref, start_ref, tri_ref, o_ref, carry_ref):
    @pl.when(pl.program_id(0) == 0)
    def _():
        carry_ref[...] = jnp.zeros_like(carry_ref)

    ids = idx_ref[...]
    tm = ids.shape[0]
    lane = lax.broadcasted_iota(I32, (tm, LANES), 1).astype(F32)
    hot = [lane == ids[:, r:r + 1] for r in range(TOP_K)]
    sel = jnp.zeros((tm, LANES), F32)
    for r in range(TOP_K):
        sel = sel + hot[r].astype(F32)
    rank = _dot(tri_ref[...], sel.astype(BF16)) + carry_ref[...]
    pos = rank + start_ref[...]
    out = jnp.zeros((tm, LANES), F32)
    for r in range(TOP_K):
        d = jnp.sum(jnp.where(hot[r], pos, 0.0), axis=1, keepdims=True)
        out = jnp.where(lane == float(r), d, out)
    o_ref[...] = out.astype(I32)
    carry_ref[...] += jnp.sum(sel, axis=0, keepdims=True)


def _positions(idx_w, start_row):
    n = idx_w.shape[0]
    tm = _pick(n, (512, 384, 256, 128, 64, 32, 16, 8))
    tri = jnp.tril(jnp.ones((tm, tm), F32), -1).astype(BF16)
    return pl.pallas_call(
        _positions_body,
        grid=(n // tm,),
        in_specs=[pl.BlockSpec((tm, LANES), lambda i: (i, 0)),
                  pl.BlockSpec((1, LANES), lambda i: (0, 0)),
                  pl.BlockSpec((tm, tm), lambda i: (0, 0))],
        out_specs=pl.BlockSpec((tm, LANES), lambda i: (i, 0)),
        out_shape=jax.ShapeDtypeStruct((n, LANES), I32),
        scratch_shapes=[pltpu.VMEM((1, LANES), F32)],
        compiler_params=_params(("arbitrary",)),
        name="route_positions",
    )(idx_w, start_row, tri)


def _dispatch_body(dest_ref, h_ref, xs_in_ref, inv_src_ref, xs_ref, inv_ref, sem, inv_sem, *, tb, tok0, n_tok):
    del xs_in_ref
    i = pl.program_id(0)

    @pl.when(i == 0)
    def _():
        cp = pltpu.make_async_copy(inv_src_ref, inv_ref, inv_sem)
        cp.start()
        cp.wait()

    tok_base = tok0 + i * tb

    def issue(t, c):
        for r in range(TOP_K):
            d = dest_ref[0, 0, t * TOP_K + r]
            inv_ref[d] = r * n_tok + tok_base + t
            pltpu.make_async_copy(h_ref.at[pl.ds(t, 1)], xs_ref.at[pl.ds(d, 1)], sem).start(priority=r % 2)
        return c

    lax.fori_loop(0, tb, issue, 0)
    for r in range(TOP_K):
        pltpu.make_async_copy(h_ref, xs_ref.at[pl.ds(0, tb)], sem).wait()


def _dispatch(h, dest, xs, inv, tok0, n_tok):
    n, d = h.shape
    tb = _pick(n, (512, 256, 128, 64, 32, 16, 8))
    dest3 = dest.reshape(n // tb, 1, tb * TOP_K)
    return pl.pallas_call(
        functools.partial(_dispatch_body, tb=tb, tok0=tok0, n_tok=n_tok),
        grid=(n // tb,),
        in_specs=[
            pl.BlockSpec((1, 1, tb * TOP_K), lambda i: (i, 0, 0), memory_space=pltpu.SMEM),
            pl.BlockSpec((tb, d), lambda i: (i, 0)),
            pl.BlockSpec(memory_space=pl.ANY),
            pl.BlockSpec(memory_space=pl.ANY),
        ],
        out_specs=(pl.BlockSpec(memory_space=pl.ANY), pl.BlockSpec(memory_space=pltpu.SMEM)),
        out_shape=(jax.ShapeDtypeStruct(xs.shape, xs.dtype), jax.ShapeDtypeStruct(inv.shape, inv.dtype)),
        scratch_shapes=[pltpu.SemaphoreType.DMA, pltpu.SemaphoreType.DMA],
        input_output_aliases={2: 0},
        compiler_params=_params(("arbitrary",)),
        name="moe_dispatch",
    )(dest3, h, xs, inv)


def _expert_body(be_ref, nu_ref, inv_ref, x_ref, wg_ref, bg_ref, wu_ref, bu_ref, wd_ref, bd_ref, out_ref,
                 wgb_ref, wub_ref, wdb_ref, ya_ref, yb_ref, sem_a, sem_b, sem_z, *, n_slots):
    i = pl.program_id(0)
    half = MOE_BLOCK // 2

    def scatter(y_ref, row0, sem):
        for k in range(half):
            slot = inv_ref[row0 + k]
            pltpu.make_async_copy(y_ref.at[pl.ds(k, 1)], out_ref.at[pl.ds(slot, 1)], sem).start(priority=k % 2)

    def drain(y_ref, sem):
        pltpu.make_async_copy(y_ref, out_ref.at[pl.ds(0, half)], sem).wait()

    def ffn(rows):
        x = x_ref[rows, :].astype(BF16)
        g = jnp.minimum(_dot(x, wgb_ref[...]) + bg_ref[0], SWIGLU_LIMIT)
        u = jnp.clip(_dot(x, wub_ref[...]) + bu_ref[0], -SWIGLU_LIMIT, SWIGLU_LIMIT)
        hidden = (u + 1.0) * (g * _sigmoid(SWIGLU_ALPHA * g))
        return _dot(hidden.astype(BF16), wdb_ref[...]) + bd_ref[0]

    @pl.when(i == 0)
    def _():
        ya_ref[...] = jnp.zeros_like(ya_ref)
        yb_ref[...] = jnp.zeros_like(yb_ref)
        pltpu.make_async_copy(ya_ref, out_ref.at[pl.ds(n_slots, half)], sem_a).start()
        cp = pltpu.make_async_copy(yb_ref, out_ref.at[pl.ds(n_slots + half, half)], sem_z)
        cp.start()
        cp.wait()

    @pl.when(i < nu_ref[0])
    def _():
        @pl.when((i == 0) | (be_ref[i] != be_ref[jnp.maximum(i - 1, 0)]))
        def _():
            wgb_ref[...] = wg_ref[0].astype(BF16)
            wub_ref[...] = wu_ref[0].astype(BF16)
            wdb_ref[...] = wd_ref[0].astype(BF16)

        drain(ya_ref, sem_a)
        scatter(yb_ref, i * MOE_BLOCK + half, sem_b)
        ya_ref[...] = ffn(slice(0, half))
        drain(yb_ref, sem_b)
        scatter(ya_ref, (i + 1) * MOE_BLOCK, sem_a)
        yb_ref[...] = ffn(slice(half, MOE_BLOCK))

    @pl.when(i == nu_ref[0])
    def _():
        drain(ya_ref, sem_a)
        scatter(yb_ref, i * MOE_BLOCK + half, sem_b)
        drain(yb_ref, sem_b)


def _experts(xs, block_e, n_used, inv_shifted, n_slots, w_gate, b_gate, w_up, b_up, w_down, b_down):
    rows, d = xs.shape
    n_e, _, de = w_gate.shape
    n_blocks = rows // MOE_BLOCK
    half = MOE_BLOCK // 2
    last = lambda i, be, nu, inv: (jnp.minimum(i, nu[0] - 1), 0)
    wsel = lambda i, be, nu, inv: (be[i], 0, 0)
    grid_spec = pltpu.PrefetchScalarGridSpec(
        num_scalar_prefetch=3,
        grid=(n_blocks,),
        in_specs=[
            pl.BlockSpec((MOE_BLOCK, d), last),
            pl.BlockSpec((1, d, de), wsel), pl.BlockSpec((1, 1, de), wsel),
            pl.BlockSpec((1, d, de), wsel), pl.BlockSpec((1, 1, de), wsel),
            pl.BlockSpec((1, de, d), wsel), pl.BlockSpec((1, 1, d), wsel),
        ],
        out_specs=pl.BlockSpec(memory_space=pl.ANY),
        scratch_shapes=[pltpu.VMEM((d, de), BF16), pltpu.VMEM((d, de), BF16), pltpu.VMEM((de, d), BF16),
                        pltpu.VMEM((half, d), F32), pltpu.VMEM((half, d), F32),
                        pltpu.SemaphoreType.DMA, pltpu.SemaphoreType.DMA, pltpu.SemaphoreType.DMA],
    )
    return pl.pallas_call(
        functools.partial(_expert_body, n_slots=n_slots),
        grid_spec=grid_spec,
        out_shape=jax.ShapeDtypeStruct((n_slots + MOE_BLOCK, d), F32),
        compiler_params=_params(("arbitrary",)),
        name="moe_experts",
    )(block_e, n_used, inv_shifted, xs, w_gate, b_gate[:, None, :], w_up, b_up[:, None, :],
      w_down, b_down[:, None, :])


def _combine_body(base_ref, gate_ref, *refs):
    y_refs = refs[:TOP_K]
    g2_ref, b2_ref, o_ref = refs[TOP_K:]
    gate = gate_ref[...]
    y = base_ref[...]
    for r in range(TOP_K):
        y = y + gate[:, r:r + 1] * y_refs[r][...]
    o_ref[...] = _layer_norm(y, g2_ref[...], b2_ref[...])


def _combine(base, gate_w, slots, tok0, n_tok, g2, b2):
    n, d = base.shape
    tb = _pick(math.gcd(n, math.gcd(tok0, n_tok)), (512, 256, 128, 64, 32, 16, 8))
    row = lambda i: (i, 0)
    const = lambda i: (0, 0)

    def slot_spec(r):
        return pl.BlockSpec((tb, d), lambda i, r=r: ((r * n_tok + tok0) // tb + i, 0))

    return pl.pallas_call(
        _combine_body,
        grid=(n // tb,),
        in_specs=([pl.BlockSpec((tb, d), row), pl.BlockSpec((tb, LANES), row)]
                  + [slot_spec(r) for r in range(TOP_K)]
                  + [pl.BlockSpec((1, d), const), pl.BlockSpec((1, d), const)]),
        out_specs=pl.BlockSpec((tb, d), row),
        out_shape=jax.ShapeDtypeStruct((n, d), F32),
        compiler_params=_params(("parallel",)),
        name="moe_combine",
    )(base, gate_w, *([slots] * TOP_K), g2, b2)


def _layer(x_p, x_s, ck, cv, clf, hist_s, page_table, p_p, p_s, lw, alpha):
    batch, seq, d = x_p.shape
    n_s = x_s.shape[0]
    n_p = batch * seq
    cc = lw["w_dw"].shape[1]
    aw = ATT_HEADS * HEAD_DIM
    heads = ATT_HEADS
    nz = 2 * cc + 3 * aw
    row = lambda v: v[None, :]

    nq_cols = 2 * cc + aw
    w_in_t = lw["w_in"].T
    w_in_b = lw["w_in"][:, :nq_cols].astype(BF16)
    b_in_row = row(lw["b_in"][:nq_cols])
    wkv_t = w_in_t[nq_cols:nz].astype(BF16)
    bkv = lw["b_in"][nq_cols:nz]
    wft_b = w_in_t[nz:].astype(BF16)
    bft_col = lw["b_in"][nz:, None]
    b_dw, gg, gb = row(lw["b_dw"]), row(lw["gn_gain"]), row(lw["gn_bias"])
    gsz = cc // CONV_GROUPS
    grp = jnp.arange(cc) // gsz
    gmat = ((grp[:, None] == grp[None, :]).astype(F32) / gsz).astype(BF16)
    wo_b = lw["w_out"].astype(BF16)
    wr = jnp.pad(lw["w_router"], ((0, 0), (0, LANES - N_EXPERTS)))
    wr_hi = wr.astype(BF16)
    wr_lo = (wr - wr_hi.astype(F32)).astype(BF16)
    wr_pair = jnp.concatenate([wr_hi, wr_lo], axis=1)
    br_row = row(jnp.pad(lw["b_router"], (0, LANES - N_EXPERTS), constant_values=NEG_INF))
    wpg_b = lw["w_ple_gate"].astype(BF16)
    wpp_b = lw["w_ple_proj"].astype(BF16)

    xp2 = x_p.reshape(n_p, d)
    xs2 = x_s.reshape(n_s, d)

    u_p, q_p, kt_p, vt_p, ktb_p, vtb_p, lft_p = _in_proj(xp2, w_in_b, b_in_row, wkv_t, bkv[:, None], wft_b, bft_col,
                                                         cc, aw, HEAD_DIM ** -0.5 * LOG2_E, batch=batch)
    conv_p = _conv_prompt(u_p, batch, seq, lw["w_dw"], b_dw, gg, gb, gmat)
    negd = _neg_cumsum(lft_p, batch, seq).reshape(heads // 2, 2, n_p)
    att_p = _attention_prompt(q_p, ktb_p, vtb_p, negd, batch, seq)
    h_p, base_p, idx_p, gate_p, cnt_p = _mix(xp2, conv_p, att_p, p_p.reshape(n_p, -1), wo_b,
                                             row(lw["ln1_gain"]), row(lw["ln1_bias"]), wr_pair, br_row,
                                             wpg_b, row(lw["b_ple_gate"]), wpp_b, alpha)

    u_s, q_s, k_s, v_s, lft_s = _in_proj(xs2, w_in_b, b_in_row, wkv_t, bkv[None, :], wft_b, bft_col,
                                         cc, aw, HEAD_DIM ** -0.5)
    conv_s = _conv_sample(hist_s.transpose(1, 0, 2), u_s, lw["w_dw"], b_dw, gg, gb, gmat)
    lf_s = lft_s.T
    att_s = _attention_sample(q_s, k_s, v_s, lf_s, ck.transpose(0, 2, 3, 1), cv.transpose(0, 2, 3, 1),
                              clf.transpose(0, 2, 1), page_table)
    h_s, base_s, idx_s, gate_s, cnt_s = _mix(xs2, conv_s, att_s, p_s.reshape(n_s, -1), wo_b,
                                             row(lw["ln1_gain"]), row(lw["ln1_bias"]), wr_pair, br_row,
                                             wpg_b, row(lw["b_ple_gate"]), wpp_b, alpha)

    n_tok = n_p + n_s
    counts = (cnt_p + cnt_s)[0].astype(I32)
    padded = (counts + MOE_BLOCK - 1) // MOE_BLOCK * MOE_BLOCK
    pad_end = jnp.cumsum(padded)
    pad_start = pad_end - padded
    n_blocks = -(-(n_tok * TOP_K) // MOE_BLOCK) + N_EXPERTS + 1
    n_used = (pad_end[N_EXPERTS - 1] // MOE_BLOCK).astype(I32).reshape(1)
    block_row0 = jnp.arange(n_blocks, dtype=I32) * MOE_BLOCK
    block_e = jnp.minimum(jnp.sum((pad_end[None, :N_EXPERTS] <= block_row0[:, None]).astype(I32), axis=1),
                          N_EXPERTS - 1)
    dest = _positions(jnp.concatenate([idx_p, idx_s], axis=0), pad_start.astype(F32)[None, :])[:, :TOP_K]
    dest_p, dest_s = dest[:n_p], dest[n_p:]

    n_rows = n_blocks * MOE_BLOCK
    n_slots = TOP_K * n_tok
    inv = n_slots + jnp.arange(n_rows, dtype=I32) % MOE_BLOCK
    xs_rows = jnp.zeros((n_rows, d), F32)
    xs_rows, inv = _dispatch(h_p, dest_p, xs_rows, inv, 0, n_tok)
    xs_rows, inv = _dispatch(h_s, dest_s, xs_rows, inv, n_p, n_tok)
    inv_shifted = jnp.concatenate([n_slots + jnp.arange(MOE_BLOCK, dtype=I32), inv])
    slots = _experts(xs_rows, block_e, n_used, inv_shifted, n_slots, lw["w_gate"], lw["b_gate"],
                     lw["w_up"], lw["b_up"], lw["w_down"], lw["b_down"])
    g2, b2 = row(lw["ln2_gain"]), row(lw["ln2_bias"])
    y_p = _combine(base_p, gate_p, slots, 0, n_tok, g2, b2).reshape(batch, seq, d)
    y_s = _combine(base_s, gate_s, slots, n_p, n_tok, g2, b2).reshape(n_s, 1, d)

    lf_p = lft_p.T.reshape(batch, seq, heads)
    hist_p = u_p.reshape(batch, seq, cc)[:, seq - (DW_WIDTH - 1):]
    hist_new_s = jnp.concatenate([hist_s[:, 1:], u_s[:, None, :]], axis=1)
    return (y_p, y_s,
            kt_p.reshape(batch, heads, HEAD_DIM, seq).transpose(0, 3, 1, 2),
            vt_p.reshape(batch, heads, HEAD_DIM, seq).transpose(0, 3, 1, 2), lf_p, hist_p,
            k_s.reshape(n_s, 1, heads, HEAD_DIM), v_s.reshape(n_s, 1, heads, HEAD_DIM),
            lf_s.reshape(n_s, 1, heads), hist_new_s)


def kernel(x_prompt, x_sample, cache_k, cache_v, cache_logf, state_conv, page_table, p_prompt, p_sample,
           w_in, b_in, w_dw, b_dw, gn_gain, gn_bias, w_out, ln1_gain, ln1_bias,
           w_router, b_router, w_gate, b_gate, w_up, b_up, w_down, b_down,
           w_ple_gate, b_ple_gate, w_ple_proj, ln2_gain, ln2_bias):
    depth = w_in.shape[0]
    if depth != 1 or x_sample.shape[1] != 1:
        raise NotImplementedError("one layer and one new token per sample sequence are supported")
    alpha = (2.0 * depth) ** 0.25
    lw = dict(w_in=w_in[0], b_in=b_in[0], w_dw=w_dw[0], b_dw=b_dw[0], gn_gain=gn_gain[0], gn_bias=gn_bias[0],
              w_out=w_out[0], ln1_gain=ln1_gain[0], ln1_bias=ln1_bias[0],
              w_router=w_router[0], b_router=b_router[0], w_gate=w_gate[0], b_gate=b_gate[0],
              w_up=w_up[0], b_up=b_up[0], w_down=w_down[0], b_down=b_down[0],
              w_ple_gate=w_ple_gate[0], b_ple_gate=b_ple_gate[0], w_ple_proj=w_ple_proj[0],
              ln2_gain=ln2_gain[0], ln2_bias=ln2_bias[0])
    outs = _layer(x_prompt, x_sample[:, 0], cache_k[0], cache_v[0], cache_logf[0], state_conv[0], page_table,
                  p_prompt[0], p_sample[0], lw, alpha)
    (y_p, y_s, k_p, v_p, lf_p, c_p, k_s, v_s, lf_s, c_s) = outs
    return (y_p, y_s, k_p[None], v_p[None], lf_p[None], c_p[None], k_s[None], v_s[None], lf_s[None], c_s[None])
```

```python
import functools
import math

import jax
import jax.numpy as jnp
from jax import lax
from jax.experimental import pallas as pl
from jax.experimental.pallas import tpu as pltpu

F32 = jnp.float32
BF16 = jnp.bfloat16
I32 = jnp.int32

ATT_HEADS = 8
HEAD_DIM = 64
CONV_GROUPS = 8
DW_WIDTH = 31
N_EXPERTS = 32
TOP_K = 4
SWIGLU_LIMIT = 7.0
SWIGLU_ALPHA = 1.702
LN_EPS = 1e-5
NEG_INF = -1e30
LOG2_E = 1.4426950408889634

LANES = 128
SUBLANES = 8
CONV_HALO = 32
MOE_BLOCK = 512
ATT_PAIRS_PER_STEP = 4
VMEM_LIMIT = 56 * 1024 * 1024


def _pick(n, cands):
    for c in cands:
        if n % c == 0:
            return c
    raise ValueError(f"no block size in {cands} divides {n}")


def _params(sem, vmem=VMEM_LIMIT):
    return pltpu.CompilerParams(dimension_semantics=sem, vmem_limit_bytes=vmem)


def _sigmoid(x):
    return 1.0 / (1.0 + jnp.exp(-x))


def _log_sigmoid(x):
    return jnp.minimum(x, 0.0) - jnp.log1p(jnp.exp(-jnp.abs(x)))


def _dot(a, b):
    return jnp.dot(a, b, preferred_element_type=F32)


def _dot_nt(a, b):
    return lax.dot_general(a, b, (((1,), (1,)), ((), ())), preferred_element_type=F32)


def _split3(x):
    x1 = x.astype(BF16)
    r1 = x - x1.astype(F32)
    x2 = r1.astype(BF16)
    x3 = (r1 - x2.astype(F32)).astype(BF16)
    return x1, x2, x3


def _dot3(x, m):
    x1, x2, x3 = _split3(x)
    return _dot(x1, m) + _dot(x2, m) + _dot(x3, m)


def _layer_norm(x, g, b):
    mu = jnp.mean(x, axis=-1, keepdims=True)
    d = x - mu
    var = jnp.mean(d * d, axis=-1, keepdims=True)
    return d * lax.rsqrt(var + LN_EPS) * g + b


def _inproj_common(x_ref, w_ref, b_ref, wft_ref, bft_ref, u_ref, q_ref, lft_ref, *, cc, aw, scale):
    x = x_ref[...].astype(BF16)

    def seg(lo, hi):
        return _dot(x, w_ref[:, lo:hi]) + b_ref[:, lo:hi]

    u_ref[...] = seg(0, cc) * _sigmoid(seg(cc, 2 * cc))
    q_ref[...] = (seg(2 * cc, 2 * cc + aw) * scale).astype(BF16)
    lft_ref[...] = _log_sigmoid(_dot_nt(wft_ref[...], x) + bft_ref[...])
    return x


def _inproj_prompt_body(x_ref, w_ref, b_ref, wkv_ref, bkv_ref, wft_ref, bft_ref,
                        u_ref, q_ref, kt_ref, vt_ref, ktb_ref, vtb_ref, lft_ref, *, cc, aw, scale):
    x = _inproj_common(x_ref, w_ref, b_ref, wft_ref, bft_ref, u_ref, q_ref, lft_ref, cc=cc, aw=aw, scale=scale)
    kt = _dot_nt(wkv_ref[0:aw, :], x) + bkv_ref[0:aw, :]
    kt_ref[0] = kt
    ktb_ref[0] = kt.astype(BF16)
    vt = _dot_nt(wkv_ref[aw:2 * aw, :], x) + bkv_ref[aw:2 * aw, :]
    vt_ref[0] = vt
    vtb_ref[0] = vt.astype(BF16)


def _inproj_sample_body(x_ref, w_ref, b_ref, wkv_ref, bkv_ref, wft_ref, bft_ref,
                        u_ref, q_ref, k_ref, v_ref, lft_ref, *, cc, aw, scale):
    x = _inproj_common(x_ref, w_ref, b_ref, wft_ref, bft_ref, u_ref, q_ref, lft_ref, cc=cc, aw=aw, scale=scale)
    k_ref[...] = _dot_nt(x, wkv_ref[0:aw, :]) + bkv_ref[:, 0:aw]
    v_ref[...] = _dot_nt(x, wkv_ref[aw:2 * aw, :]) + bkv_ref[:, aw:2 * aw]


def _in_proj(x, w_b, b_row, wkv_t, bkv, wft_b, bft_col, cc, aw, q_scale, batch=None):
    n, d = x.shape
    nw = w_b.shape[1]
    heads = wft_b.shape[0]
    const = lambda *_: (0, 0)
    w_specs = [pl.BlockSpec((d, nw), const), pl.BlockSpec((1, nw), const),
               pl.BlockSpec((2 * aw, d), const), pl.BlockSpec(bkv.shape, const),
               pl.BlockSpec((heads, d), const), pl.BlockSpec((heads, 1), const)]
    common_shapes = (jax.ShapeDtypeStruct((n, cc), F32),
                     jax.ShapeDtypeStruct((n, aw), BF16))
    lft_shape = jax.ShapeDtypeStruct((heads, n), F32)
    if batch is None:
        tm = _pick(n, (512, 384, 256, 128, 64, 32, 16, 8))
        row = lambda i: (i, 0)
        return pl.pallas_call(
            functools.partial(_inproj_sample_body, cc=cc, aw=aw, scale=q_scale),
            grid=(n // tm,),
            in_specs=[pl.BlockSpec((tm, d), row)] + w_specs,
            out_specs=(pl.BlockSpec((tm, cc), row), pl.BlockSpec((tm, aw), row),
                       pl.BlockSpec((tm, aw), row), pl.BlockSpec((tm, aw), row),
                       pl.BlockSpec((heads, tm), lambda i: (0, i))),
            out_shape=common_shapes + (jax.ShapeDtypeStruct((n, aw), F32), jax.ShapeDtypeStruct((n, aw), F32),
                                       lft_shape),
            compiler_params=_params(("parallel",)),
            name="in_proj_sample",
        )(x, w_b, b_row, wkv_t, bkv, wft_b, bft_col)
    seq = n // batch
    tm = _pick(seq, (512, 256, 128))
    nt = seq // tm
    row = lambda b, i: (b * nt + i, 0)
    tr = lambda b, i: (b, 0, i)
    t_f32 = jax.ShapeDtypeStruct((batch, aw, seq), F32)
    t_b16 = jax.ShapeDtypeStruct((batch, aw, seq), BF16)
    return pl.pallas_call(
        functools.partial(_inproj_prompt_body, cc=cc, aw=aw, scale=q_scale),
        grid=(batch, nt),
        in_specs=[pl.BlockSpec((tm, d), row)] + w_specs,
        out_specs=(pl.BlockSpec((tm, cc), row), pl.BlockSpec((tm, aw), row),
                   pl.BlockSpec((1, aw, tm), tr), pl.BlockSpec((1, aw, tm), tr),
                   pl.BlockSpec((1, aw, tm), tr), pl.BlockSpec((1, aw, tm), tr),
                   pl.BlockSpec((heads, tm), lambda b, i: (0, b * nt + i))),
        out_shape=common_shapes + (t_f32, t_f32, t_b16, t_b16, lft_shape),
        compiler_params=_params(("parallel", "parallel")),
        name="in_proj_prompt",
    )(x, w_b, b_row, wkv_t, bkv, wft_b, bft_col)


def _group_norm_silu(y, m, gg, gb):
    mean = _dot3(y, m)
    d = y - mean
    var = _dot3(d * d, m)
    yn = d * lax.rsqrt(var + LN_EPS) * gg + gb
    return yn * _sigmoid(yn)


def _conv_body(u_ref, halo_ref, w_ref, b_ref, gg_ref, gb_ref, m_ref, o_ref, ext_ref, sh_ref, y_ref, *, ts, rc):
    i = pl.program_id(1)
    c = u_ref.shape[1]
    rows = CONV_HALO + ts
    halo = halo_ref[...]
    ext_ref[0:CONV_HALO, :] = jnp.where(i > 0, halo, jnp.zeros_like(halo))
    ext_ref[CONV_HALO:rows, :] = u_ref[...]
    ext_ref[rows:rows + SUBLANES, :] = jnp.zeros((SUBLANES, c), F32)
    for r in range(SUBLANES):
        sh_ref[r] = ext_ref[r:r + rows, :]
    off = CONV_HALO - (DW_WIDTH - 1)
    for r0 in range(0, ts, rc):
        acc = jnp.broadcast_to(b_ref[...], (rc, c))
        for j in range(DW_WIDTH):
            s = r0 + off + j
            r = s % SUBLANES
            acc = acc + w_ref[j:j + 1, :] * sh_ref[r, s - r:s - r + rc, :]
        y_ref[r0:r0 + rc, :] = acc
    o_ref[...] = _group_norm_silu(y_ref[...], m_ref[...], gg_ref[...], gb_ref[...]).astype(BF16)


def _conv_prompt(u, batch, seq, w_dw, b_row, gg_row, gb_row, gmat):
    n, c = u.shape
    ts = _pick(seq, (512, 256, 128, 64, 32))
    rc = min(ts, 64)
    nt = seq // ts
    hb = ts // CONV_HALO
    rows = CONV_HALO + ts
    const = lambda b, i: (0, 0)
    return pl.pallas_call(
        functools.partial(_conv_body, ts=ts, rc=rc),
        grid=(batch, nt),
        in_specs=[
            pl.BlockSpec((ts, c), lambda b, i: (b * nt + i, 0)),
            pl.BlockSpec((CONV_HALO, c), lambda b, i: (jnp.maximum((b * nt + i) * hb - 1, 0), 0)),
            pl.BlockSpec((DW_WIDTH, c), const),
            pl.BlockSpec((1, c), const),
            pl.BlockSpec((1, c), const),
            pl.BlockSpec((1, c), const),
            pl.BlockSpec((c, c), const),
        ],
        out_specs=pl.BlockSpec((ts, c), lambda b, i: (b * nt + i, 0)),
        out_shape=jax.ShapeDtypeStruct((n, c), BF16),
        scratch_shapes=[pltpu.VMEM((rows + SUBLANES, c), F32), pltpu.VMEM((SUBLANES, rows, c), F32),
                        pltpu.VMEM((ts, c), F32)],
        compiler_params=_params(("parallel", "parallel")),
        name="conv_prompt",
    )(u, u, w_dw, b_row, gg_row, gb_row, gmat)


def _conv_sample_body(hist_ref, u_ref, w_ref, b_ref, gg_ref, gb_ref, m_ref, o_ref):
    acc = b_ref[...] + w_ref[DW_WIDTH - 1:DW_WIDTH, :] * u_ref[...]
    for j in range(DW_WIDTH - 1):
        acc = acc + w_ref[j:j + 1, :] * hist_ref[j]
    o_ref[...] = _group_norm_silu(acc, m_ref[...], gg_ref[...], gb_ref[...]).astype(BF16)


def _conv_sample(hist_t, u, w_dw, b_row, gg_row, gb_row, gmat):
    n, c = u.shape
    full2 = lambda i: (0, 0)
    return pl.pallas_call(
        _conv_sample_body,
        grid=(1,),
        in_specs=[
            pl.BlockSpec(hist_t.shape, lambda i: (0, 0, 0)),
            pl.BlockSpec((n, c), full2),
            pl.BlockSpec((DW_WIDTH, c), full2),
            pl.BlockSpec((1, c), full2),
            pl.BlockSpec((1, c), full2),
            pl.BlockSpec((1, c), full2),
            pl.BlockSpec((c, c), full2),
        ],
        out_specs=pl.BlockSpec((n, c), full2),
        out_shape=jax.ShapeDtypeStruct((n, c), BF16),
        compiler_params=_params(("arbitrary",)),
        name="conv_sample",
    )(hist_t, u, w_dw, b_row, gg_row, gb_row, gmat)


def _cumsum_body(lf_ref, tri_ref, o_ref, carry_ref):
    @pl.when(pl.program_id(1) == 0)
    def _():
        carry_ref[...] = jnp.zeros_like(carry_ref)

    cl = lf_ref.shape[1]
    cum = _dot3(lf_ref[...], tri_ref[...]) + carry_ref[:, 0:1]
    o_ref[...] = -LOG2_E * cum
    carry_ref[...] = jnp.broadcast_to(cum[:, cl - 1:cl], carry_ref.shape)


def _neg_cumsum(lft, batch, seq):
    heads, n = lft.shape
    cl = _pick(seq, (512, 256, 128))
    nc = seq // cl
    tri = jnp.triu(jnp.ones((cl, cl), F32)).astype(BF16)
    return pl.pallas_call(
        _cumsum_body,
        grid=(batch, nc),
        in_specs=[pl.BlockSpec((heads, cl), lambda b, c: (0, b * nc + c)),
                  pl.BlockSpec((cl, cl), lambda b, c: (0, 0))],
        out_specs=pl.BlockSpec((heads, cl), lambda b, c: (0, b * nc + c)),
        out_shape=jax.ShapeDtypeStruct((heads, n), F32),
        scratch_shapes=[pltpu.VMEM((heads, LANES), F32)],
        compiler_params=_params(("parallel", "arbitrary")),
        name="forget_cumsum",
    )(lft, tri)


def _att_body(qi_ref, ki_ref, q_ref, k_ref, v_ref, nd_ref, o_ref, m_ref, l_ref, acc_ref, *, hd, pairs):
    step = pl.program_id(2)
    qi = qi_ref[step]
    ki = ki_ref[step]
    tq = q_ref.shape[0]
    tk = k_ref.shape[2]
    is_a = lax.broadcasted_iota(I32, (1, LANES), 1) < hd

    @pl.when(ki == 0)
    def _():
        m_ref[...] = jnp.full(m_ref.shape, NEG_INF, F32)
        l_ref[...] = jnp.zeros_like(l_ref)
        acc_ref[...] = jnp.zeros_like(acc_ref)

    def process(diagonal):
        if diagonal:
            causal = (lax.broadcasted_iota(I32, (tq, tk), 1) <= lax.broadcasted_iota(I32, (tq, tk), 0))
        for g in range(pairs):
            lanes = slice(g * LANES, (g + 1) * LANES)
            q = q_ref[:, lanes]
            kt = k_ref[0, lanes, :]
            vt = v_ref[0, lanes, :]
            nd = nd_ref[g]
            zero = jnp.zeros_like(q)
            pvs = []
            alphas = []
            for h in range(2):
                qh = jnp.where(is_a, q, zero) if h == 0 else jnp.where(is_a, zero, q)
                s = _dot(qh, kt) + nd[h:h + 1, :]
                if diagonal:
                    s = jnp.where(causal, s, NEG_INF)
                m_old = m_ref[2 * g + h]
                m_new = jnp.maximum(m_old, jnp.max(s, axis=1, keepdims=True))
                alpha = jnp.exp2(m_old - m_new)
                p = jnp.exp2(s - jnp.tile(m_new, (1, tk // LANES)))
                l_ref[2 * g + h] = alpha * l_ref[2 * g + h] + jnp.sum(p, axis=1, keepdims=True)
                m_ref[2 * g + h] = m_new
                pvs.append(_dot_nt(p.astype(BF16), vt))
                alphas.append(alpha)
            acc_ref[:, lanes] = (acc_ref[:, lanes] * jnp.where(is_a, alphas[0], alphas[1])
                                 + jnp.where(is_a, pvs[0], pvs[1]))

    @pl.when(ki < qi)
    def _():
        process(False)

    @pl.when(ki == qi)
    def _():
        process(True)
        for g in range(pairs):
            lanes = slice(g * LANES, (g + 1) * LANES)
            o_ref[:, lanes] = (acc_ref[:, lanes]
                               / jnp.where(is_a, l_ref[2 * g], l_ref[2 * g + 1])).astype(BF16)


def _attention_prompt(q_b, kt_b, vt_b, negd, batch, seq):
    n, aw = q_b.shape
    pairs = ATT_PAIRS_PER_STEP
    groups = aw // (pairs * LANES)
    w = pairs * LANES
    t = _pick(seq, (512, 256, 128))
    nq = seq // t
    steps = [(a, b) for a in range(nq) for b in range(a + 1)]
    qi_tab = jnp.asarray([s[0] for s in steps], I32)
    ki_tab = jnp.asarray([s[1] for s in steps], I32)
    grid_spec = pltpu.PrefetchScalarGridSpec(
        num_scalar_prefetch=2,
        grid=(batch, groups, len(steps)),
        in_specs=[
            pl.BlockSpec((t, w), lambda b, g, s, qt, kt: (b * nq + qt[s], g)),
            pl.BlockSpec((1, w, t), lambda b, g, s, qt, kt: (b, g, kt[s])),
            pl.BlockSpec((1, w, t), lambda b, g, s, qt, kt: (b, g, kt[s])),
            pl.BlockSpec((pairs, 2, t), lambda b, g, s, qt, kt: (g, 0, b * nq + kt[s])),
        ],
        out_specs=pl.BlockSpec((t, w), lambda b, g, s, qt, kt: (b * nq + qt[s], g)),
        scratch_shapes=[pltpu.VMEM((2 * pairs, t, LANES), F32), pltpu.VMEM((2 * pairs, t, LANES), F32),
                        pltpu.VMEM((t, w), F32)],
    )
    return pl.pallas_call(
        functools.partial(_att_body, hd=HEAD_DIM, pairs=pairs),
        grid_spec=grid_spec,
        out_shape=jax.ShapeDtypeStruct((n, aw), BF16),
        compiler_params=_params(("parallel", "parallel", "arbitrary")),
        name="attention_prompt",
    )(qi_tab, ki_tab, q_b, kt_b, vt_b, negd)


def _pool_suffix_body(lf_ref, tri_ref, ones_ref, w_ref, t_ref):
    x = lf_ref[...]
    w_ref[...] = _dot3(x, tri_ref[...])
    t_ref[...] = _dot3(x, ones_ref[...])


def _pool_suffix(lf_rows):
    n, page = lf_rows.shape
    tp = _pick(n, (2048, 1024, 512, 256, 128, 64, 32, 16, 8, n))
    tri = jnp.tril(jnp.ones((page, page), F32), -1).astype(BF16)
    ones = jnp.ones((page, page), BF16)
    blk = pl.BlockSpec((tp, page), lambda i: (i, 0))
    mat = pl.BlockSpec((page, page), lambda i: (0, 0))
    return pl.pallas_call(
        _pool_suffix_body,
        grid=(n // tp,),
        in_specs=[blk, mat, mat],
        out_specs=(blk, blk),
        out_shape=(jax.ShapeDtypeStruct((n, page), F32), jax.ShapeDtypeStruct((n, page), F32)),
        compiler_params=_params(("parallel",)),
        name="pool_forget_suffix",
    )(lf_rows, tri, ones)


def _satt_body(pt_ref, *refs, n_pages, heads):
    k_refs = refs[:n_pages]
    v_refs = refs[n_pages:2 * n_pages]
    w_refs = refs[2 * n_pages:3 * n_pages]
    t_refs = refs[3 * n_pages:4 * n_pages]
    q_ref, kn_ref, vn_ref, lfn_ref, o_ref, s_ref = refs[4 * n_pages:]
    hd, page = k_refs[0].shape[2:]
    q = q_ref[0].astype(F32)
    s_new = jnp.sum(q * kn_ref[0], axis=1, keepdims=True)
    q_cols = [jnp.broadcast_to(q[h:h + 1, :], (page, hd)).T for h in range(heads)]
    later = lfn_ref[0]
    m = s_new
    for j in reversed(range(n_pages)):
        for h in range(heads):
            s_ref[j, h:h + 1, :] = jnp.sum(k_refs[j][0, h] * q_cols[h], axis=0, keepdims=True)
        s = s_ref[j] + (w_refs[j][0] + later)
        s_ref[j] = s
        m = jnp.maximum(m, jnp.max(s, axis=1, keepdims=True))
        later = later + t_refs[j][0]
    p_new = jnp.exp(s_new - m)
    l = p_new
    for j in range(n_pages):
        p = jnp.exp(s_ref[j] - m)
        s_ref[j] = p
        l = l + jnp.sum(p, axis=1, keepdims=True)
    vn = vn_ref[0]
    for h in range(heads):
        acc = jnp.zeros((hd, page), F32)
        for j in range(n_pages):
            acc = acc + v_refs[j][0, h] * s_ref[j, h:h + 1, :]
        o = jnp.sum(acc.T, axis=0, keepdims=True) + p_new[h:h + 1, :] * vn[h:h + 1, :]
        o_ref[0, h:h + 1, :] = (o / l[h:h + 1, :]).astype(BF16)


def _attention_sample(q_b, k_new, v_new, lf_new, cache_kt, cache_vt, cache_lft, page_table):
    n, aw = q_b.shape
    n_pages = page_table.shape[1]
    pool, heads, hd, page = cache_kt.shape
    within, totals = _pool_suffix(cache_lft.reshape(pool * heads, page))
    within = within.reshape(pool, heads, page)
    totals = totals.reshape(pool, heads, page)
    lfn_lanes = jnp.broadcast_to(lf_new[:, :, None], (n, heads, page))

    def kv_spec(j):
        return pl.BlockSpec((1, heads, hd, page), lambda i, pt, j=j: (pt[i, j], 0, 0, 0))

    def row_spec(j):
        return pl.BlockSpec((1, heads, page), lambda i, pt, j=j: (pt[i, j], 0, 0))

    per_seq = lambda shape: pl.BlockSpec((1,) + shape, lambda i, pt: (i, 0, 0))
    grid_spec = pltpu.PrefetchScalarGridSpec(
        num_scalar_prefetch=1,
        grid=(n,),
        in_specs=([kv_spec(j) for j in range(n_pages)] + [kv_spec(j) for j in range(n_pages)]
                  + [row_spec(j) for j in range(n_pages)] + [row_spec(j) for j in range(n_pages)]
                  + [per_seq((heads, hd)), per_seq((heads, hd)), per_seq((heads, hd)), per_seq((heads, page))]),
        out_specs=per_seq((heads, hd)),
        scratch_shapes=[pltpu.VMEM((n_pages, heads, page), F32)],
    )
    out = pl.pallas_call(
        functools.partial(_satt_body, n_pages=n_pages, heads=heads),
        grid_spec=grid_spec,
        out_shape=jax.ShapeDtypeStruct((n, heads, hd), BF16),
        compiler_params=_params(("arbitrary",)),
        name="attention_sample",
    )(page_table, *([cache_kt] * n_pages), *([cache_vt] * n_pages), *([within] * n_pages), *([totals] * n_pages),
      q_b.reshape(n, heads, hd), k_new.reshape(n, heads, hd), v_new.reshape(n, heads, hd), lfn_lanes)
    return out.reshape(n, aw)


def _mix_body(x_ref, c_ref, a_ref, p_ref, wo_ref, g1_ref, b1_ref, wr_ref, br_ref,
              wpg_ref, bpg_ref, wpp_ref, h_ref, base_ref, idx_ref, gate_ref, cnt_ref, *, alpha, cc):
    mix = _dot(c_ref[...], wo_ref[:cc, :]) + _dot(a_ref[...], wo_ref[cc:, :])
    h = _layer_norm(alpha * x_ref[...] + mix, g1_ref[...], b1_ref[...])
    h_ref[...] = h
    hb = h.astype(BF16)
    hl = (h - hb.astype(F32)).astype(BF16)
    both = _dot(hb, wr_ref[...])
    logits = both[:, :LANES] + both[:, LANES:] + _dot(hl, wr_ref[:, :LANES]) + br_ref[...]

    tm = logits.shape[0]
    lane = lax.broadcasted_iota(I32, (tm, LANES), 1).astype(F32)
    vals, idxs = [], []
    l = logits
    for _ in range(TOP_K):
        m = jnp.max(l, axis=1, keepdims=True)
        ix = jnp.min(jnp.where(l == m, lane, float(LANES)), axis=1, keepdims=True)
        vals.append(m)
        idxs.append(ix)
        l = jnp.where(lane == ix, -jnp.inf, l)
    es = [jnp.exp(v - vals[0]) for v in vals]
    den = es[0] + es[1] + es[2] + es[3]
    idx_w = jnp.zeros((tm, LANES), F32)
    gate_w = jnp.zeros((tm, LANES), F32)
    sel = jnp.zeros((tm, LANES), F32)
    for r in range(TOP_K):
        idx_w = jnp.where(lane == float(r), idxs[r], idx_w)
        gate_w = jnp.where(lane == float(r), es[r] / den, gate_w)
        sel = sel + (lane == idxs[r]).astype(F32)
    idx_ref[...] = idx_w
    gate_ref[...] = gate_w

    @pl.when(pl.program_id(0) == 0)
    def _():
        cnt_ref[...] = jnp.zeros_like(cnt_ref)

    cnt_ref[...] += jnp.sum(sel, axis=0, keepdims=True)

    ple = _sigmoid(_dot(hb, wpg_ref[...]) + bpg_ref[...]) * _dot(p_ref[...].astype(BF16), wpp_ref[...])
    base_ref[...] = alpha * h + ple


def _mix(x, conv_b, att_b, p, wo_b, g1, b1, wr_pair, br_row, wpg_b, bpg, wpp_b, alpha):
    n, d = x.shape
    cc = conv_b.shape[1]
    aw = att_b.shape[1]
    pd = p.shape[1]
    tm = _pick(n, (512, 256, 128, 64, 32, 16, 8))
    row = lambda i: (i, 0)
    const = lambda i: (0, 0)
    outs = (
        jax.ShapeDtypeStruct((n, d), F32),
        jax.ShapeDtypeStruct((n, d), F32),
        jax.ShapeDtypeStruct((n, LANES), F32),
        jax.ShapeDtypeStruct((n, LANES), F32),
        jax.ShapeDtypeStruct((1, LANES), F32),
    )
    return pl.pallas_call(
        functools.partial(_mix_body, alpha=alpha, cc=cc),
        grid=(n // tm,),
        in_specs=[
            pl.BlockSpec((tm, d), row), pl.BlockSpec((tm, cc), row), pl.BlockSpec((tm, aw), row),
            pl.BlockSpec((tm, pd), row),
            pl.BlockSpec((d, d), const), pl.BlockSpec((1, d), const), pl.BlockSpec((1, d), const),
            pl.BlockSpec((d, 2 * LANES), const), pl.BlockSpec((1, LANES), const),
            pl.BlockSpec((d, d), const), pl.BlockSpec((1, d), const), pl.BlockSpec((pd, d), const),
        ],
        out_specs=(
            pl.BlockSpec((tm, d), row), pl.BlockSpec((tm, d), row),
            pl.BlockSpec((tm, LANES), row), pl.BlockSpec((tm, LANES), row),
            pl.BlockSpec((1, LANES), const),
        ),
        out_shape=outs,
        compiler_params=_params(("arbitrary",)),
        name="out_proj_router",
    )(x, conv_b, att_b, p, wo_b, g1, b1, wr_pair, br_row, wpg_b, bpg, wpp_b)


def _positions_body(idx_ref, start_ref, tri_ref, o_ref, carry_ref):
    @pl.when(pl.program_id(0) == 0)
    def _():
        carry_ref[...] = jnp.zeros_like(carry_ref)

    ids = idx_ref[...]
    tm = ids.shape[0]
    lane = lax.broadcasted_iota(I32, (tm, LANES), 1).astype(F32)
    hot = [lane == ids[:, r:r + 1] for r in range(TOP_K)]
    sel = jnp.zeros((tm, LANES), F32)
    for r in range(TOP_K):
        sel = sel + hot[r].astype(F32)
    rank = _dot(tri_ref[...], sel.astype(BF16)) + carry_ref[...]
    pos = rank + start_ref[...]
    out = jnp.zeros((tm, LANES), F32)
    for r in range(TOP_K):
        d = jnp.sum(jnp.where(hot[r], pos, 0.0), axis=1, keepdims=True)
        out = jnp.where(lane == float(r), d, out)
    o_ref[...] = out.astype(I32)
    carry_ref[...] += jnp.sum(sel, axis=0, keepdims=True)


def _positions(idx_w, start_row):
    n = idx_w.shape[0]
    tm = _pick(n, (512, 384, 256, 128, 64, 32, 16, 8))
    tri = jnp.tril(jnp.ones((tm, tm), F32), -1).astype(BF16)
    return pl.pallas_call(
        _positions_body,
        grid=(n // tm,),
        in_specs=[pl.BlockSpec((tm, LANES), lambda i: (i, 0)),
                  pl.BlockSpec((1, LANES), lambda i: (0, 0)),
                  pl.BlockSpec((tm, tm), lambda i: (0, 0))],
        out_specs=pl.BlockSpec((tm, LANES), lambda i: (i, 0)),
        out_shape=jax.ShapeDtypeStruct((n, LANES), I32),
        scratch_shapes=[pltpu.VMEM((1, LANES), F32)],
        compiler_params=_params(("arbitrary",)),
        name="route_positions",
    )(idx_w, start_row, tri)


def _rows_body(dest_ref, src_ref, inv_ref, sem, *, tb):
    i = pl.program_id(0)

    @pl.when(i == 0)
    def _():
        cp = pltpu.make_async_copy(src_ref, inv_ref, sem)
        cp.start()
        cp.wait()

    code0 = i * (tb * TOP_K)

    def body(t, c):
        for r in range(TOP_K):
            j = t * TOP_K + r
            inv_ref[dest_ref[0, 0, j]] = code0 + j
        return c

    lax.fori_loop(0, tb, body, 0, unroll=2)


def _sorted_rows(dest, init):
    n = dest.shape[0]
    tb = _pick(n, (4128, 4096, 2064, 2048, 1032, 1024, 516, 512, 258, 256, 129, 128, 64, 43, 32, 16, 8, 4, 3, 2, 1))
    return pl.pallas_call(
        functools.partial(_rows_body, tb=tb),
        grid=(n // tb,),
        in_specs=[pl.BlockSpec((1, 1, tb * TOP_K), lambda i: (i, 0, 0), memory_space=pltpu.SMEM),
                  pl.BlockSpec(memory_space=pl.ANY)],
        out_specs=pl.BlockSpec(memory_space=pltpu.SMEM),
        out_shape=jax.ShapeDtypeStruct(init.shape, init.dtype),
        scratch_shapes=[pltpu.SemaphoreType.DMA],
        compiler_params=_params(("arbitrary",)),
        name="moe_sorted_rows",
    )(dest.reshape(n // tb, 1, tb * TOP_K), init)


def _expert_body(be_ref, nu_ref, inv_ref, h_ref, wg_ref, bg_ref, wu_ref, bu_ref, wd_ref, bd_ref, out_ref,
                 wgb_ref, wub_ref, wdb_ref, xa_ref, xb_ref, ya_ref, yb_ref,
                 sem_xa, sem_xb, sem_ya, sem_yb, sem_z, *, n_tok, slot_stride):
    i = pl.program_id(0)
    half = MOE_BLOCK // 2

    def gather(x_ref, row0, sem):
        for k in range(half):
            tok = lax.shift_right_logical(inv_ref[row0 + k], 2)
            pltpu.make_async_copy(h_ref.at[pl.ds(tok, 1)], x_ref.at[pl.ds(k, 1)], sem).start(priority=k % 2)

    def scatter(y_ref, row0, sem):
        for k in range(half):
            code = inv_ref[row0 + k]
            slot = (code & (TOP_K - 1)) * slot_stride + lax.shift_right_logical(code, 2)
            pltpu.make_async_copy(y_ref.at[pl.ds(k, 1)], out_ref.at[pl.ds(slot, 1)], sem).start(priority=k % 2)

    def drain_x(x_ref, sem):
        pltpu.make_async_copy(h_ref.at[pl.ds(0, half)], x_ref, sem).wait()

    def drain_y(y_ref, sem):
        pltpu.make_async_copy(y_ref, out_ref.at[pl.ds(0, half)], sem).wait()

    def ffn(x_ref):
        x = x_ref[...].astype(BF16)
        g = jnp.minimum(_dot(x, wgb_ref[...]) + bg_ref[0], SWIGLU_LIMIT)
        u = jnp.clip(_dot(x, wub_ref[...]) + bu_ref[0], -SWIGLU_LIMIT, SWIGLU_LIMIT)
        hidden = (u + 1.0) * (g * _sigmoid(SWIGLU_ALPHA * g))
        return _dot(hidden.astype(BF16), wdb_ref[...]) + bd_ref[0]

    @pl.when(i == 0)
    def _():
        ya_ref[...] = jnp.zeros_like(ya_ref)
        yb_ref[...] = jnp.zeros_like(yb_ref)
        for r in range(TOP_K):
            for y_ref, off in ((ya_ref, 0), (yb_ref, half)):
                first = r == 0 and off == 0
                pltpu.make_async_copy(y_ref, out_ref.at[pl.ds(r * slot_stride + n_tok + off, half)],
                                      sem_ya if first else sem_z).start()
        for _ in range(2 * TOP_K - 1):
            pltpu.make_async_copy(yb_ref, out_ref.at[pl.ds(0, half)], sem_z).wait()
        gather(xa_ref, MOE_BLOCK, sem_xa)

    @pl.when(i < nu_ref[0])
    def _():
        @pl.when((i == 0) | (be_ref[i] != be_ref[jnp.maximum(i - 1, 0)]))
        def _():
            wgb_ref[...] = wg_ref[0].astype(BF16)
            wub_ref[...] = wu_ref[0].astype(BF16)
            wdb_ref[...] = wd_ref[0].astype(BF16)

        drain_x(xa_ref, sem_xa)
        drain_y(ya_ref, sem_ya)
        scatter(yb_ref, i * MOE_BLOCK + half, sem_yb)
        gather(xb_ref, (i + 1) * MOE_BLOCK + half, sem_xb)
        ya_ref[...] = ffn(xa_ref)
        drain_y(yb_ref, sem_yb)
        drain_x(xb_ref, sem_xb)
        scatter(ya_ref, (i + 1) * MOE_BLOCK, sem_ya)
        gather(xa_ref, (i + 2) * MOE_BLOCK, sem_xa)
        yb_ref[...] = ffn(xb_ref)

    @pl.when(i == nu_ref[0])
    def _():
        drain_x(xa_ref, sem_xa)
        drain_y(ya_ref, sem_ya)
        scatter(yb_ref, i * MOE_BLOCK + half, sem_yb)
        drain_y(yb_ref, sem_yb)


def _experts(h_all, block_e, n_used, inv_shifted, n_tok, slot_stride, w_gate, b_gate, w_up, b_up, w_down, b_down):
    d = h_all.shape[1]
    n_e, _, de = w_gate.shape
    n_blocks = block_e.shape[0]
    half = MOE_BLOCK // 2
    wsel = lambda i, be, nu, inv: (be[i], 0, 0)
    grid_spec = pltpu.PrefetchScalarGridSpec(
        num_scalar_prefetch=3,
        grid=(n_blocks,),
        in_specs=[
            pl.BlockSpec(memory_space=pl.ANY),
            pl.BlockSpec((1, d, de), wsel), pl.BlockSpec((1, 1, de), wsel),
            pl.BlockSpec((1, d, de), wsel), pl.BlockSpec((1, 1, de), wsel),
            pl.BlockSpec((1, de, d), wsel), pl.BlockSpec((1, 1, d), wsel),
        ],
        out_specs=pl.BlockSpec(memory_space=pl.ANY),
        scratch_shapes=[pltpu.VMEM((d, de), BF16), pltpu.VMEM((d, de), BF16), pltpu.VMEM((de, d), BF16),
                        pltpu.VMEM((half, d), F32), pltpu.VMEM((half, d), F32),
                        pltpu.VMEM((half, d), F32), pltpu.VMEM((half, d), F32)]
        + [pltpu.SemaphoreType.DMA] * 5,
    )
    return pl.pallas_call(
        functools.partial(_expert_body, n_tok=n_tok, slot_stride=slot_stride),
        grid_spec=grid_spec,
        out_shape=jax.ShapeDtypeStruct((TOP_K * slot_stride, d), F32),
        compiler_params=_params(("arbitrary",)),
        name="moe_experts",
    )(block_e, n_used, inv_shifted, h_all, w_gate, b_gate[:, None, :], w_up, b_up[:, None, :],
      w_down, b_down[:, None, :])


def _combine_body(base_ref, gate_ref, *refs):
    y_refs = refs[:TOP_K]
    g2_ref, b2_ref, o_ref = refs[TOP_K:]
    gate = gate_ref[...]
    y = base_ref[...]
    for r in range(TOP_K):
        y = y + gate[:, r:r + 1] * y_refs[r][...]
    o_ref[...] = _layer_norm(y, g2_ref[...], b2_ref[...])


def _combine(base, gate_w, slots, tok0, slot_stride, g2, b2):
    n, d = base.shape
    tb = _pick(math.gcd(n, math.gcd(tok0, slot_stride)), (512, 256, 128, 64, 32, 16, 8))
    row = lambda i: (i, 0)
    const = lambda i: (0, 0)

    def slot_spec(r):
        return pl.BlockSpec((tb, d), lambda i, r=r: ((r * slot_stride + tok0) // tb + i, 0))

    return pl.pallas_call(
        _combine_body,
        grid=(n // tb,),
        in_specs=([pl.BlockSpec((tb, d), row), pl.BlockSpec((tb, LANES), row)]
                  + [slot_spec(r) for r in range(TOP_K)]
                  + [pl.BlockSpec((1, d), const), pl.BlockSpec((1, d), const)]),
        out_specs=pl.BlockSpec((tb, d), row),
        out_shape=jax.ShapeDtypeStruct((n, d), F32),
        compiler_params=_params(("parallel",)),
        name="moe_combine",
    )(base, gate_w, *([slots] * TOP_K), g2, b2)


def _layer(x_p, x_s, ck, cv, clf, hist_s, page_table, p_p, p_s, lw, alpha):
    batch, seq, d = x_p.shape
    n_s = x_s.shape[0]
    n_p = batch * seq
    cc = lw["w_dw"].shape[1]
    aw = ATT_HEADS * HEAD_DIM
    heads = ATT_HEADS
    nz = 2 * cc + 3 * aw
    row = lambda v: v[None, :]

    nq_cols = 2 * cc + aw
    w_in_t = lw["w_in"].T
    w_in_b = lw["w_in"][:, :nq_cols].astype(BF16)
    b_in_row = row(lw["b_in"][:nq_cols])
    wkv_t = w_in_t[nq_cols:nz].astype(BF16)
    bkv = lw["b_in"][nq_cols:nz]
    wft_b = w_in_t[nz:].astype(BF16)
    bft_col = lw["b_in"][nz:, None]
    b_dw, gg, gb = row(lw["b_dw"]), row(lw["gn_gain"]), row(lw["gn_bias"])
    gsz = cc // CONV_GROUPS
    grp = jnp.arange(cc) // gsz
    gmat = ((grp[:, None] == grp[None, :]).astype(F32) / gsz).astype(BF16)
    wo_b = lw["w_out"].astype(BF16)
    wr = jnp.pad(lw["w_router"], ((0, 0), (0, LANES - N_EXPERTS)))
    wr_hi = wr.astype(BF16)
    wr_lo = (wr - wr_hi.astype(F32)).astype(BF16)
    wr_pair = jnp.concatenate([wr_hi, wr_lo], axis=1)
    br_row = row(jnp.pad(lw["b_router"], (0, LANES - N_EXPERTS), constant_values=NEG_INF))
    wpg_b = lw["w_ple_gate"].astype(BF16)
    wpp_b = lw["w_ple_proj"].astype(BF16)

    xp2 = x_p.reshape(n_p, d)
    xs2 = x_s.reshape(n_s, d)

    u_p, q_p, kt_p, vt_p, ktb_p, vtb_p, lft_p = _in_proj(xp2, w_in_b, b_in_row, wkv_t, bkv[:, None], wft_b, bft_col,
                                                         cc, aw, HEAD_DIM ** -0.5 * LOG2_E, batch=batch)
    conv_p = _conv_prompt(u_p, batch, seq, lw["w_dw"], b_dw, gg, gb, gmat)
    negd = _neg_cumsum(lft_p, batch, seq).reshape(heads // 2, 2, n_p)
    att_p = _attention_prompt(q_p, ktb_p, vtb_p, negd, batch, seq)
    h_p, base_p, idx_p, gate_p, cnt_p = _mix(xp2, conv_p, att_p, p_p.reshape(n_p, -1), wo_b,
                                             row(lw["ln1_gain"]), row(lw["ln1_bias"]), wr_pair, br_row,
                                             wpg_b, row(lw["b_ple_gate"]), wpp_b, alpha)

    u_s, q_s, k_s, v_s, lft_s = _in_proj(xs2, w_in_b, b_in_row, wkv_t, bkv[None, :], wft_b, bft_col,
                                         cc, aw, HEAD_DIM ** -0.5)
    conv_s = _conv_sample(hist_s.transpose(1, 0, 2), u_s, lw["w_dw"], b_dw, gg, gb, gmat)
    lf_s = lft_s.T
    att_s = _attention_sample(q_s, k_s, v_s, lf_s, ck.transpose(0, 2, 3, 1), cv.transpose(0, 2, 3, 1),
                              clf.transpose(0, 2, 1), page_table)
    h_s, base_s, idx_s, gate_s, cnt_s = _mix(xs2, conv_s, att_s, p_s.reshape(n_s, -1), wo_b,
                                             row(lw["ln1_gain"]), row(lw["ln1_bias"]), wr_pair, br_row,
                                             wpg_b, row(lw["b_ple_gate"]), wpp_b, alpha)

    n_tok = n_p + n_s
    counts = (cnt_p + cnt_s)[0].astype(I32)
    padded = (counts + MOE_BLOCK - 1) // MOE_BLOCK * MOE_BLOCK
    pad_end = jnp.cumsum(padded)
    pad_start = pad_end - padded
    n_blocks = -(-(n_tok * TOP_K) // MOE_BLOCK) + N_EXPERTS + 1
    n_used = (pad_end[N_EXPERTS - 1] // MOE_BLOCK).astype(I32).reshape(1)
    block_row0 = jnp.arange(n_blocks, dtype=I32) * MOE_BLOCK
    block_e = jnp.minimum(jnp.sum((pad_end[None, :N_EXPERTS] <= block_row0[:, None]).astype(I32), axis=1),
                          N_EXPERTS - 1)
    dest = _positions(jnp.concatenate([idx_p, idx_s], axis=0), pad_start.astype(F32)[None, :])[:, :TOP_K]

    n_rows = n_blocks * MOE_BLOCK
    slot_stride = n_tok + MOE_BLOCK
    pad_code = lambda m: TOP_K * (n_tok + jnp.arange(m, dtype=I32) % MOE_BLOCK)
    inv = _sorted_rows(dest, pad_code(n_rows))
    inv_shifted = jnp.concatenate([pad_code(MOE_BLOCK), inv])
    h_all = jnp.concatenate([h_p, h_s, jnp.zeros((MOE_BLOCK, d), F32)], axis=0)
    slots = _experts(h_all, block_e, n_used, inv_shifted, n_tok, slot_stride, lw["w_gate"], lw["b_gate"],
                     lw["w_up"], lw["b_up"], lw["w_down"], lw["b_down"])
    g2, b2 = row(lw["ln2_gain"]), row(lw["ln2_bias"])
    y_p = _combine(base_p, gate_p, slots, 0, slot_stride, g2, b2).reshape(batch, seq, d)
    y_s = _combine(base_s, gate_s, slots, n_p, slot_stride, g2, b2).reshape(n_s, 1, d)

    lf_p = lft_p.T.reshape(batch, seq, heads)
    hist_p = u_p.reshape(batch, seq, cc)[:, seq - (DW_WIDTH - 1):]
    hist_new_s = jnp.concatenate([hist_s[:, 1:], u_s[:, None, :]], axis=1)
    return (y_p, y_s,
            kt_p.reshape(batch, heads, HEAD_DIM, seq).transpose(0, 3, 1, 2),
            vt_p.reshape(batch, heads, HEAD_DIM, seq).transpose(0, 3, 1, 2), lf_p, hist_p,
            k_s.reshape(n_s, 1, heads, HEAD_DIM), v_s.reshape(n_s, 1, heads, HEAD_DIM),
            lf_s.reshape(n_s, 1, heads), hist_new_s)


def kernel(x_prompt, x_sample, cache_k, cache_v, cache_logf, state_conv, page_table, p_prompt, p_sample,
           w_in, b_in, w_dw, b_dw, gn_gain, gn_bias, w_out, ln1_gain, ln1_bias,
           w_router, b_router, w_gate, b_gate, w_up, b_up, w_down, b_down,
           w_ple_gate, b_ple_gate, w_ple_proj, ln2_gain, ln2_bias):
    depth = w_in.shape[0]
    if depth != 1 or x_sample.shape[1] != 1:
        raise NotImplementedError("one layer and one new token per sample sequence are supported")
    alpha = (2.0 * depth) ** 0.25
    lw = dict(w_in=w_in[0], b_in=b_in[0], w_dw=w_dw[0], b_dw=b_dw[0], gn_gain=gn_gain[0], gn_bias=gn_bias[0],
              w_out=w_out[0], ln1_gain=ln1_gain[0], ln1_bias=ln1_bias[0],
              w_router=w_router[0], b_router=b_router[0], w_gate=w_gate[0], b_gate=b_gate[0],
              w_up=w_up[0], b_up=b_up[0], w_down=w_down[0], b_down=b_down[0],
              w_ple_gate=w_ple_gate[0], b_ple_gate=b_ple_gate[0], w_ple_proj=w_ple_proj[0],
              ln2_gain=ln2_gain[0], ln2_bias=ln2_bias[0])
    outs = _layer(x_prompt, x_sample[:, 0], cache_k[0], cache_v[0], cache_logf[0], state_conv[0], page_table,
                  p_prompt[0], p_sample[0], lw, alpha)
    (y_p, y_s, k_p, v_p, lf_p, c_p, k_s, v_s, lf_s, c_s) = outs
    return (y_p, y_s, k_p[None], v_p[None], lf_p[None], c_p[None], k_s[None], v_s[None], lf_s[None], c_s[None])
```

```python
import functools
import math

import jax
import jax.numpy as jnp
from jax import lax
from jax.experimental import pallas as pl
from jax.experimental.pallas import tpu as pltpu

F32 = jnp.float32
BF16 = jnp.bfloat16
I32 = jnp.int32

ATT_HEADS = 8
HEAD_DIM = 64
CONV_GROUPS = 8
DW_WIDTH = 31
N_EXPERTS = 32
TOP_K = 4
SWIGLU_LIMIT = 7.0
SWIGLU_ALPHA = 1.702
LN_EPS = 1e-5
NEG_INF = -1e30
LOG2_E = 1.4426950408889634

LANES = 128
SUBLANES = 8
CONV_HALO = 32
MOE_BLOCK = 512
ATT_PAIRS_PER_STEP = 4
VMEM_LIMIT = 56 * 1024 * 1024


def _pick(n, cands):
    for c in cands:
        if n % c == 0:
            return c
    raise ValueError(f"no block size in {cands} divides {n}")


def _params(sem, vmem=VMEM_LIMIT):
    return pltpu.CompilerParams(dimension_semantics=sem, vmem_limit_bytes=vmem)


def _sigmoid(x):
    return 1.0 / (1.0 + jnp.exp(-x))


def _log_sigmoid(x):
    return jnp.minimum(x, 0.0) - jnp.log1p(jnp.exp(-jnp.abs(x)))


def _dot(a, b):
    return jnp.dot(a, b, preferred_element_type=F32)


def _dot_nt(a, b):
    return lax.dot_general(a, b, (((1,), (1,)), ((), ())), preferred_element_type=F32)


def _split3(x):
    x1 = x.astype(BF16)
    r1 = x - x1.astype(F32)
    x2 = r1.astype(BF16)
    x3 = (r1 - x2.astype(F32)).astype(BF16)
    return x1, x2, x3


def _dot3(x, m):
    x1, x2, x3 = _split3(x)
    return _dot(x1, m) + _dot(x2, m) + _dot(x3, m)


def _layer_norm(x, g, b):
    mu = jnp.mean(x, axis=-1, keepdims=True)
    d = x - mu
    var = jnp.mean(d * d, axis=-1, keepdims=True)
    return d * lax.rsqrt(var + LN_EPS) * g + b


def _inproj_common(x_ref, w_ref, b_ref, wft_ref, bft_ref, u_ref, q_ref, lft_ref, *, cc, aw, scale):
    x = x_ref[...].astype(BF16)

    def seg(lo, hi):
        return _dot(x, w_ref[:, lo:hi]) + b_ref[:, lo:hi]

    u_ref[...] = seg(0, cc) * _sigmoid(seg(cc, 2 * cc))
    q_ref[...] = (seg(2 * cc, 2 * cc + aw) * scale).astype(BF16)
    lft_ref[...] = _log_sigmoid(_dot_nt(wft_ref[...], x) + bft_ref[...])
    return x


def _inproj_prompt_body(x_ref, w_ref, b_ref, wkv_ref, bkv_ref, wft_ref, bft_ref,
                        u_ref, q_ref, kt_ref, vt_ref, ktb_ref, vtb_ref, lft_ref, *, cc, aw, scale):
    x = _inproj_common(x_ref, w_ref, b_ref, wft_ref, bft_ref, u_ref, q_ref, lft_ref, cc=cc, aw=aw, scale=scale)
    kt = _dot_nt(wkv_ref[0:aw, :], x) + bkv_ref[0:aw, :]
    kt_ref[0] = kt
    ktb_ref[0] = kt.astype(BF16)
    vt = _dot_nt(wkv_ref[aw:2 * aw, :], x) + bkv_ref[aw:2 * aw, :]
    vt_ref[0] = vt
    vtb_ref[0] = vt.astype(BF16)


def _inproj_sample_body(x_ref, w_ref, b_ref, wkv_ref, bkv_ref, wft_ref, bft_ref,
                        u_ref, q_ref, k_ref, v_ref, lft_ref, *, cc, aw, scale):
    x = _inproj_common(x_ref, w_ref, b_ref, wft_ref, bft_ref, u_ref, q_ref, lft_ref, cc=cc, aw=aw, scale=scale)
    k_ref[...] = _dot_nt(x, wkv_ref[0:aw, :]) + bkv_ref[:, 0:aw]
    v_ref[...] = _dot_nt(x, wkv_ref[aw:2 * aw, :]) + bkv_ref[:, aw:2 * aw]


def _in_proj(x, w_b, b_row, wkv_t, bkv, wft_b, bft_col, cc, aw, q_scale, batch=None):
    n, d = x.shape
    nw = w_b.shape[1]
    heads = wft_b.shape[0]
    const = lambda *_: (0, 0)
    w_specs = [pl.BlockSpec((d, nw), const), pl.BlockSpec((1, nw), const),
               pl.BlockSpec((2 * aw, d), const), pl.BlockSpec(bkv.shape, const),
               pl.BlockSpec((heads, d), const), pl.BlockSpec((heads, 1), const)]
    common_shapes = (jax.ShapeDtypeStruct((n, cc), F32),
                     jax.ShapeDtypeStruct((n, aw), BF16))
    lft_shape = jax.ShapeDtypeStruct((heads, n), F32)
    if batch is None:
        tm = _pick(n, (512, 384, 256, 128, 64, 32, 16, 8))
        row = lambda i: (i, 0)
        return pl.pallas_call(
            functools.partial(_inproj_sample_body, cc=cc, aw=aw, scale=q_scale),
            grid=(n // tm,),
            in_specs=[pl.BlockSpec((tm, d), row)] + w_specs,
            out_specs=(pl.BlockSpec((tm, cc), row), pl.BlockSpec((tm, aw), row),
                       pl.BlockSpec((tm, aw), row), pl.BlockSpec((tm, aw), row),
                       pl.BlockSpec((heads, tm), lambda i: (0, i))),
            out_shape=common_shapes + (jax.ShapeDtypeStruct((n, aw), F32), jax.ShapeDtypeStruct((n, aw), F32),
                                       lft_shape),
            compiler_params=_params(("parallel",)),
            name="in_proj_sample",
        )(x, w_b, b_row, wkv_t, bkv, wft_b, bft_col)
    seq = n // batch
    tm = _pick(seq, (512, 256, 128))
    nt = seq // tm
    row = lambda b, i: (b * nt + i, 0)
    tr = lambda b, i: (b, 0, i)
    t_f32 = jax.ShapeDtypeStruct((batch, aw, seq), F32)
    t_b16 = jax.ShapeDtypeStruct((batch, aw, seq), BF16)
    return pl.pallas_call(
        functools.partial(_inproj_prompt_body, cc=cc, aw=aw, scale=q_scale),
        grid=(batch, nt),
        in_specs=[pl.BlockSpec((tm, d), row)] + w_specs,
        out_specs=(pl.BlockSpec((tm, cc), row), pl.BlockSpec((tm, aw), row),
                   pl.BlockSpec((1, aw, tm), tr), pl.BlockSpec((1, aw, tm), tr),
                   pl.BlockSpec((1, aw, tm), tr), pl.BlockSpec((1, aw, tm), tr),
                   pl.BlockSpec((heads, tm), lambda b, i: (0, b * nt + i))),
        out_shape=common_shapes + (t_f32, t_f32, t_b16, t_b16, lft_shape),
        compiler_params=_params(("parallel", "parallel")),
        name="in_proj_prompt",
    )(x, w_b, b_row, wkv_t, bkv, wft_b, bft_col)


def _group_norm_silu(y, m, gg, gb):
    mean = _dot3(y, m)
    d = y - mean
    var = _dot3(d * d, m)
    yn = d * lax.rsqrt(var + LN_EPS) * gg + gb
    return yn * _sigmoid(yn)


def _conv_body(u_ref, halo_ref, w_ref, b_ref, gg_ref, gb_ref, m_ref, o_ref, ext_ref, sh_ref, y_ref, *, ts, rc):
    i = pl.program_id(1)
    c = u_ref.shape[1]
    rows = CONV_HALO + ts
    halo = halo_ref[...]
    ext_ref[0:CONV_HALO, :] = jnp.where(i > 0, halo, jnp.zeros_like(halo))
    ext_ref[CONV_HALO:rows, :] = u_ref[...]
    ext_ref[rows:rows + SUBLANES, :] = jnp.zeros((SUBLANES, c), F32)
    for r in range(SUBLANES):
        sh_ref[r] = ext_ref[r:r + rows, :]
    off = CONV_HALO - (DW_WIDTH - 1)
    for r0 in range(0, ts, rc):
        acc = jnp.broadcast_to(b_ref[...], (rc, c))
        for j in range(DW_WIDTH):
            s = r0 + off + j
            r = s % SUBLANES
            acc = acc + w_ref[j:j + 1, :] * sh_ref[r, s - r:s - r + rc, :]
        y_ref[r0:r0 + rc, :] = acc
    o_ref[...] = _group_norm_silu(y_ref[...], m_ref[...], gg_ref[...], gb_ref[...]).astype(BF16)


def _conv_prompt(u, batch, seq, w_dw, b_row, gg_row, gb_row, gmat):
    n, c = u.shape
    ts = _pick(seq, (512, 256, 128, 64, 32))
    rc = min(ts, 64)
    nt = seq // ts
    hb = ts // CONV_HALO
    rows = CONV_HALO + ts
    const = lambda b, i: (0, 0)
    return pl.pallas_call(
        functools.partial(_conv_body, ts=ts, rc=rc),
        grid=(batch, nt),
        in_specs=[
            pl.BlockSpec((ts, c), lambda b, i: (b * nt + i, 0)),
            pl.BlockSpec((CONV_HALO, c), lambda b, i: (jnp.maximum((b * nt + i) * hb - 1, 0), 0)),
            pl.BlockSpec((DW_WIDTH, c), const),
            pl.BlockSpec((1, c), const),
            pl.BlockSpec((1, c), const),
            pl.BlockSpec((1, c), const),
            pl.BlockSpec((c, c), const),
        ],
        out_specs=pl.BlockSpec((ts, c), lambda b, i: (b * nt + i, 0)),
        out_shape=jax.ShapeDtypeStruct((n, c), BF16),
        scratch_shapes=[pltpu.VMEM((rows + SUBLANES, c), F32), pltpu.VMEM((SUBLANES, rows, c), F32),
                        pltpu.VMEM((ts, c), F32)],
        compiler_params=_params(("parallel", "parallel")),
        name="conv_prompt",
    )(u, u, w_dw, b_row, gg_row, gb_row, gmat)


def _conv_sample_body(hist_ref, u_ref, w_ref, b_ref, gg_ref, gb_ref, m_ref, o_ref):
    acc = b_ref[...] + w_ref[DW_WIDTH - 1:DW_WIDTH, :] * u_ref[...]
    for j in range(DW_WIDTH - 1):
        acc = acc + w_ref[j:j + 1, :] * hist_ref[j]
    o_ref[...] = _group_norm_silu(acc, m_ref[...], gg_ref[...], gb_ref[...]).astype(BF16)


def _conv_sample(hist_t, u, w_dw, b_row, gg_row, gb_row, gmat):
    n, c = u.shape
    full2 = lambda i: (0, 0)
    return pl.pallas_call(
        _conv_sample_body,
        grid=(1,),
        in_specs=[
            pl.BlockSpec(hist_t.shape, lambda i: (0, 0, 0)),
            pl.BlockSpec((n, c), full2),
            pl.BlockSpec((DW_WIDTH, c), full2),
            pl.BlockSpec((1, c), full2),
            pl.BlockSpec((1, c), full2),
            pl.BlockSpec((1, c), full2),
            pl.BlockSpec((c, c), full2),
        ],
        out_specs=pl.BlockSpec((n, c), full2),
        out_shape=jax.ShapeDtypeStruct((n, c), BF16),
        compiler_params=_params(("arbitrary",)),
        name="conv_sample",
    )(hist_t, u, w_dw, b_row, gg_row, gb_row, gmat)


def _cumsum_body(lf_ref, tri_ref, o_ref, carry_ref):
    @pl.when(pl.program_id(1) == 0)
    def _():
        carry_ref[...] = jnp.zeros_like(carry_ref)

    cl = lf_ref.shape[1]
    cum = _dot3(lf_ref[...], tri_ref[...]) + carry_ref[:, 0:1]
    o_ref[...] = -LOG2_E * cum
    carry_ref[...] = jnp.broadcast_to(cum[:, cl - 1:cl], carry_ref.shape)


def _neg_cumsum(lft, batch, seq):
    heads, n = lft.shape
    cl = _pick(seq, (512, 256, 128))
    nc = seq // cl
    tri = jnp.triu(jnp.ones((cl, cl), F32)).astype(BF16)
    return pl.pallas_call(
        _cumsum_body,
        grid=(batch, nc),
        in_specs=[pl.BlockSpec((heads, cl), lambda b, c: (0, b * nc + c)),
                  pl.BlockSpec((cl, cl), lambda b, c: (0, 0))],
        out_specs=pl.BlockSpec((heads, cl), lambda b, c: (0, b * nc + c)),
        out_shape=jax.ShapeDtypeStruct((heads, n), F32),
        scratch_shapes=[pltpu.VMEM((heads, LANES), F32)],
        compiler_params=_params(("parallel", "arbitrary")),
        name="forget_cumsum",
    )(lft, tri)


def _att_body(qi_ref, ki_ref, q_ref, k_ref, v_ref, nd_ref, o_ref, m_ref, l_ref, acc_ref, *, hd, pairs):
    step = pl.program_id(2)
    qi = qi_ref[step]
    ki = ki_ref[step]
    tq = q_ref.shape[0]
    tk = k_ref.shape[2]
    is_a = lax.broadcasted_iota(I32, (1, LANES), 1) < hd

    @pl.when(ki == 0)
    def _():
        m_ref[...] = jnp.full(m_ref.shape, NEG_INF, F32)
        l_ref[...] = jnp.zeros_like(l_ref)
        acc_ref[...] = jnp.zeros_like(acc_ref)

    def process(diagonal):
        if diagonal:
            causal = (lax.broadcasted_iota(I32, (tq, tk), 1) <= lax.broadcasted_iota(I32, (tq, tk), 0))
        for g in range(pairs):
            lanes = slice(g * LANES, (g + 1) * LANES)
            q = q_ref[:, lanes]
            kt = k_ref[0, lanes, :]
            vt = v_ref[0, lanes, :]
            nd = nd_ref[g]
            zero = jnp.zeros_like(q)
            pvs = []
            alphas = []
            for h in range(2):
                qh = jnp.where(is_a, q, zero) if h == 0 else jnp.where(is_a, zero, q)
                s = _dot(qh, kt) + nd[h:h + 1, :]
                if diagonal:
                    s = jnp.where(causal, s, NEG_INF)
                m_old = m_ref[2 * g + h]
                m_new = jnp.maximum(m_old, jnp.max(s, axis=1, keepdims=True))
                alpha = jnp.exp2(m_old - m_new)
                p = jnp.exp2(s - jnp.tile(m_new, (1, tk // LANES)))
                l_ref[2 * g + h] = alpha * l_ref[2 * g + h] + jnp.sum(p, axis=1, keepdims=True)
                m_ref[2 * g + h] = m_new
                pvs.append(_dot_nt(p.astype(BF16), vt))
                alphas.append(alpha)
            acc_ref[:, lanes] = (acc_ref[:, lanes] * jnp.where(is_a, alphas[0], alphas[1])
                                 + jnp.where(is_a, pvs[0], pvs[1]))

    @pl.when(ki < qi)
    def _():
        process(False)

    @pl.when(ki == qi)
    def _():
        process(True)
        for g in range(pairs):
            lanes = slice(g * LANES, (g + 1) * LANES)
            o_ref[:, lanes] = (acc_ref[:, lanes]
                               / jnp.where(is_a, l_ref[2 * g], l_ref[2 * g + 1])).astype(BF16)


def _attention_prompt(q_b, kt_b, vt_b, negd, batch, seq):
    n, aw = q_b.shape
    pairs = ATT_PAIRS_PER_STEP
    groups = aw // (pairs * LANES)
    w = pairs * LANES
    t = _pick(seq, (512, 256, 128))
    nq = seq // t
    steps = [(a, b) for a in range(nq) for b in range(a + 1)]
    qi_tab = jnp.asarray([s[0] for s in steps], I32)
    ki_tab = jnp.asarray([s[1] for s in steps], I32)
    grid_spec = pltpu.PrefetchScalarGridSpec(
        num_scalar_prefetch=2,
        grid=(batch, groups, len(steps)),
        in_specs=[
            pl.BlockSpec((t, w), lambda b, g, s, qt, kt: (b * nq + qt[s], g)),
            pl.BlockSpec((1, w, t), lambda b, g, s, qt, kt: (b, g, kt[s])),
            pl.BlockSpec((1, w, t), lambda b, g, s, qt, kt: (b, g, kt[s])),
            pl.BlockSpec((pairs, 2, t), lambda b, g, s, qt, kt: (g, 0, b * nq + kt[s])),
        ],
        out_specs=pl.BlockSpec((t, w), lambda b, g, s, qt, kt: (b * nq + qt[s], g)),
        scratch_shapes=[pltpu.VMEM((2 * pairs, t, LANES), F32), pltpu.VMEM((2 * pairs, t, LANES), F32),
                        pltpu.VMEM((t, w), F32)],
    )
    return pl.pallas_call(
        functools.partial(_att_body, hd=HEAD_DIM, pairs=pairs),
        grid_spec=grid_spec,
        out_shape=jax.ShapeDtypeStruct((n, aw), BF16),
        compiler_params=_params(("parallel", "parallel", "arbitrary")),
        name="attention_prompt",
    )(qi_tab, ki_tab, q_b, kt_b, vt_b, negd)


def _pool_suffix_body(lf_ref, tri_ref, ones_ref, w_ref, t_ref):
    x = lf_ref[...]
    w_ref[...] = _dot3(x, tri_ref[...])
    t_ref[...] = _dot3(x, ones_ref[...])


def _pool_suffix(lf_rows):
    n, page = lf_rows.shape
    tp = _pick(n, (2048, 1024, 512, 256, 128, 64, 32, 16, 8, n))
    tri = jnp.tril(jnp.ones((page, page), F32), -1).astype(BF16)
    ones = jnp.ones((page, page), BF16)
    blk = pl.BlockSpec((tp, page), lambda i: (i, 0))
    mat = pl.BlockSpec((page, page), lambda i: (0, 0))
    return pl.pallas_call(
        _pool_suffix_body,
        grid=(n // tp,),
        in_specs=[blk, mat, mat],
        out_specs=(blk, blk),
        out_shape=(jax.ShapeDtypeStruct((n, page), F32), jax.ShapeDtypeStruct((n, page), F32)),
        compiler_params=_params(("parallel",)),
        name="pool_forget_suffix",
    )(lf_rows, tri, ones)


def _satt_body(pt_ref, *refs, n_pages, heads):
    k_refs = refs[:n_pages]
    v_refs = refs[n_pages:2 * n_pages]
    w_refs = refs[2 * n_pages:3 * n_pages]
    t_refs = refs[3 * n_pages:4 * n_pages]
    q_ref, kn_ref, vn_ref, lfn_ref, o_ref, s_ref = refs[4 * n_pages:]
    hd, page = k_refs[0].shape[2:]
    q = q_ref[0].astype(F32)
    s_new = jnp.sum(q * kn_ref[0], axis=1, keepdims=True)
    q_cols = [jnp.broadcast_to(q[h:h + 1, :], (page, hd)).T for h in range(heads)]
    later = lfn_ref[0]
    m = s_new
    for j in reversed(range(n_pages)):
        for h in range(heads):
            s_ref[j, h:h + 1, :] = jnp.sum(k_refs[j][0, h] * q_cols[h], axis=0, keepdims=True)
        s = s_ref[j] + (w_refs[j][0] + later)
        s_ref[j] = s
        m = jnp.maximum(m, jnp.max(s, axis=1, keepdims=True))
        later = later + t_refs[j][0]
    p_new = jnp.exp(s_new - m)
    l = p_new
    for j in range(n_pages):
        p = jnp.exp(s_ref[j] - m)
        s_ref[j] = p
        l = l + jnp.sum(p, axis=1, keepdims=True)
    vn = vn_ref[0]
    for h in range(heads):
        acc = jnp.zeros((hd, page), F32)
        for j in range(n_pages):
            acc = acc + v_refs[j][0, h] * s_ref[j, h:h + 1, :]
        o = jnp.sum(acc.T, axis=0, keepdims=True) + p_new[h:h + 1, :] * vn[h:h + 1, :]
        o_ref[0, h:h + 1, :] = (o / l[h:h + 1, :]).astype(BF16)


def _attention_sample(q_b, k_new, v_new, lf_new, cache_kt, cache_vt, cache_lft, page_table):
    n, aw = q_b.shape
    n_pages = page_table.shape[1]
    pool, heads, hd, page = cache_kt.shape
    within, totals = _pool_suffix(cache_lft.reshape(pool * heads, page))
    within = within.reshape(pool, heads, page)
    totals = totals.reshape(pool, heads, page)
    lfn_lanes = jnp.broadcast_to(lf_new[:, :, None], (n, heads, page))

    def kv_spec(j):
        return pl.BlockSpec((1, heads, hd, page), lambda i, pt, j=j: (pt[i, j], 0, 0, 0))

    def row_spec(j):
        return pl.BlockSpec((1, heads, page), lambda i, pt, j=j: (pt[i, j], 0, 0))

    per_seq = lambda shape: pl.BlockSpec((1,) + shape, lambda i, pt: (i, 0, 0))
    grid_spec = pltpu.PrefetchScalarGridSpec(
        num_scalar_prefetch=1,
        grid=(n,),
        in_specs=([kv_spec(j) for j in range(n_pages)] + [kv_spec(j) for j in range(n_pages)]
                  + [row_spec(j) for j in range(n_pages)] + [row_spec(j) for j in range(n_pages)]
                  + [per_seq((heads, hd)), per_seq((heads, hd)), per_seq((heads, hd)), per_seq((heads, page))]),
        out_specs=per_seq((heads, hd)),
        scratch_shapes=[pltpu.VMEM((n_pages, heads, page), F32)],
    )
    out = pl.pallas_call(
        functools.partial(_satt_body, n_pages=n_pages, heads=heads),
        grid_spec=grid_spec,
        out_shape=jax.ShapeDtypeStruct((n, heads, hd), BF16),
        compiler_params=_params(("arbitrary",)),
        name="attention_sample",
    )(page_table, *([cache_kt] * n_pages), *([cache_vt] * n_pages), *([within] * n_pages), *([totals] * n_pages),
      q_b.reshape(n, heads, hd), k_new.reshape(n, heads, hd), v_new.reshape(n, heads, hd), lfn_lanes)
    return out.reshape(n, aw)


def _mix_body(x_ref, c_ref, a_ref, p_ref, wo_ref, g1_ref, b1_ref, wr_ref, br_ref,
              wpg_ref, bpg_ref, wpp_ref, h_ref, base_ref, idx_ref, gate_ref, cnt_ref, *, alpha, cc):
    mix = _dot(c_ref[...], wo_ref[:cc, :]) + _dot(a_ref[...], wo_ref[cc:, :])
    h = _layer_norm(alpha * x_ref[...] + mix, g1_ref[...], b1_ref[...])
    h_ref[...] = h
    hb = h.astype(BF16)
    hl = (h - hb.astype(F32)).astype(BF16)
    both = _dot(hb, wr_ref[...])
    logits = both[:, :LANES] + both[:, LANES:] + _dot(hl, wr_ref[:, :LANES]) + br_ref[...]

    tm = logits.shape[0]
    lane = lax.broadcasted_iota(I32, (tm, LANES), 1).astype(F32)
    vals, idxs = [], []
    l = logits
    for _ in range(TOP_K):
        m = jnp.max(l, axis=1, keepdims=True)
        ix = jnp.min(jnp.where(l == m, lane, float(LANES)), axis=1, keepdims=True)
        vals.append(m)
        idxs.append(ix)
        l = jnp.where(lane == ix, -jnp.inf, l)
    es = [jnp.exp(v - vals[0]) for v in vals]
    den = es[0] + es[1] + es[2] + es[3]
    idx_w = jnp.zeros((tm, LANES), F32)
    gate_w = jnp.zeros((tm, LANES), F32)
    sel = jnp.zeros((tm, LANES), F32)
    for r in range(TOP_K):
        idx_w = jnp.where(lane == float(r), idxs[r], idx_w)
        gate_w = jnp.where(lane == float(r), es[r] / den, gate_w)
        sel = sel + (lane == idxs[r]).astype(F32)
    idx_ref[...] = idx_w
    gate_ref[...] = gate_w

    @pl.when(pl.program_id(0) == 0)
    def _():
        cnt_ref[...] = jnp.zeros_like(cnt_ref)

    cnt_ref[...] += jnp.sum(sel, axis=0, keepdims=True)

    ple = _sigmoid(_dot(hb, wpg_ref[...]) + bpg_ref[...]) * _dot(p_ref[...].astype(BF16), wpp_ref[...])
    base_ref[...] = alpha * h + ple


def _mix(x, conv_b, att_b, p, wo_b, g1, b1, wr_pair, br_row, wpg_b, bpg, wpp_b, alpha):
    n, d = x.shape
    cc = conv_b.shape[1]
    aw = att_b.shape[1]
    pd = p.shape[1]
    tm = _pick(n, (512, 256, 128, 64, 32, 16, 8))
    row = lambda i: (i, 0)
    const = lambda i: (0, 0)
    outs = (
        jax.ShapeDtypeStruct((n, d), F32),
        jax.ShapeDtypeStruct((n, d), F32),
        jax.ShapeDtypeStruct((n, LANES), F32),
        jax.ShapeDtypeStruct((n, LANES), F32),
        jax.ShapeDtypeStruct((1, LANES), F32),
    )
    return pl.pallas_call(
        functools.partial(_mix_body, alpha=alpha, cc=cc),
        grid=(n // tm,),
        in_specs=[
            pl.BlockSpec((tm, d), row), pl.BlockSpec((tm, cc), row), pl.BlockSpec((tm, aw), row),
            pl.BlockSpec((tm, pd), row),
            pl.BlockSpec((d, d), const), pl.BlockSpec((1, d), const), pl.BlockSpec((1, d), const),
            pl.BlockSpec((d, 2 * LANES), const), pl.BlockSpec((1, LANES), const),
            pl.BlockSpec((d, d), const), pl.BlockSpec((1, d), const), pl.BlockSpec((pd, d), const),
        ],
        out_specs=(
            pl.BlockSpec((tm, d), row), pl.BlockSpec((tm, d), row),
            pl.BlockSpec((tm, LANES), row), pl.BlockSpec((tm, LANES), row),
            pl.BlockSpec((1, LANES), const),
        ),
        out_shape=outs,
        compiler_params=_params(("arbitrary",)),
        name="out_proj_router",
    )(x, conv_b, att_b, p, wo_b, g1, b1, wr_pair, br_row, wpg_b, bpg, wpp_b)


def _positions_body(idx_ref, start_ref, tri_ref, o_ref, carry_ref):
    @pl.when(pl.program_id(0) == 0)
    def _():
        carry_ref[...] = jnp.zeros_like(carry_ref)

    ids = idx_ref[...]
    tm = ids.shape[0]
    lane = lax.broadcasted_iota(I32, (tm, LANES), 1).astype(F32)
    hot = [lane == ids[:, r:r + 1] for r in range(TOP_K)]
    sel = jnp.zeros((tm, LANES), F32)
    for r in range(TOP_K):
        sel = sel + hot[r].astype(F32)
    rank = _dot(tri_ref[...], sel.astype(BF16)) + carry_ref[...]
    pos = rank + start_ref[...]
    out = jnp.zeros((tm, LANES), F32)
    for r in range(TOP_K):
        d = jnp.sum(jnp.where(hot[r], pos, 0.0), axis=1, keepdims=True)
        out = jnp.where(lane == float(r), d, out)
    o_ref[...] = out.astype(I32)
    carry_ref[...] += jnp.sum(sel, axis=0, keepdims=True)


def _positions(idx_w, start_row):
    n = idx_w.shape[0]
    tm = _pick(n, (512, 384, 256, 128, 64, 32, 16, 8))
    tri = jnp.tril(jnp.ones((tm, tm), F32), -1).astype(BF16)
    return pl.pallas_call(
        _positions_body,
        grid=(n // tm,),
        in_specs=[pl.BlockSpec((tm, LANES), lambda i: (i, 0)),
                  pl.BlockSpec((1, LANES), lambda i: (0, 0)),
                  pl.BlockSpec((tm, tm), lambda i: (0, 0))],
        out_specs=pl.BlockSpec((tm, LANES), lambda i: (i, 0)),
        out_shape=jax.ShapeDtypeStruct((n, LANES), I32),
        scratch_shapes=[pltpu.VMEM((1, LANES), F32)],
        compiler_params=_params(("arbitrary",)),
        name="route_positions",
    )(idx_w, start_row, tri)


def _dispatch_body(dest_ref, code_ref, h_ref, xs_in_ref, inv_src_ref, xs_ref, inv_ref, sem, inv_sem, *, tb):
    del xs_in_ref

    @pl.when(pl.program_id(0) == 0)
    def _():
        cp = pltpu.make_async_copy(inv_src_ref, inv_ref, inv_sem)
        cp.start()
        cp.wait()

    def issue(t, c):
        for r in range(TOP_K):
            j = t * TOP_K + r
            d = dest_ref[0, 0, j]
            inv_ref[d] = code_ref[0, 0, j]
            pltpu.make_async_copy(h_ref.at[pl.ds(t, 1)], xs_ref.at[pl.ds(d, 1)], sem).start(priority=r % 2)
        return c

    lax.fori_loop(0, tb, issue, 0)
    for r in range(TOP_K):
        pltpu.make_async_copy(h_ref, xs_ref.at[pl.ds(0, tb)], sem).wait()


def _dispatch(h, dest, xs, inv, tok0, n_tok):
    n, d = h.shape
    tb = _pick(n, (512, 256, 128, 64, 32, 16, 8))
    dest3 = dest.reshape(n // tb, 1, tb * TOP_K)
    code3 = (jnp.arange(TOP_K, dtype=I32)[None, :] * n_tok
             + (tok0 + jnp.arange(n, dtype=I32))[:, None]).reshape(n // tb, 1, tb * TOP_K)
    smem_blk = pl.BlockSpec((1, 1, tb * TOP_K), lambda i: (i, 0, 0), memory_space=pltpu.SMEM)
    return pl.pallas_call(
        functools.partial(_dispatch_body, tb=tb),
        grid=(n // tb,),
        in_specs=[
            smem_blk, smem_blk,
            pl.BlockSpec((tb, d), lambda i: (i, 0)),
            pl.BlockSpec(memory_space=pl.ANY),
            pl.BlockSpec(memory_space=pl.ANY),
        ],
        out_specs=(pl.BlockSpec(memory_space=pl.ANY), pl.BlockSpec(memory_space=pltpu.SMEM)),
        out_shape=(jax.ShapeDtypeStruct(xs.shape, xs.dtype), jax.ShapeDtypeStruct(inv.shape, inv.dtype)),
        scratch_shapes=[pltpu.SemaphoreType.DMA, pltpu.SemaphoreType.DMA],
        input_output_aliases={3: 0},
        compiler_params=_params(("arbitrary",)),
        name="moe_dispatch",
    )(dest3, code3, h, xs, inv)


def _expert_body(be_ref, nu_ref, inv_ref, x_ref, wg_ref, bg_ref, wu_ref, bu_ref, wd_ref, bd_ref, out_ref,
                 wgb_ref, wub_ref, wdb_ref, ya_ref, yb_ref, sem_a, sem_b, sem_z, *, n_slots):
    i = pl.program_id(0)
    half = MOE_BLOCK // 2

    def scatter(y_ref, row0, sem):
        for k in range(half):
            slot = inv_ref[row0 + k]
            pltpu.make_async_copy(y_ref.at[pl.ds(k, 1)], out_ref.at[pl.ds(slot, 1)], sem).start(priority=k % 2)

    def drain(y_ref, sem):
        pltpu.make_async_copy(y_ref, out_ref.at[pl.ds(0, half)], sem).wait()

    def ffn(rows):
        x = x_ref[rows, :].astype(BF16)
        g = jnp.minimum(_dot(x, wgb_ref[...]) + bg_ref[0], SWIGLU_LIMIT)
        u = jnp.clip(_dot(x, wub_ref[...]) + bu_ref[0], -SWIGLU_LIMIT, SWIGLU_LIMIT)
        hidden = (u + 1.0) * (g * _sigmoid(SWIGLU_ALPHA * g))
        return _dot(hidden.astype(BF16), wdb_ref[...]) + bd_ref[0]

    @pl.when(i == 0)
    def _():
        ya_ref[...] = jnp.zeros_like(ya_ref)
        yb_ref[...] = jnp.zeros_like(yb_ref)
        pltpu.make_async_copy(ya_ref, out_ref.at[pl.ds(n_slots, half)], sem_a).start()
        cp = pltpu.make_async_copy(yb_ref, out_ref.at[pl.ds(n_slots + half, half)], sem_z)
        cp.start()
        cp.wait()

    @pl.when(i < nu_ref[0])
    def _():
        @pl.when((i == 0) | (be_ref[i] != be_ref[jnp.maximum(i - 1, 0)]))
        def _():
            wgb_ref[...] = wg_ref[0].astype(BF16)
            wub_ref[...] = wu_ref[0].astype(BF16)
            wdb_ref[...] = wd_ref[0].astype(BF16)

        drain(ya_ref, sem_a)
        scatter(yb_ref, i * MOE_BLOCK + half, sem_b)
        ya_ref[...] = ffn(slice(0, half))
        drain(yb_ref, sem_b)
        scatter(ya_ref, (i + 1) * MOE_BLOCK, sem_a)
        yb_ref[...] = ffn(slice(half, MOE_BLOCK))

    @pl.when(i == nu_ref[0])
    def _():
        drain(ya_ref, sem_a)
        scatter(yb_ref, i * MOE_BLOCK + half, sem_b)
        drain(yb_ref, sem_b)


def _experts(xs, block_e, n_used, inv_shifted, n_slots, w_gate, b_gate, w_up, b_up, w_down, b_down):
    rows, d = xs.shape
    n_e, _, de = w_gate.shape
    n_blocks = rows // MOE_BLOCK
    half = MOE_BLOCK // 2
    last = lambda i, be, nu, inv: (jnp.minimum(i, nu[0] - 1), 0)
    wsel = lambda i, be, nu, inv: (be[i], 0, 0)
    grid_spec = pltpu.PrefetchScalarGridSpec(
        num_scalar_prefetch=3,
        grid=(n_blocks,),
        in_specs=[
            pl.BlockSpec((MOE_BLOCK, d), last),
            pl.BlockSpec((1, d, de), wsel), pl.BlockSpec((1, 1, de), wsel),
            pl.BlockSpec((1, d, de), wsel), pl.BlockSpec((1, 1, de), wsel),
            pl.BlockSpec((1, de, d), wsel), pl.BlockSpec((1, 1, d), wsel),
        ],
        out_specs=pl.BlockSpec(memory_space=pl.ANY),
        scratch_shapes=[pltpu.VMEM((d, de), BF16), pltpu.VMEM((d, de), BF16), pltpu.VMEM((de, d), BF16),
                        pltpu.VMEM((half, d), F32), pltpu.VMEM((half, d), F32),
                        pltpu.SemaphoreType.DMA, pltpu.SemaphoreType.DMA, pltpu.SemaphoreType.DMA],
    )
    return pl.pallas_call(
        functools.partial(_expert_body, n_slots=n_slots),
        grid_spec=grid_spec,
        out_shape=jax.ShapeDtypeStruct((n_slots + MOE_BLOCK, d), F32),
        compiler_params=_params(("arbitrary",)),
        name="moe_experts",
    )(block_e, n_used, inv_shifted, xs, w_gate, b_gate[:, None, :], w_up, b_up[:, None, :],
      w_down, b_down[:, None, :])


def _combine_body(base_ref, gate_ref, *refs):
    y_refs = refs[:TOP_K]
    g2_ref, b2_ref, o_ref = refs[TOP_K:]
    gate = gate_ref[...]
    y = base_ref[...]
    for r in range(TOP_K):
        y = y + gate[:, r:r + 1] * y_refs[r][...]
    o_ref[...] = _layer_norm(y, g2_ref[...], b2_ref[...])


def _combine(base, gate_w, slots, tok0, n_tok, g2, b2):
    n, d = base.shape
    tb = _pick(math.gcd(n, math.gcd(tok0, n_tok)), (512, 256, 128, 64, 32, 16, 8))
    row = lambda i: (i, 0)
    const = lambda i: (0, 0)

    def slot_spec(r):
        return pl.BlockSpec((tb, d), lambda i, r=r: ((r * n_tok + tok0) // tb + i, 0))

    return pl.pallas_call(
        _combine_body,
        grid=(n // tb,),
        in_specs=([pl.BlockSpec((tb, d), row), pl.BlockSpec((tb, LANES), row)]
                  + [slot_spec(r) for r in range(TOP_K)]
                  + [pl.BlockSpec((1, d), const), pl.BlockSpec((1, d), const)]),
        out_specs=pl.BlockSpec((tb, d), row),
        out_shape=jax.ShapeDtypeStruct((n, d), F32),
        compiler_params=_params(("parallel",)),
        name="moe_combine",
    )(base, gate_w, *([slots] * TOP_K), g2, b2)


def _layer(x_p, x_s, ck, cv, clf, hist_s, page_table, p_p, p_s, lw, alpha):
    batch, seq, d = x_p.shape
    n_s = x_s.shape[0]
    n_p = batch * seq
    cc = lw["w_dw"].shape[1]
    aw = ATT_HEADS * HEAD_DIM
    heads = ATT_HEADS
    nz = 2 * cc + 3 * aw
    row = lambda v: v[None, :]

    nq_cols = 2 * cc + aw
    w_in_t = lw["w_in"].T
    w_in_b = lw["w_in"][:, :nq_cols].astype(BF16)
    b_in_row = row(lw["b_in"][:nq_cols])
    wkv_t = w_in_t[nq_cols:nz].astype(BF16)
    bkv = lw["b_in"][nq_cols:nz]
    wft_b = w_in_t[nz:].astype(BF16)
    bft_col = lw["b_in"][nz:, None]
    b_dw, gg, gb = row(lw["b_dw"]), row(lw["gn_gain"]), row(lw["gn_bias"])
    gsz = cc // CONV_GROUPS
    grp = jnp.arange(cc) // gsz
    gmat = ((grp[:, None] == grp[None, :]).astype(F32) / gsz).astype(BF16)
    wo_b = lw["w_out"].astype(BF16)
    wr = jnp.pad(lw["w_router"], ((0, 0), (0, LANES - N_EXPERTS)))
    wr_hi = wr.astype(BF16)
    wr_lo = (wr - wr_hi.astype(F32)).astype(BF16)
    wr_pair = jnp.concatenate([wr_hi, wr_lo], axis=1)
    br_row = row(jnp.pad(lw["b_router"], (0, LANES - N_EXPERTS), constant_values=NEG_INF))
    wpg_b = lw["w_ple_gate"].astype(BF16)
    wpp_b = lw["w_ple_proj"].astype(BF16)

    xp2 = x_p.reshape(n_p, d)
    xs2 = x_s.reshape(n_s, d)

    u_p, q_p, kt_p, vt_p, ktb_p, vtb_p, lft_p = _in_proj(xp2, w_in_b, b_in_row, wkv_t, bkv[:, None], wft_b, bft_col,
                                                         cc, aw, HEAD_DIM ** -0.5 * LOG2_E, batch=batch)
    conv_p = _conv_prompt(u_p, batch, seq, lw["w_dw"], b_dw, gg, gb, gmat)
    negd = _neg_cumsum(lft_p, batch, seq).reshape(heads // 2, 2, n_p)
    att_p = _attention_prompt(q_p, ktb_p, vtb_p, negd, batch, seq)
    h_p, base_p, idx_p, gate_p, cnt_p = _mix(xp2, conv_p, att_p, p_p.reshape(n_p, -1), wo_b,
                                             row(lw["ln1_gain"]), row(lw["ln1_bias"]), wr_pair, br_row,
                                             wpg_b, row(lw["b_ple_gate"]), wpp_b, alpha)

    u_s, q_s, k_s, v_s, lft_s = _in_proj(xs2, w_in_b, b_in_row, wkv_t, bkv[None, :], wft_b, bft_col,
                                         cc, aw, HEAD_DIM ** -0.5)
    conv_s = _conv_sample(hist_s.transpose(1, 0, 2), u_s, lw["w_dw"], b_dw, gg, gb, gmat)
    lf_s = lft_s.T
    att_s = _attention_sample(q_s, k_s, v_s, lf_s, ck.transpose(0, 2, 3, 1), cv.transpose(0, 2, 3, 1),
                              clf.transpose(0, 2, 1), page_table)
    h_s, base_s, idx_s, gate_s, cnt_s = _mix(xs2, conv_s, att_s, p_s.reshape(n_s, -1), wo_b,
                                             row(lw["ln1_gain"]), row(lw["ln1_bias"]), wr_pair, br_row,
                                             wpg_b, row(lw["b_ple_gate"]), wpp_b, alpha)

    n_tok = n_p + n_s
    counts = (cnt_p + cnt_s)[0].astype(I32)
    padded = (counts + MOE_BLOCK - 1) // MOE_BLOCK * MOE_BLOCK
    pad_end = jnp.cumsum(padded)
    pad_start = pad_end - padded
    n_blocks = -(-(n_tok * TOP_K) // MOE_BLOCK) + N_EXPERTS + 1
    n_used = (pad_end[N_EXPERTS - 1] // MOE_BLOCK).astype(I32).reshape(1)
    block_row0 = jnp.arange(n_blocks, dtype=I32) * MOE_BLOCK
    block_e = jnp.minimum(jnp.sum((pad_end[None, :N_EXPERTS] <= block_row0[:, None]).astype(I32), axis=1),
                          N_EXPERTS - 1)
    dest = _positions(jnp.concatenate([idx_p, idx_s], axis=0), pad_start.astype(F32)[None, :])[:, :TOP_K]
    dest_p, dest_s = dest[:n_p], dest[n_p:]

    n_rows = n_blocks * MOE_BLOCK
    n_slots = TOP_K * n_tok
    inv = n_slots + jnp.arange(n_rows, dtype=I32) % MOE_BLOCK
    xs_rows = jnp.zeros((n_rows, d), F32)
    xs_rows, inv = _dispatch(h_p, dest_p, xs_rows, inv, 0, n_tok)
    xs_rows, inv = _dispatch(h_s, dest_s, xs_rows, inv, n_p, n_tok)
    inv_shifted = jnp.concatenate([n_slots + jnp.arange(MOE_BLOCK, dtype=I32), inv])
    slots = _experts(xs_rows, block_e, n_used, inv_shifted, n_slots, lw["w_gate"], lw["b_gate"],
                     lw["w_up"], lw["b_up"], lw["w_down"], lw["b_down"])
    g2, b2 = row(lw["ln2_gain"]), row(lw["ln2_bias"])
    y_p = _combine(base_p, gate_p, slots, 0, n_tok, g2, b2).reshape(batch, seq, d)
    y_s = _combine(base_s, gate_s, slots, n_p, n_tok, g2, b2).reshape(n_s, 1, d)

    lf_p = lft_p.T.reshape(batch, seq, heads)
    hist_p = u_p.reshape(batch, seq, cc)[:, seq - (DW_WIDTH - 1):]
    hist_new_s = jnp.concatenate([hist_s[:, 1:], u_s[:, None, :]], axis=1)
    return (y_p, y_s,
            kt_p.reshape(batch, heads, HEAD_DIM, seq).transpose(0, 3, 1, 2),
            vt_p.reshape(batch, heads, HEAD_DIM, seq).transpose(0, 3, 1, 2), lf_p, hist_p,
            k_s.reshape(n_s, 1, heads, HEAD_DIM), v_s.reshape(n_s, 1, heads, HEAD_DIM),
            lf_s.reshape(n_s, 1, heads), hist_new_s)


def kernel(x_prompt, x_sample, cache_k, cache_v, cache_logf, state_conv, page_table, p_prompt, p_sample,
           w_in, b_in, w_dw, b_dw, gn_gain, gn_bias, w_out, ln1_gain, ln1_bias,
           w_router, b_router, w_gate, b_gate, w_up, b_up, w_down, b_down,
           w_ple_gate, b_ple_gate, w_ple_proj, ln2_gain, ln2_bias):
    depth = w_in.shape[0]
    if depth != 1 or x_sample.shape[1] != 1:
        raise NotImplementedError("one layer and one new token per sample sequence are supported")
    alpha = (2.0 * depth) ** 0.25
    lw = dict(w_in=w_in[0], b_in=b_in[0], w_dw=w_dw[0], b_dw=b_dw[0], gn_gain=gn_gain[0], gn_bias=gn_bias[0],
              w_out=w_out[0], ln1_gain=ln1_gain[0], ln1_bias=ln1_bias[0],
              w_router=w_router[0], b_router=b_router[0], w_gate=w_gate[0], b_gate=b_gate[0],
              w_up=w_up[0], b_up=b_up[0], w_down=w_down[0], b_down=b_down[0],
              w_ple_gate=w_ple_gate[0], b_ple_gate=b_ple_gate[0], w_ple_proj=w_ple_proj[0],
              ln2_gain=ln2_gain[0], ln2_bias=ln2_bias[0])
    outs = _layer(x_prompt, x_sample[:, 0], cache_k[0], cache_v[0], cache_logf[0], state_conv[0], page_table,
                  p_prompt[0], p_sample[0], lw, alpha)
    (y_p, y_s, k_p, v_p, lf_p, c_p, k_s, v_s, lf_s, c_s) = outs
    return (y_p, y_s, k_p[None], v_p[None], lf_p[None], c_p[None], k_s[None], v_s[None], lf_s[None], c_s[None])
```

```python
import functools
import math

import jax
import jax.numpy as jnp
from jax import lax
from jax.experimental import pallas as pl
from jax.experimental.pallas import tpu as pltpu

F32 = jnp.float32
BF16 = jnp.bfloat16
I32 = jnp.int32

ATT_HEADS = 8
HEAD_DIM = 64
CONV_GROUPS = 8
DW_WIDTH = 31
N_EXPERTS = 32
TOP_K = 4
SWIGLU_LIMIT = 7.0
SWIGLU_ALPHA = 1.702
LN_EPS = 1e-5
NEG_INF = -1e30
LOG2_E = 1.4426950408889634

LANES = 128
SUBLANES = 8
CONV_HALO = 32
MOE_BLOCK = 512
ATT_PAIRS_PER_STEP = 4
VMEM_LIMIT = 56 * 1024 * 1024


def _pick(n, cands):
    for c in cands:
        if n % c == 0:
            return c
    raise ValueError(f"no block size in {cands} divides {n}")


def _params(sem, vmem=VMEM_LIMIT):
    return pltpu.CompilerParams(dimension_semantics=sem, vmem_limit_bytes=vmem)


def _sigmoid(x):
    return 1.0 / (1.0 + jnp.exp(-x))


def _log_sigmoid(x):
    return jnp.minimum(x, 0.0) - jnp.log1p(jnp.exp(-jnp.abs(x)))


def _dot(a, b):
    return jnp.dot(a, b, preferred_element_type=F32)


def _dot_nt(a, b):
    return lax.dot_general(a, b, (((1,), (1,)), ((), ())), preferred_element_type=F32)


def _split3(x):
    x1 = x.astype(BF16)
    r1 = x - x1.astype(F32)
    x2 = r1.astype(BF16)
    x3 = (r1 - x2.astype(F32)).astype(BF16)
    return x1, x2, x3


def _dot3(x, m):
    x1, x2, x3 = _split3(x)
    return _dot(x1, m) + _dot(x2, m) + _dot(x3, m)


def _layer_norm(x, g, b):
    mu = jnp.mean(x, axis=-1, keepdims=True)
    d = x - mu
    var = jnp.mean(d * d, axis=-1, keepdims=True)
    return d * lax.rsqrt(var + LN_EPS) * g + b


def _inproj_common(x_ref, w_ref, b_ref, wft_ref, bft_ref, u_ref, q_ref, lft_ref, *, cc, aw, scale):
    x = x_ref[...].astype(BF16)

    def seg(lo, hi):
        return _dot(x, w_ref[:, lo:hi]) + b_ref[:, lo:hi]

    u_ref[...] = seg(0, cc) * _sigmoid(seg(cc, 2 * cc))
    q_ref[...] = (seg(2 * cc, 2 * cc + aw) * scale).astype(BF16)
    lft_ref[...] = _log_sigmoid(_dot_nt(wft_ref[...], x) + bft_ref[...])
    return x


def _inproj_prompt_body(x_ref, w_ref, b_ref, wkv_ref, bkv_ref, wft_ref, bft_ref,
                        u_ref, q_ref, kt_ref, vt_ref, ktb_ref, vtb_ref, lft_ref, *, cc, aw, scale):
    x = _inproj_common(x_ref, w_ref, b_ref, wft_ref, bft_ref, u_ref, q_ref, lft_ref, cc=cc, aw=aw, scale=scale)
    kt = _dot_nt(wkv_ref[0:aw, :], x) + bkv_ref[0:aw, :]
    kt_ref[0] = kt
    ktb_ref[0] = kt.astype(BF16)
    vt = _dot_nt(wkv_ref[aw:2 * aw, :], x) + bkv_ref[aw:2 * aw, :]
    vt_ref[0] = vt
    vtb_ref[0] = vt.astype(BF16)


def _inproj_sample_body(x_ref, w_ref, b_ref, wkv_ref, bkv_ref, wft_ref, bft_ref,
                        u_ref, q_ref, k_ref, v_ref, lft_ref, *, cc, aw, scale):
    x = _inproj_common(x_ref, w_ref, b_ref, wft_ref, bft_ref, u_ref, q_ref, lft_ref, cc=cc, aw=aw, scale=scale)
    k_ref[...] = _dot_nt(x, wkv_ref[0:aw, :]) + bkv_ref[:, 0:aw]
    v_ref[...] = _dot_nt(x, wkv_ref[aw:2 * aw, :]) + bkv_ref[:, aw:2 * aw]


def _in_proj(x, w_b, b_row, wkv_t, bkv, wft_b, bft_col, cc, aw, q_scale, batch=None):
    n, d = x.shape
    nw = w_b.shape[1]
    heads = wft_b.shape[0]
    const = lambda *_: (0, 0)
    w_specs = [pl.BlockSpec((d, nw), const), pl.BlockSpec((1, nw), const),
               pl.BlockSpec((2 * aw, d), const), pl.BlockSpec(bkv.shape, const),
               pl.BlockSpec((heads, d), const), pl.BlockSpec((heads, 1), const)]
    common_shapes = (jax.ShapeDtypeStruct((n, cc), F32),
                     jax.ShapeDtypeStruct((n, aw), BF16))
    lft_shape = jax.ShapeDtypeStruct((heads, n), F32)
    if batch is None:
        tm = _pick(n, (512, 384, 256, 128, 64, 32, 16, 8))
        row = lambda i: (i, 0)
        return pl.pallas_call(
            functools.partial(_inproj_sample_body, cc=cc, aw=aw, scale=q_scale),
            grid=(n // tm,),
            in_specs=[pl.BlockSpec((tm, d), row)] + w_specs,
            out_specs=(pl.BlockSpec((tm, cc), row), pl.BlockSpec((tm, aw), row),
                       pl.BlockSpec((tm, aw), row), pl.BlockSpec((tm, aw), row),
                       pl.BlockSpec((heads, tm), lambda i: (0, i))),
            out_shape=common_shapes + (jax.ShapeDtypeStruct((n, aw), F32), jax.ShapeDtypeStruct((n, aw), F32),
                                       lft_shape),
            compiler_params=_params(("parallel",)),
            name="in_proj_sample",
        )(x, w_b, b_row, wkv_t, bkv, wft_b, bft_col)
    seq = n // batch
    tm = _pick(seq, (512, 256, 128))
    nt = seq // tm
    row = lambda b, i: (b * nt + i, 0)
    tr = lambda b, i: (b, 0, i)
    t_f32 = jax.ShapeDtypeStruct((batch, aw, seq), F32)
    t_b16 = jax.ShapeDtypeStruct((batch, aw, seq), BF16)
    return pl.pallas_call(
        functools.partial(_inproj_prompt_body, cc=cc, aw=aw, scale=q_scale),
        grid=(batch, nt),
        in_specs=[pl.BlockSpec((tm, d), row)] + w_specs,
        out_specs=(pl.BlockSpec((tm, cc), row), pl.BlockSpec((tm, aw), row),
                   pl.BlockSpec((1, aw, tm), tr), pl.BlockSpec((1, aw, tm), tr),
                   pl.BlockSpec((1, aw, tm), tr), pl.BlockSpec((1, aw, tm), tr),
                   pl.BlockSpec((heads, tm), lambda b, i: (0, b * nt + i))),
        out_shape=common_shapes + (t_f32, t_f32, t_b16, t_b16, lft_shape),
        compiler_params=_params(("parallel", "parallel")),
        name="in_proj_prompt",
    )(x, w_b, b_row, wkv_t, bkv, wft_b, bft_col)


def _group_norm_silu(y, m, gg, gb):
    mean = _dot3(y, m)
    d = y - mean
    var = _dot3(d * d, m)
    yn = d * lax.rsqrt(var + LN_EPS) * gg + gb
    return yn * _sigmoid(yn)


def _conv_body(u_ref, halo_ref, w_ref, b_ref, gg_ref, gb_ref, m_ref, o_ref, ext_ref, sh_ref, y_ref, *, ts, rc):
    i = pl.program_id(1)
    c = u_ref.shape[1]
    rows = CONV_HALO + ts
    halo = halo_ref[...]
    ext_ref[0:CONV_HALO, :] = jnp.where(i > 0, halo, jnp.zeros_like(halo))
    ext_ref[CONV_HALO:rows, :] = u_ref[...]
    ext_ref[rows:rows + SUBLANES, :] = jnp.zeros((SUBLANES, c), F32)
    for r in range(SUBLANES):
        sh_ref[r] = ext_ref[r:r + rows, :]
    off = CONV_HALO - (DW_WIDTH - 1)
    for r0 in range(0, ts, rc):
        acc = jnp.broadcast_to(b_ref[...], (rc, c))
        for j in range(DW_WIDTH):
            s = r0 + off + j
            r = s % SUBLANES
            acc = acc + w_ref[j:j + 1, :] * sh_ref[r, s - r:s - r + rc, :]
        y_ref[r0:r0 + rc, :] = acc
    o_ref[...] = _group_norm_silu(y_ref[...], m_ref[...], gg_ref[...], gb_ref[...]).astype(BF16)


def _conv_prompt(u, batch, seq, w_dw, b_row, gg_row, gb_row, gmat):
    n, c = u.shape
    ts = _pick(seq, (512, 256, 128, 64, 32))
    rc = min(ts, 64)
    nt = seq // ts
    hb = ts // CONV_HALO
    rows = CONV_HALO + ts
    const = lambda b, i: (0, 0)
    return pl.pallas_call(
        functools.partial(_conv_body, ts=ts, rc=rc),
        grid=(batch, nt),
        in_specs=[
            pl.BlockSpec((ts, c), lambda b, i: (b * nt + i, 0)),
            pl.BlockSpec((CONV_HALO, c), lambda b, i: (jnp.maximum((b * nt + i) * hb - 1, 0), 0)),
            pl.BlockSpec((DW_WIDTH, c), const),
            pl.BlockSpec((1, c), const),
            pl.BlockSpec((1, c), const),
            pl.BlockSpec((1, c), const),
            pl.BlockSpec((c, c), const),
        ],
        out_specs=pl.BlockSpec((ts, c), lambda b, i: (b * nt + i, 0)),
        out_shape=jax.ShapeDtypeStruct((n, c), BF16),
        scratch_shapes=[pltpu.VMEM((rows + SUBLANES, c), F32), pltpu.VMEM((SUBLANES, rows, c), F32),
                        pltpu.VMEM((ts, c), F32)],
        compiler_params=_params(("parallel", "parallel")),
        name="conv_prompt",
    )(u, u, w_dw, b_row, gg_row, gb_row, gmat)


def _conv_sample_body(hist_ref, u_ref, w_ref, b_ref, gg_ref, gb_ref, m_ref, o_ref):
    acc = b_ref[...] + w_ref[DW_WIDTH - 1:DW_WIDTH, :] * u_ref[...]
    for j in range(DW_WIDTH - 1):
        acc = acc + w_ref[j:j + 1, :] * hist_ref[j]
    o_ref[...] = _group_norm_silu(acc, m_ref[...], gg_ref[...], gb_ref[...]).astype(BF16)


def _conv_sample(hist_t, u, w_dw, b_row, gg_row, gb_row, gmat):
    n, c = u.shape
    full2 = lambda i: (0, 0)
    return pl.pallas_call(
        _conv_sample_body,
        grid=(1,),
        in_specs=[
            pl.BlockSpec(hist_t.shape, lambda i: (0, 0, 0)),
            pl.BlockSpec((n, c), full2),
            pl.BlockSpec((DW_WIDTH, c), full2),
            pl.BlockSpec((1, c), full2),
            pl.BlockSpec((1, c), full2),
            pl.BlockSpec((1, c), full2),
            pl.BlockSpec((c, c), full2),
        ],
        out_specs=pl.BlockSpec((n, c), full2),
        out_shape=jax.ShapeDtypeStruct((n, c), BF16),
        compiler_params=_params(("arbitrary",)),
        name="conv_sample",
    )(hist_t, u, w_dw, b_row, gg_row, gb_row, gmat)


def _cumsum_body(lf_ref, tri_ref, o_ref, carry_ref):
    @pl.when(pl.program_id(1) == 0)
    def _():
        carry_ref[...] = jnp.zeros_like(carry_ref)

    cl = lf_ref.shape[1]
    cum = _dot3(lf_ref[...], tri_ref[...]) + carry_ref[:, 0:1]
    o_ref[...] = -LOG2_E * cum
    carry_ref[...] = jnp.broadcast_to(cum[:, cl - 1:cl], carry_ref.shape)


def _neg_cumsum(lft, batch, seq):
    heads, n = lft.shape
    cl = _pick(seq, (512, 256, 128))
    nc = seq // cl
    tri = jnp.triu(jnp.ones((cl, cl), F32)).astype(BF16)
    return pl.pallas_call(
        _cumsum_body,
        grid=(batch, nc),
        in_specs=[pl.BlockSpec((heads, cl), lambda b, c: (0, b * nc + c)),
                  pl.BlockSpec((cl, cl), lambda b, c: (0, 0))],
        out_specs=pl.BlockSpec((heads, cl), lambda b, c: (0, b * nc + c)),
        out_shape=jax.ShapeDtypeStruct((heads, n), F32),
        scratch_shapes=[pltpu.VMEM((heads, LANES), F32)],
        compiler_params=_params(("parallel", "arbitrary")),
        name="forget_cumsum",
    )(lft, tri)


def _att_body(qi_ref, ki_ref, q_ref, k_ref, v_ref, nd_ref, o_ref, m_ref, l_ref, acc_ref, *, hd, pairs):
    step = pl.program_id(2)
    qi = qi_ref[step]
    ki = ki_ref[step]
    tq = q_ref.shape[0]
    tk = k_ref.shape[2]
    is_a = lax.broadcasted_iota(I32, (1, LANES), 1) < hd

    @pl.when(ki == 0)
    def _():
        m_ref[...] = jnp.full(m_ref.shape, NEG_INF, F32)
        l_ref[...] = jnp.zeros_like(l_ref)
        acc_ref[...] = jnp.zeros_like(acc_ref)

    def process(diagonal):
        if diagonal:
            causal = (lax.broadcasted_iota(I32, (tq, tk), 1) <= lax.broadcasted_iota(I32, (tq, tk), 0))
        for g in range(pairs):
            lanes = slice(g * LANES, (g + 1) * LANES)
            q = q_ref[:, lanes]
            kt = k_ref[0, lanes, :]
            vt = v_ref[0, lanes, :]
            nd = nd_ref[g]
            zero = jnp.zeros_like(q)
            pvs = []
            alphas = []
            for h in range(2):
                qh = jnp.where(is_a, q, zero) if h == 0 else jnp.where(is_a, zero, q)
                s = _dot(qh, kt) + nd[h:h + 1, :]
                if diagonal:
                    s = jnp.where(causal, s, NEG_INF)
                m_old = m_ref[2 * g + h]
                m_new = jnp.maximum(m_old, jnp.max(s, axis=1, keepdims=True))
                alpha = jnp.exp2(m_old - m_new)
                p = jnp.exp2(s - jnp.tile(m_new, (1, tk // LANES)))
                l_ref[2 * g + h] = alpha * l_ref[2 * g + h] + jnp.sum(p, axis=1, keepdims=True)
                m_ref[2 * g + h] = m_new
                pvs.append(_dot_nt(p.astype(BF16), vt))
                alphas.append(alpha)
            acc_ref[:, lanes] = (acc_ref[:, lanes] * jnp.where(is_a, alphas[0], alphas[1])
                                 + jnp.where(is_a, pvs[0], pvs[1]))

    @pl.when(ki < qi)
    def _():
        process(False)

    @pl.when(ki == qi)
    def _():
        process(True)
        for g in range(pairs):
            lanes = slice(g * LANES, (g + 1) * LANES)
            o_ref[:, lanes] = (acc_ref[:, lanes]
                               / jnp.where(is_a, l_ref[2 * g], l_ref[2 * g + 1])).astype(BF16)


def _attention_prompt(q_b, kt_b, vt_b, negd, batch, seq):
    n, aw = q_b.shape
    pairs = ATT_PAIRS_PER_STEP
    groups = aw // (pairs * LANES)
    w = pairs * LANES
    t = _pick(seq, (512, 256, 128))
    nq = seq // t
    steps = [(a, b) for a in range(nq) for b in range(a + 1)]
    qi_tab = jnp.asarray([s[0] for s in steps], I32)
    ki_tab = jnp.asarray([s[1] for s in steps], I32)
    grid_spec = pltpu.PrefetchScalarGridSpec(
        num_scalar_prefetch=2,
        grid=(batch, groups, len(steps)),
        in_specs=[
            pl.BlockSpec((t, w), lambda b, g, s, qt, kt: (b * nq + qt[s], g)),
            pl.BlockSpec((1, w, t), lambda b, g, s, qt, kt: (b, g, kt[s])),
            pl.BlockSpec((1, w, t), lambda b, g, s, qt, kt: (b, g, kt[s])),
            pl.BlockSpec((pairs, 2, t), lambda b, g, s, qt, kt: (g, 0, b * nq + kt[s])),
        ],
        out_specs=pl.BlockSpec((t, w), lambda b, g, s, qt, kt: (b * nq + qt[s], g)),
        scratch_shapes=[pltpu.VMEM((2 * pairs, t, LANES), F32), pltpu.VMEM((2 * pairs, t, LANES), F32),
                        pltpu.VMEM((t, w), F32)],
    )
    return pl.pallas_call(
        functools.partial(_att_body, hd=HEAD_DIM, pairs=pairs),
        grid_spec=grid_spec,
        out_shape=jax.ShapeDtypeStruct((n, aw), BF16),
        compiler_params=_params(("parallel", "parallel", "arbitrary")),
        name="attention_prompt",
    )(qi_tab, ki_tab, q_b, kt_b, vt_b, negd)


def _pool_suffix_body(lf_ref, tri_ref, ones_ref, w_ref, t_ref):
    x = lf_ref[...]
    w_ref[...] = _dot3(x, tri_ref[...])
    t_ref[...] = _dot3(x, ones_ref[...])


def _pool_suffix(lf_rows):
    n, page = lf_rows.shape
    tp = _pick(n, (2048, 1024, 512, 256, 128, 64, 32, 16, 8, n))
    tri = jnp.tril(jnp.ones((page, page), F32), -1).astype(BF16)
    ones = jnp.ones((page, page), BF16)
    blk = pl.BlockSpec((tp, page), lambda i: (i, 0))
    mat = pl.BlockSpec((page, page), lambda i: (0, 0))
    return pl.pallas_call(
        _pool_suffix_body,
        grid=(n // tp,),
        in_specs=[blk, mat, mat],
        out_specs=(blk, blk),
        out_shape=(jax.ShapeDtypeStruct((n, page), F32), jax.ShapeDtypeStruct((n, page), F32)),
        compiler_params=_params(("parallel",)),
        name="pool_forget_suffix",
    )(lf_rows, tri, ones)


def _satt_body(pt_ref, *refs, n_pages, heads):
    k_refs = refs[:n_pages]
    v_refs = refs[n_pages:2 * n_pages]
    w_refs = refs[2 * n_pages:3 * n_pages]
    t_refs = refs[3 * n_pages:4 * n_pages]
    q_ref, kn_ref, vn_ref, lfn_ref, o_ref, s_ref = refs[4 * n_pages:]
    hd, page = k_refs[0].shape[2:]
    q = q_ref[0].astype(F32)
    s_new = jnp.sum(q * kn_ref[0], axis=1, keepdims=True)
    q_cols = [jnp.broadcast_to(q[h:h + 1, :], (page, hd)).T for h in range(heads)]
    later = lfn_ref[0]
    m = s_new
    for j in reversed(range(n_pages)):
        for h in range(heads):
            s_ref[j, h:h + 1, :] = jnp.sum(k_refs[j][0, h] * q_cols[h], axis=0, keepdims=True)
        s = s_ref[j] + (w_refs[j][0] + later)
        s_ref[j] = s
        m = jnp.maximum(m, jnp.max(s, axis=1, keepdims=True))
        later = later + t_refs[j][0]
    p_new = jnp.exp(s_new - m)
    l = p_new
    for j in range(n_pages):
        p = jnp.exp(s_ref[j] - m)
        s_ref[j] = p
        l = l + jnp.sum(p, axis=1, keepdims=True)
    vn = vn_ref[0]
    for h in range(heads):
        acc = jnp.zeros((hd, page), F32)
        for j in range(n_pages):
            acc = acc + v_refs[j][0, h] * s_ref[j, h:h + 1, :]
        o = jnp.sum(acc.T, axis=0, keepdims=True) + p_new[h:h + 1, :] * vn[h:h + 1, :]
        o_ref[0, h:h + 1, :] = (o / l[h:h + 1, :]).astype(BF16)


def _attention_sample(q_b, k_new, v_new, lf_new, cache_kt, cache_vt, cache_lft, page_table):
    n, aw = q_b.shape
    n_pages = page_table.shape[1]
    pool, heads, hd, page = cache_kt.shape
    within, totals = _pool_suffix(cache_lft.reshape(pool * heads, page))
    within = within.reshape(pool, heads, page)
    totals = totals.reshape(pool, heads, page)
    lfn_lanes = jnp.broadcast_to(lf_new[:, :, None], (n, heads, page))

    def kv_spec(j):
        return pl.BlockSpec((1, heads, hd, page), lambda i, pt, j=j: (pt[i, j], 0, 0, 0))

    def row_spec(j):
        return pl.BlockSpec((1, heads, page), lambda i, pt, j=j: (pt[i, j], 0, 0))

    per_seq = lambda shape: pl.BlockSpec((1,) + shape, lambda i, pt: (i, 0, 0))
    grid_spec = pltpu.PrefetchScalarGridSpec(
        num_scalar_prefetch=1,
        grid=(n,),
        in_specs=([kv_spec(j) for j in range(n_pages)] + [kv_spec(j) for j in range(n_pages)]
                  + [row_spec(j) for j in range(n_pages)] + [row_spec(j) for j in range(n_pages)]
                  + [per_seq((heads, hd)), per_seq((heads, hd)), per_seq((heads, hd)), per_seq((heads, page))]),
        out_specs=per_seq((heads, hd)),
        scratch_shapes=[pltpu.VMEM((n_pages, heads, page), F32)],
    )
    out = pl.pallas_call(
        functools.partial(_satt_body, n_pages=n_pages, heads=heads),
        grid_spec=grid_spec,
        out_shape=jax.ShapeDtypeStruct((n, heads, hd), BF16),
        compiler_params=_params(("arbitrary",)),
        name="attention_sample",
    )(page_table, *([cache_kt] * n_pages), *([cache_vt] * n_pages), *([within] * n_pages), *([totals] * n_pages),
      q_b.reshape(n, heads, hd), k_new.reshape(n, heads, hd), v_new.reshape(n, heads, hd), lfn_lanes)
    return out.reshape(n, aw)


def _mix_body(x_ref, c_ref, a_ref, p_ref, wo_ref, g1_ref, b1_ref, wr_ref, br_ref,
              wpg_ref, bpg_ref, wpp_ref, h_ref, base_ref, idx_ref, gate_ref, cnt_ref, *, alpha, cc):
    mix = _dot(c_ref[...], wo_ref[:cc, :]) + _dot(a_ref[...], wo_ref[cc:, :])
    h = _layer_norm(alpha * x_ref[...] + mix, g1_ref[...], b1_ref[...])
    h_ref[...] = h
    hb = h.astype(BF16)
    hl = (h - hb.astype(F32)).astype(BF16)
    both = _dot(hb, wr_ref[...])
    logits = both[:, :LANES] + both[:, LANES:] + _dot(hl, wr_ref[:, :LANES]) + br_ref[...]

    tm = logits.shape[0]
    lane = lax.broadcasted_iota(I32, (tm, LANES), 1).astype(F32)
    vals, idxs = [], []
    l = logits
    for _ in range(TOP_K):
        m = jnp.max(l, axis=1, keepdims=True)
        ix = jnp.min(jnp.where(l == m, lane, float(LANES)), axis=1, keepdims=True)
        vals.append(m)
        idxs.append(ix)
        l = jnp.where(lane == ix, -jnp.inf, l)
    es = [jnp.exp(v - vals[0]) for v in vals]
    den = es[0] + es[1] + es[2] + es[3]
    idx_w = jnp.zeros((tm, LANES), F32)
    gate_w = jnp.zeros((tm, LANES), F32)
    sel = jnp.zeros((tm, LANES), F32)
    for r in range(TOP_K):
        idx_w = jnp.where(lane == float(r), idxs[r], idx_w)
        gate_w = jnp.where(lane == float(r), es[r] / den, gate_w)
        sel = sel + (lane == idxs[r]).astype(F32)
    idx_ref[...] = idx_w
    gate_ref[...] = gate_w

    @pl.when(pl.program_id(0) == 0)
    def _():
        cnt_ref[...] = jnp.zeros_like(cnt_ref)

    cnt_ref[...] += jnp.sum(sel, axis=0, keepdims=True)

    ple = _sigmoid(_dot(hb, wpg_ref[...]) + bpg_ref[...]) * _dot(p_ref[...].astype(BF16), wpp_ref[...])
    base_ref[...] = alpha * h + ple


def _mix(x, conv_b, att_b, p, wo_b, g1, b1, wr_pair, br_row, wpg_b, bpg, wpp_b, alpha):
    n, d = x.shape
    cc = conv_b.shape[1]
    aw = att_b.shape[1]
    pd = p.shape[1]
    tm = _pick(n, (512, 256, 128, 64, 32, 16, 8))
    row = lambda i: (i, 0)
    const = lambda i: (0, 0)
    outs = (
        jax.ShapeDtypeStruct((n, d), F32),
        jax.ShapeDtypeStruct((n, d), F32),
        jax.ShapeDtypeStruct((n, LANES), F32),
        jax.ShapeDtypeStruct((n, LANES), F32),
        jax.ShapeDtypeStruct((1, LANES), F32),
    )
    return pl.pallas_call(
        functools.partial(_mix_body, alpha=alpha, cc=cc),
        grid=(n // tm,),
        in_specs=[
            pl.BlockSpec((tm, d), row), pl.BlockSpec((tm, cc), row), pl.BlockSpec((tm, aw), row),
            pl.BlockSpec((tm, pd), row),
            pl.BlockSpec((d, d), const), pl.BlockSpec((1, d), const), pl.BlockSpec((1, d), const),
            pl.BlockSpec((d, 2 * LANES), const), pl.BlockSpec((1, LANES), const),
            pl.BlockSpec((d, d), const), pl.BlockSpec((1, d), const), pl.BlockSpec((pd, d), const),
        ],
        out_specs=(
            pl.BlockSpec((tm, d), row), pl.BlockSpec((tm, d), row),
            pl.BlockSpec((tm, LANES), row), pl.BlockSpec((tm, LANES), row),
            pl.BlockSpec((1, LANES), const),
        ),
        out_shape=outs,
        compiler_params=_params(("arbitrary",)),
        name="out_proj_router",
    )(x, conv_b, att_b, p, wo_b, g1, b1, wr_pair, br_row, wpg_b, bpg, wpp_b)


def _positions_body(idx_ref, start_ref, tri_ref, o_ref, carry_ref):
    @pl.when(pl.program_id(0) == 0)
    def _():
        carry_ref[...] = jnp.zeros_like(carry_ref)

    ids = idx_ref[...]
    tm = ids.shape[0]
    lane = lax.broadcasted_iota(I32, (tm, LANES), 1).astype(F32)
    hot = [lane == ids[:, r:r + 1] for r in range(TOP_K)]
    sel = jnp.zeros((tm, LANES), F32)
    for r in range(TOP_K):
        sel = sel + hot[r].astype(F32)
    rank = _dot(tri_ref[...], sel.astype(BF16)) + carry_ref[...]
    pos = rank + start_ref[...]
    out = jnp.zeros((tm, LANES), F32)
    for r in range(TOP_K):
        d = jnp.sum(jnp.where(hot[r], pos, 0.0), axis=1, keepdims=True)
        out = jnp.where(lane == float(r), d, out)
    o_ref[...] = out.astype(I32)
    carry_ref[...] += jnp.sum(sel, axis=0, keepdims=True)


def _positions(idx_w, start_row):
    n = idx_w.shape[0]
    tm = _pick(n, (512, 384, 256, 128, 64, 32, 16, 8))
    tri = jnp.tril(jnp.ones((tm, tm), F32), -1).astype(BF16)
    return pl.pallas_call(
        _positions_body,
        grid=(n // tm,),
        in_specs=[pl.BlockSpec((tm, LANES), lambda i: (i, 0)),
                  pl.BlockSpec((1, LANES), lambda i: (0, 0)),
                  pl.BlockSpec((tm, tm), lambda i: (0, 0))],
        out_specs=pl.BlockSpec((tm, LANES), lambda i: (i, 0)),
        out_shape=jax.ShapeDtypeStruct((n, LANES), I32),
        scratch_shapes=[pltpu.VMEM((1, LANES), F32)],
        compiler_params=_params(("arbitrary",)),
        name="route_positions",
    )(idx_w, start_row, tri)


def _dispatch_body(lo_ref, hi_ref, nu_ref, dp_ref, cp_ref, ds_ref, cs_ref, hp_ref, hs_ref, inv_src_ref,
                   xs_ref, inv_ref, zrow_ref, zblk_ref, sem, sem_fill, sem_inv, *, tb, ns, steps_p, n_blocks):
    i = pl.program_id(0)

    def scatter_rows(dest_ref, code_ref, h_ref, n):
        def issue(t, c):
            for r in range(TOP_K):
                j = t * TOP_K + r
                d = dest_ref[0, 0, j]
                inv_ref[d] = code_ref[0, 0, j]
                pltpu.make_async_copy(h_ref.at[pl.ds(t, 1)], xs_ref.at[pl.ds(d, 1)], sem).start(priority=r % 2)
            return c

        lax.fori_loop(0, n, issue, 0)
        for r in range(TOP_K):
            pltpu.make_async_copy(h_ref, xs_ref.at[pl.ds(0, n)], sem).wait()

    @pl.when(i == 0)
    def _():
        cp = pltpu.make_async_copy(inv_src_ref, inv_ref, sem_inv)
        cp.start()
        zrow_ref[...] = jnp.zeros_like(zrow_ref)
        zblk_ref[...] = jnp.zeros_like(zblk_ref)
        cp.wait()

        def zero_row(row):
            return pltpu.make_async_copy(zrow_ref.at[pl.ds(0, 1)], xs_ref.at[pl.ds(row, 1)], sem_fill)

        def zero_block(blk):
            return pltpu.make_async_copy(zblk_ref, xs_ref.at[pl.ds(blk * MOE_BLOCK, MOE_BLOCK)], sem_fill)

        def start(f):
            def body(k, c):
                f(k).start()
                return c
            return body

        def wait(f):
            def body(k, c):
                f(k).wait()
                return c
            return body

        for e in range(N_EXPERTS):
            lax.fori_loop(lo_ref[e], hi_ref[e], start(zero_row), 0)
        for e in range(N_EXPERTS):
            lax.fori_loop(lo_ref[e], hi_ref[e], wait(zero_row), 0)
        lax.fori_loop(nu_ref[0], n_blocks, start(zero_block), 0)
        lax.fori_loop(nu_ref[0], n_blocks, wait(zero_block), 0)

    @pl.when(i < steps_p)
    def _():
        scatter_rows(dp_ref, cp_ref, hp_ref, tb)

    @pl.when(i == steps_p)
    def _():
        scatter_rows(ds_ref, cs_ref, hs_ref, ns)


def _dispatch(h_p, h_s, dest, pad_lo, pad_hi, n_used, n_rows, inv):
    n_p, d = h_p.shape
    ns = h_s.shape[0]
    n_tok = n_p + ns
    tb = _pick(n_p, (512, 256, 128, 64, 32, 16, 8))
    steps_p = n_p // tb
    code = jnp.arange(TOP_K, dtype=I32)[None, :] * n_tok + jnp.arange(n_tok, dtype=I32)[:, None]
    blocked = lambda a: a.reshape(steps_p, 1, tb * TOP_K)
    whole = lambda a: a.reshape(1, 1, ns * TOP_K)
    clamp = lambda i, *_: (jnp.minimum(i, steps_p - 1), 0, 0)
    first = lambda i, *_: (0, 0, 0)
    smem_p = pl.BlockSpec((1, 1, tb * TOP_K), clamp, memory_space=pltpu.SMEM)
    smem_s = pl.BlockSpec((1, 1, ns * TOP_K), first, memory_space=pltpu.SMEM)
    grid_spec = pltpu.PrefetchScalarGridSpec(
        num_scalar_prefetch=3,
        grid=(steps_p + 1,),
        in_specs=[
            smem_p, smem_p, smem_s, smem_s,
            pl.BlockSpec((tb, d), lambda i, *_: (jnp.minimum(i, steps_p - 1), 0)),
            pl.BlockSpec((ns, d), lambda i, *_: (0, 0)),
            pl.BlockSpec(memory_space=pl.ANY),
        ],
        out_specs=(pl.BlockSpec(memory_space=pl.ANY), pl.BlockSpec(memory_space=pltpu.SMEM)),
        scratch_shapes=[pltpu.VMEM((SUBLANES, d), F32), pltpu.VMEM((MOE_BLOCK, d), F32),
                        pltpu.SemaphoreType.DMA, pltpu.SemaphoreType.DMA, pltpu.SemaphoreType.DMA],
    )
    return pl.pallas_call(
        functools.partial(_dispatch_body, tb=tb, ns=ns, steps_p=steps_p, n_blocks=n_rows // MOE_BLOCK),
        grid_spec=grid_spec,
        out_shape=(jax.ShapeDtypeStruct((n_rows, d), F32), jax.ShapeDtypeStruct(inv.shape, inv.dtype)),
        compiler_params=_params(("arbitrary",)),
        name="moe_dispatch",
    )(pad_lo, pad_hi, n_used, blocked(dest[:n_p]), blocked(code[:n_p]), whole(dest[n_p:]), whole(code[n_p:]),
      h_p, h_s, inv)


def _expert_body(be_ref, nu_ref, inv_ref, x_ref, wg_ref, bg_ref, wu_ref, bu_ref, wd_ref, bd_ref, out_ref,
                 wgb_ref, wub_ref, wdb_ref, ya_ref, yb_ref, sem_a, sem_b, sem_z, *, n_slots):
    i = pl.program_id(0)
    half = MOE_BLOCK // 2

    def scatter(y_ref, row0, sem):
        for k in range(half):
            slot = inv_ref[row0 + k]
            pltpu.make_async_copy(y_ref.at[pl.ds(k, 1)], out_ref.at[pl.ds(slot, 1)], sem).start(priority=k % 2)

    def drain(y_ref, sem):
        pltpu.make_async_copy(y_ref, out_ref.at[pl.ds(0, half)], sem).wait()

    def ffn(rows):
        x = x_ref[rows, :].astype(BF16)
        g = jnp.minimum(_dot(x, wgb_ref[...]) + bg_ref[0], SWIGLU_LIMIT)
        u = jnp.clip(_dot(x, wub_ref[...]) + bu_ref[0], -SWIGLU_LIMIT, SWIGLU_LIMIT)
        hidden = (u + 1.0) * (g * _sigmoid(SWIGLU_ALPHA * g))
        return _dot(hidden.astype(BF16), wdb_ref[...]) + bd_ref[0]

    @pl.when(i == 0)
    def _():
        ya_ref[...] = jnp.zeros_like(ya_ref)
        yb_ref[...] = jnp.zeros_like(yb_ref)
        pltpu.make_async_copy(ya_ref, out_ref.at[pl.ds(n_slots, half)], sem_a).start()
        cp = pltpu.make_async_copy(yb_ref, out_ref.at[pl.ds(n_slots + half, half)], sem_z)
        cp.start()
        cp.wait()

    @pl.when(i < nu_ref[0])
    def _():
        @pl.when((i == 0) | (be_ref[i] != be_ref[jnp.maximum(i - 1, 0)]))
        def _():
            wgb_ref[...] = wg_ref[0].astype(BF16)
            wub_ref[...] = wu_ref[0].astype(BF16)
            wdb_ref[...] = wd_ref[0].astype(BF16)

        drain(ya_ref, sem_a)
        scatter(yb_ref, i * MOE_BLOCK + half, sem_b)
        ya_ref[...] = ffn(slice(0, half))
        drain(yb_ref, sem_b)
        scatter(ya_ref, (i + 1) * MOE_BLOCK, sem_a)
        yb_ref[...] = ffn(slice(half, MOE_BLOCK))

    @pl.when(i == nu_ref[0])
    def _():
        drain(ya_ref, sem_a)
        scatter(yb_ref, i * MOE_BLOCK + half, sem_b)
        drain(yb_ref, sem_b)


def _experts(xs, block_e, n_used, inv_shifted, n_slots, w_gate, b_gate, w_up, b_up, w_down, b_down):
    rows, d = xs.shape
    n_e, _, de = w_gate.shape
    n_blocks = rows // MOE_BLOCK
    half = MOE_BLOCK // 2
    last = lambda i, be, nu, inv: (jnp.minimum(i, nu[0] - 1), 0)
    wsel = lambda i, be, nu, inv: (be[i], 0, 0)
    grid_spec = pltpu.PrefetchScalarGridSpec(
        num_scalar_prefetch=3,
        grid=(n_blocks,),
        in_specs=[
            pl.BlockSpec((MOE_BLOCK, d), last),
            pl.BlockSpec((1, d, de), wsel), pl.BlockSpec((1, 1, de), wsel),
            pl.BlockSpec((1, d, de), wsel), pl.BlockSpec((1, 1, de), wsel),
            pl.BlockSpec((1, de, d), wsel), pl.BlockSpec((1, 1, d), wsel),
        ],
        out_specs=pl.BlockSpec(memory_space=pl.ANY),
        scratch_shapes=[pltpu.VMEM((d, de), BF16), pltpu.VMEM((d, de), BF16), pltpu.VMEM((de, d), BF16),
                        pltpu.VMEM((half, d), F32), pltpu.VMEM((half, d), F32),
                        pltpu.SemaphoreType.DMA, pltpu.SemaphoreType.DMA, pltpu.SemaphoreType.DMA],
    )
    return pl.pallas_call(
        functools.partial(_expert_body, n_slots=n_slots),
        grid_spec=grid_spec,
        out_shape=jax.ShapeDtypeStruct((n_slots + MOE_BLOCK, d), F32),
        compiler_params=_params(("arbitrary",)),
        name="moe_experts",
    )(block_e, n_used, inv_shifted, xs, w_gate, b_gate[:, None, :], w_up, b_up[:, None, :],
      w_down, b_down[:, None, :])


def _combine_body(base_ref, gate_ref, *refs):
    y_refs = refs[:TOP_K]
    g2_ref, b2_ref, o_ref = refs[TOP_K:]
    gate = gate_ref[...]
    y = base_ref[...]
    for r in range(TOP_K):
        y = y + gate[:, r:r + 1] * y_refs[r][...]
    o_ref[...] = _layer_norm(y, g2_ref[...], b2_ref[...])


def _combine(base, gate_w, slots, tok0, n_tok, g2, b2):
    n, d = base.shape
    tb = _pick(math.gcd(n, math.gcd(tok0, n_tok)), (512, 256, 128, 64, 32, 16, 8))
    row = lambda i: (i, 0)
    const = lambda i: (0, 0)

    def slot_spec(r):
        return pl.BlockSpec((tb, d), lambda i, r=r: ((r * n_tok + tok0) // tb + i, 0))

    return pl.pallas_call(
        _combine_body,
        grid=(n // tb,),
        in_specs=([pl.BlockSpec((tb, d), row), pl.BlockSpec((tb, LANES), row)]
                  + [slot_spec(r) for r in range(TOP_K)]
                  + [pl.BlockSpec((1, d), const), pl.BlockSpec((1, d), const)]),
        out_specs=pl.BlockSpec((tb, d), row),
        out_shape=jax.ShapeDtypeStruct((n, d), F32),
        compiler_params=_params(("parallel",)),
        name="moe_combine",
    )(base, gate_w, *([slots] * TOP_K), g2, b2)


def _layer(x_p, x_s, ck, cv, clf, hist_s, page_table, p_p, p_s, lw, alpha):
    batch, seq, d = x_p.shape
    n_s = x_s.shape[0]
    n_p = batch * seq
    cc = lw["w_dw"].shape[1]
    aw = ATT_HEADS * HEAD_DIM
    heads = ATT_HEADS
    nz = 2 * cc + 3 * aw
    row = lambda v: v[None, :]

    nq_cols = 2 * cc + aw
    w_in_t = lw["w_in"].T
    w_in_b = lw["w_in"][:, :nq_cols].astype(BF16)
    b_in_row = row(lw["b_in"][:nq_cols])
    wkv_t = w_in_t[nq_cols:nz].astype(BF16)
    bkv = lw["b_in"][nq_cols:nz]
    wft_b = w_in_t[nz:].astype(BF16)
    bft_col = lw["b_in"][nz:, None]
    b_dw, gg, gb = row(lw["b_dw"]), row(lw["gn_gain"]), row(lw["gn_bias"])
    gsz = cc // CONV_GROUPS
    grp = jnp.arange(cc) // gsz
    gmat = ((grp[:, None] == grp[None, :]).astype(F32) / gsz).astype(BF16)
    wo_b = lw["w_out"].astype(BF16)
    wr = jnp.pad(lw["w_router"], ((0, 0), (0, LANES - N_EXPERTS)))
    wr_hi = wr.astype(BF16)
    wr_lo = (wr - wr_hi.astype(F32)).astype(BF16)
    wr_pair = jnp.concatenate([wr_hi, wr_lo], axis=1)
    br_row = row(jnp.pad(lw["b_router"], (0, LANES - N_EXPERTS), constant_values=NEG_INF))
    wpg_b = lw["w_ple_gate"].astype(BF16)
    wpp_b = lw["w_ple_proj"].astype(BF16)

    xp2 = x_p.reshape(n_p, d)
    xs2 = x_s.reshape(n_s, d)

    u_p, q_p, kt_p, vt_p, ktb_p, vtb_p, lft_p = _in_proj(xp2, w_in_b, b_in_row, wkv_t, bkv[:, None], wft_b, bft_col,
                                                         cc, aw, HEAD_DIM ** -0.5 * LOG2_E, batch=batch)
    conv_p = _conv_prompt(u_p, batch, seq, lw["w_dw"], b_dw, gg, gb, gmat)
    negd = _neg_cumsum(lft_p, batch, seq).reshape(heads // 2, 2, n_p)
    att_p = _attention_prompt(q_p, ktb_p, vtb_p, negd, batch, seq)
    h_p, base_p, idx_p, gate_p, cnt_p = _mix(xp2, conv_p, att_p, p_p.reshape(n_p, -1), wo_b,
                                             row(lw["ln1_gain"]), row(lw["ln1_bias"]), wr_pair, br_row,
                                             wpg_b, row(lw["b_ple_gate"]), wpp_b, alpha)

    u_s, q_s, k_s, v_s, lft_s = _in_proj(xs2, w_in_b, b_in_row, wkv_t, bkv[None, :], wft_b, bft_col,
                                         cc, aw, HEAD_DIM ** -0.5)
    conv_s = _conv_sample(hist_s.transpose(1, 0, 2), u_s, lw["w_dw"], b_dw, gg, gb, gmat)
    lf_s = lft_s.T
    att_s = _attention_sample(q_s, k_s, v_s, lf_s, ck.transpose(0, 2, 3, 1), cv.transpose(0, 2, 3, 1),
                              clf.transpose(0, 2, 1), page_table)
    h_s, base_s, idx_s, gate_s, cnt_s = _mix(xs2, conv_s, att_s, p_s.reshape(n_s, -1), wo_b,
                                             row(lw["ln1_gain"]), row(lw["ln1_bias"]), wr_pair, br_row,
                                             wpg_b, row(lw["b_ple_gate"]), wpp_b, alpha)

    n_tok = n_p + n_s
    counts = (cnt_p + cnt_s)[0].astype(I32)
    padded = (counts + MOE_BLOCK - 1) // MOE_BLOCK * MOE_BLOCK
    pad_end = jnp.cumsum(padded)
    pad_start = pad_end - padded
    n_blocks = -(-(n_tok * TOP_K) // MOE_BLOCK) + N_EXPERTS + 1
    n_used = (pad_end[N_EXPERTS - 1] // MOE_BLOCK).astype(I32).reshape(1)
    block_row0 = jnp.arange(n_blocks, dtype=I32) * MOE_BLOCK
    block_e = jnp.minimum(jnp.sum((pad_end[None, :N_EXPERTS] <= block_row0[:, None]).astype(I32), axis=1),
                          N_EXPERTS - 1)
    dest = _positions(jnp.concatenate([idx_p, idx_s], axis=0), pad_start.astype(F32)[None, :])[:, :TOP_K]

    n_rows = n_blocks * MOE_BLOCK
    n_slots = TOP_K * n_tok
    inv = n_slots + jnp.arange(n_rows, dtype=I32) % MOE_BLOCK
    xs_rows, inv = _dispatch(h_p, h_s, dest, (pad_start + counts).astype(I32), pad_end.astype(I32), n_used,
                             n_rows, inv)
    inv_shifted = jnp.concatenate([n_slots + jnp.arange(MOE_BLOCK, dtype=I32), inv])
    slots = _experts(xs_rows, block_e, n_used, inv_shifted, n_slots, lw["w_gate"], lw["b_gate"],
                     lw["w_up"], lw["b_up"], lw["w_down"], lw["b_down"])
    g2, b2 = row(lw["ln2_gain"]), row(lw["ln2_bias"])
    y_p = _combine(base_p, gate_p, slots, 0, n_tok, g2, b2).reshape(batch, seq, d)
    y_s = _combine(base_s, gate_s, slots, n_p, n_tok, g2, b2).reshape(n_s, 1, d)

    lf_p = lft_p.T.reshape(batch, seq, heads)
    hist_p = u_p.reshape(batch, seq, cc)[:, seq - (DW_WIDTH - 1):]
    hist_new_s = jnp.concatenate([hist_s[:, 1:], u_s[:, None, :]], axis=1)
    return (y_p, y_s,
            kt_p.reshape(batch, heads, HEAD_DIM, seq).transpose(0, 3, 1, 2),
            vt_p.reshape(batch, heads, HEAD_DIM, seq).transpose(0, 3, 1, 2), lf_p, hist_p,
            k_s.reshape(n_s, 1, heads, HEAD_DIM), v_s.reshape(n_s, 1, heads, HEAD_DIM),
            lf_s.reshape(n_s, 1, heads), hist_new_s)


def kernel(x_prompt, x_sample, cache_k, cache_v, cache_logf, state_conv, page_table, p_prompt, p_sample,
           w_in, b_in, w_dw, b_dw, gn_gain, gn_bias, w_out, ln1_gain, ln1_bias,
           w_router, b_router, w_gate, b_gate, w_up, b_up, w_down, b_down,
           w_ple_gate, b_ple_gate, w_ple_proj, ln2_gain, ln2_bias):
    depth = w_in.shape[0]
    if depth != 1 or x_sample.shape[1] != 1:
        raise NotImplementedError("one layer and one new token per sample sequence are supported")
    alpha = (2.0 * depth) ** 0.25
    lw = dict(w_in=w_in[0], b_in=b_in[0], w_dw=w_dw[0], b_dw=b_dw[0], gn_gain=gn_gain[0], gn_bias=gn_bias[0],
              w_out=w_out[0], ln1_gain=ln1_gain[0], ln1_bias=ln1_bias[0],
              w_router=w_router[0], b_router=b_router[0], w_gate=w_gate[0], b_gate=b_gate[0],
              w_up=w_up[0], b_up=b_up[0], w_down=w_down[0], b_down=b_down[0],
              w_ple_gate=w_ple_gate[0], b_ple_gate=b_ple_gate[0], w_ple_proj=w_ple_proj[0],
              ln2_gain=ln2_gain[0], ln2_bias=ln2_bias[0])
    outs = _layer(x_prompt, x_sample[:, 0], cache_k[0], cache_v[0], cache_logf[0], state_conv[0], page_table,
                  p_prompt[0], p_sample[0], lw, alpha)
    (y_p, y_s, k_p, v_p, lf_p, c_p, k_s, v_s, lf_s, c_s) = outs
    return (y_p, y_s, k_p[None], v_p[None], lf_p[None], c_p[None], k_s[None], v_s[None], lf_s[None], c_s[None])
```

```python
import functools
import math

import jax
import jax.numpy as jnp
from jax import lax
from jax.experimental import pallas as pl
from jax.experimental.pallas import tpu as pltpu

F32 = jnp.float32
BF16 = jnp.bfloat16
I32 = jnp.int32

ATT_HEADS = 8
HEAD_DIM = 64
CONV_GROUPS = 8
DW_WIDTH = 31
N_EXPERTS = 32
TOP_K = 4
SWIGLU_LIMIT = 7.0
SWIGLU_ALPHA = 1.702
LN_EPS = 1e-5
NEG_INF = -1e30
LOG2_E = 1.4426950408889634

LANES = 128
SUBLANES = 8
CONV_HALO = 32
MOE_BLOCK = 512
ATT_PAIRS_PER_STEP = 4
VMEM_LIMIT = 56 * 1024 * 1024


def _pick(n, cands):
    for c in cands:
        if n % c == 0:
            return c
    raise ValueError(f"no block size in {cands} divides {n}")


def _params(sem, vmem=VMEM_LIMIT):
    return pltpu.CompilerParams(dimension_semantics=sem, vmem_limit_bytes=vmem)


def _sigmoid(x):
    return 1.0 / (1.0 + jnp.exp(-x))


def _log_sigmoid(x):
    return jnp.minimum(x, 0.0) - jnp.log1p(jnp.exp(-jnp.abs(x)))


def _dot(a, b):
    return jnp.dot(a, b, preferred_element_type=F32)


def _dot_nt(a, b):
    return lax.dot_general(a, b, (((1,), (1,)), ((), ())), preferred_element_type=F32)


def _split3(x):
    x1 = x.astype(BF16)
    r1 = x - x1.astype(F32)
    x2 = r1.astype(BF16)
    x3 = (r1 - x2.astype(F32)).astype(BF16)
    return x1, x2, x3


def _dot3(x, m):
    x1, x2, x3 = _split3(x)
    return _dot(x1, m) + _dot(x2, m) + _dot(x3, m)


def _layer_norm(x, g, b):
    mu = jnp.mean(x, axis=-1, keepdims=True)
    d = x - mu
    var = jnp.mean(d * d, axis=-1, keepdims=True)
    return d * lax.rsqrt(var + LN_EPS) * g + b


def _inproj_common(x_ref, w_ref, b_ref, wft_ref, bft_ref, u_ref, q_ref, lft_ref, *, cc, aw, scale):
    x = x_ref[...].astype(BF16)

    def seg(lo, hi):
        return _dot(x, w_ref[:, lo:hi]) + b_ref[:, lo:hi]

    u_ref[...] = seg(0, cc) * _sigmoid(seg(cc, 2 * cc))
    q_ref[...] = (seg(2 * cc, 2 * cc + aw) * scale).astype(BF16)
    lft_ref[...] = _log_sigmoid(_dot_nt(wft_ref[...], x) + bft_ref[...])
    return x


def _inproj_prompt_body(x_ref, w_ref, b_ref, wkv_ref, bkv_ref, wft_ref, bft_ref,
                        u_ref, q_ref, kt_ref, vt_ref, ktb_ref, vtb_ref, lft_ref, *, cc, aw, scale):
    x = _inproj_common(x_ref, w_ref, b_ref, wft_ref, bft_ref, u_ref, q_ref, lft_ref, cc=cc, aw=aw, scale=scale)
    kt = _dot_nt(wkv_ref[0:aw, :], x) + bkv_ref[0:aw, :]
    kt_ref[0] = kt
    ktb_ref[0] = kt.astype(BF16)
    vt = _dot_nt(wkv_ref[aw:2 * aw, :], x) + bkv_ref[aw:2 * aw, :]
    vt_ref[0] = vt
    vtb_ref[0] = vt.astype(BF16)


def _inproj_sample_body(x_ref, w_ref, b_ref, wkv_ref, bkv_ref, wft_ref, bft_ref,
                        u_ref, q_ref, k_ref, v_ref, lft_ref, *, cc, aw, scale):
    x = _inproj_common(x_ref, w_ref, b_ref, wft_ref, bft_ref, u_ref, q_ref, lft_ref, cc=cc, aw=aw, scale=scale)
    k_ref[...] = _dot_nt(x, wkv_ref[0:aw, :]) + bkv_ref[:, 0:aw]
    v_ref[...] = _dot_nt(x, wkv_ref[aw:2 * aw, :]) + bkv_ref[:, aw:2 * aw]


def _in_proj(x, w_b, b_row, wkv_t, bkv, wft_b, bft_col, cc, aw, q_scale, batch=None):
    n, d = x.shape
    nw = w_b.shape[1]
    heads = wft_b.shape[0]
    const = lambda *_: (0, 0)
    w_specs = [pl.BlockSpec((d, nw), const), pl.BlockSpec((1, nw), const),
               pl.BlockSpec((2 * aw, d), const), pl.BlockSpec(bkv.shape, const),
               pl.BlockSpec((heads, d), const), pl.BlockSpec((heads, 1), const)]
    common_shapes = (jax.ShapeDtypeStruct((n, cc), F32),
                     jax.ShapeDtypeStruct((n, aw), BF16))
    lft_shape = jax.ShapeDtypeStruct((heads, n), F32)
    if batch is None:
        tm = _pick(n, (512, 384, 256, 128, 64, 32, 16, 8))
        row = lambda i: (i, 0)
        return pl.pallas_call(
            functools.partial(_inproj_sample_body, cc=cc, aw=aw, scale=q_scale),
            grid=(n // tm,),
            in_specs=[pl.BlockSpec((tm, d), row)] + w_specs,
            out_specs=(pl.BlockSpec((tm, cc), row), pl.BlockSpec((tm, aw), row),
                       pl.BlockSpec((tm, aw), row), pl.BlockSpec((tm, aw), row),
                       pl.BlockSpec((heads, tm), lambda i: (0, i))),
            out_shape=common_shapes + (jax.ShapeDtypeStruct((n, aw), F32), jax.ShapeDtypeStruct((n, aw), F32),
                                       lft_shape),
            compiler_params=_params(("parallel",)),
            name="in_proj_sample",
        )(x, w_b, b_row, wkv_t, bkv, wft_b, bft_col)
    seq = n // batch
    tm = _pick(seq, (512, 256, 128))
    nt = seq // tm
    row = lambda b, i: (b * nt + i, 0)
    tr = lambda b, i: (b, 0, i)
    t_f32 = jax.ShapeDtypeStruct((batch, aw, seq), F32)
    t_b16 = jax.ShapeDtypeStruct((batch, aw, seq), BF16)
    return pl.pallas_call(
        functools.partial(_inproj_prompt_body, cc=cc, aw=aw, scale=q_scale),
        grid=(batch, nt),
        in_specs=[pl.BlockSpec((tm, d), row)] + w_specs,
        out_specs=(pl.BlockSpec((tm, cc), row), pl.BlockSpec((tm, aw), row),
                   pl.BlockSpec((1, aw, tm), tr), pl.BlockSpec((1, aw, tm), tr),
                   pl.BlockSpec((1, aw, tm), tr), pl.BlockSpec((1, aw, tm), tr),
                   pl.BlockSpec((heads, tm), lambda b, i: (0, b * nt + i))),
        out_shape=common_shapes + (t_f32, t_f32, t_b16, t_b16, lft_shape),
        compiler_params=_params(("parallel", "parallel")),
        name="in_proj_prompt",
    )(x, w_b, b_row, wkv_t, bkv, wft_b, bft_col)


def _group_norm_silu(y, m, gg, gb):
    mean = _dot3(y, m)
    d = y - mean
    var = _dot3(d * d, m)
    yn = d * lax.rsqrt(var + LN_EPS) * gg + gb
    return yn * _sigmoid(yn)


def _conv_body(u_ref, halo_ref, w_ref, b_ref, gg_ref, gb_ref, m_ref, o_ref, ext_ref, sh_ref, y_ref, *, ts, rc):
    i = pl.program_id(1)
    c = u_ref.shape[1]
    rows = CONV_HALO + ts
    halo = halo_ref[...]
    ext_ref[0:CONV_HALO, :] = jnp.where(i > 0, halo, jnp.zeros_like(halo))
    ext_ref[CONV_HALO:rows, :] = u_ref[...]
    ext_ref[rows:rows + SUBLANES, :] = jnp.zeros((SUBLANES, c), F32)
    for r in range(SUBLANES):
        sh_ref[r] = ext_ref[r:r + rows, :]
    off = CONV_HALO - (DW_WIDTH - 1)
    for r0 in range(0, ts, rc):
        acc = jnp.broadcast_to(b_ref[...], (rc, c))
        for j in range(DW_WIDTH):
            s = r0 + off + j
            r = s % SUBLANES
            acc = acc + w_ref[j:j + 1, :] * sh_ref[r, s - r:s - r + rc, :]
        y_ref[r0:r0 + rc, :] = acc
    o_ref[...] = _group_norm_silu(y_ref[...], m_ref[...], gg_ref[...], gb_ref[...]).astype(BF16)


def _conv_prompt(u, batch, seq, w_dw, b_row, gg_row, gb_row, gmat):
    n, c = u.shape
    ts = _pick(seq, (512, 256, 128, 64, 32))
    rc = min(ts, 64)
    nt = seq // ts
    hb = ts // CONV_HALO
    rows = CONV_HALO + ts
    const = lambda b, i: (0, 0)
    return pl.pallas_call(
        functools.partial(_conv_body, ts=ts, rc=rc),
        grid=(batch, nt),
        in_specs=[
            pl.BlockSpec((ts, c), lambda b, i: (b * nt + i, 0)),
            pl.BlockSpec((CONV_HALO, c), lambda b, i: (jnp.maximum((b * nt + i) * hb - 1, 0), 0)),
            pl.BlockSpec((DW_WIDTH, c), const),
            pl.BlockSpec((1, c), const),
            pl.BlockSpec((1, c), const),
            pl.BlockSpec((1, c), const),
            pl.BlockSpec((c, c), const),
        ],
        out_specs=pl.BlockSpec((ts, c), lambda b, i: (b * nt + i, 0)),
        out_shape=jax.ShapeDtypeStruct((n, c), BF16),
        scratch_shapes=[pltpu.VMEM((rows + SUBLANES, c), F32), pltpu.VMEM((SUBLANES, rows, c), F32),
                        pltpu.VMEM((ts, c), F32)],
        compiler_params=_params(("parallel", "parallel")),
        name="conv_prompt",
    )(u, u, w_dw, b_row, gg_row, gb_row, gmat)


def _conv_sample_body(hist_ref, u_ref, w_ref, b_ref, gg_ref, gb_ref, m_ref, o_ref):
    acc = b_ref[...] + w_ref[DW_WIDTH - 1:DW_WIDTH, :] * u_ref[...]
    for j in range(DW_WIDTH - 1):
        acc = acc + w_ref[j:j + 1, :] * hist_ref[j]
    o_ref[...] = _group_norm_silu(acc, m_ref[...], gg_ref[...], gb_ref[...]).astype(BF16)


def _conv_sample(hist_t, u, w_dw, b_row, gg_row, gb_row, gmat):
    n, c = u.shape
    full2 = lambda i: (0, 0)
    return pl.pallas_call(
        _conv_sample_body,
        grid=(1,),
        in_specs=[
            pl.BlockSpec(hist_t.shape, lambda i: (0, 0, 0)),
            pl.BlockSpec((n, c), full2),
            pl.BlockSpec((DW_WIDTH, c), full2),
            pl.BlockSpec((1, c), full2),
            pl.BlockSpec((1, c), full2),
            pl.BlockSpec((1, c), full2),
            pl.BlockSpec((c, c), full2),
        ],
        out_specs=pl.BlockSpec((n, c), full2),
        out_shape=jax.ShapeDtypeStruct((n, c), BF16),
        compiler_params=_params(("arbitrary",)),
        name="conv_sample",
    )(hist_t, u, w_dw, b_row, gg_row, gb_row, gmat)


def _cumsum_body(lf_ref, tri_ref, o_ref, carry_ref):
    @pl.when(pl.program_id(1) == 0)
    def _():
        carry_ref[...] = jnp.zeros_like(carry_ref)

    cl = lf_ref.shape[1]
    cum = _dot3(lf_ref[...], tri_ref[...]) + carry_ref[:, 0:1]
    o_ref[...] = -LOG2_E * cum
    carry_ref[...] = jnp.broadcast_to(cum[:, cl - 1:cl], carry_ref.shape)


def _neg_cumsum(lft, batch, seq):
    heads, n = lft.shape
    cl = _pick(seq, (512, 256, 128))
    nc = seq // cl
    tri = jnp.triu(jnp.ones((cl, cl), F32)).astype(BF16)
    return pl.pallas_call(
        _cumsum_body,
        grid=(batch, nc),
        in_specs=[pl.BlockSpec((heads, cl), lambda b, c: (0, b * nc + c)),
                  pl.BlockSpec((cl, cl), lambda b, c: (0, 0))],
        out_specs=pl.BlockSpec((heads, cl), lambda b, c: (0, b * nc + c)),
        out_shape=jax.ShapeDtypeStruct((heads, n), F32),
        scratch_shapes=[pltpu.VMEM((heads, LANES), F32)],
        compiler_params=_params(("parallel", "arbitrary")),
        name="forget_cumsum",
    )(lft, tri)


def _att_body(qi_ref, ki_ref, q_ref, k_ref, v_ref, nd_ref, o_ref, m_ref, l_ref, acc_ref, *, hd, pairs):
    step = pl.program_id(2)
    qi = qi_ref[step]
    ki = ki_ref[step]
    tq = q_ref.shape[0]
    tk = k_ref.shape[2]
    is_a = lax.broadcasted_iota(I32, (1, LANES), 1) < hd

    @pl.when(ki == 0)
    def _():
        m_ref[...] = jnp.full(m_ref.shape, NEG_INF, F32)
        l_ref[...] = jnp.zeros_like(l_ref)
        acc_ref[...] = jnp.zeros_like(acc_ref)

    def process(diagonal):
        if diagonal:
            causal = (lax.broadcasted_iota(I32, (tq, tk), 1) <= lax.broadcasted_iota(I32, (tq, tk), 0))
        for g in range(pairs):
            lanes = slice(g * LANES, (g + 1) * LANES)
            q = q_ref[:, lanes]
            kt = k_ref[0, lanes, :]
            vt = v_ref[0, lanes, :]
            nd = nd_ref[g]
            zero = jnp.zeros_like(q)
            pvs = []
            alphas = []
            for h in range(2):
                qh = jnp.where(is_a, q, zero) if h == 0 else jnp.where(is_a, zero, q)
                s = _dot(qh, kt) + nd[h:h + 1, :]
                if diagonal:
                    s = jnp.where(causal, s, NEG_INF)
                m_old = m_ref[2 * g + h]
                m_new = jnp.maximum(m_old, jnp.max(s, axis=1, keepdims=True))
                alpha = jnp.exp2(m_old - m_new)
                p = jnp.exp2(s - jnp.tile(m_new, (1, tk // LANES)))
                l_ref[2 * g + h] = alpha * l_ref[2 * g + h] + jnp.sum(p, axis=1, keepdims=True)
                m_ref[2 * g + h] = m_new
                pvs.append(_dot_nt(p.astype(BF16), vt))
                alphas.append(alpha)
            acc_ref[:, lanes] = (acc_ref[:, lanes] * jnp.where(is_a, alphas[0], alphas[1])
                                 + jnp.where(is_a, pvs[0], pvs[1]))

    @pl.when(ki < qi)
    def _():
        process(False)

    @pl.when(ki == qi)
    def _():
        process(True)
        for g in range(pairs):
            lanes = slice(g * LANES, (g + 1) * LANES)
            o_ref[:, lanes] = (acc_ref[:, lanes]
                               / jnp.where(is_a, l_ref[2 * g], l_ref[2 * g + 1])).astype(BF16)


def _attention_prompt(q_b, kt_b, vt_b, negd, batch, seq):
    n, aw = q_b.shape
    pairs = ATT_PAIRS_PER_STEP
    groups = aw // (pairs * LANES)
    w = pairs * LANES
    t = _pick(seq, (512, 256, 128))
    nq = seq // t
    steps = [(a, b) for a in range(nq) for b in range(a + 1)]
    qi_tab = jnp.asarray([s[0] for s in steps], I32)
    ki_tab = jnp.asarray([s[1] for s in steps], I32)
    grid_spec = pltpu.PrefetchScalarGridSpec(
        num_scalar_prefetch=2,
        grid=(batch, groups, len(steps)),
        in_specs=[
            pl.BlockSpec((t, w), lambda b, g, s, qt, kt: (b * nq + qt[s], g)),
            pl.BlockSpec((1, w, t), lambda b, g, s, qt, kt: (b, g, kt[s])),
            pl.BlockSpec((1, w, t), lambda b, g, s, qt, kt: (b, g, kt[s])),
            pl.BlockSpec((pairs, 2, t), lambda b, g, s, qt, kt: (g, 0, b * nq + kt[s])),
        ],
        out_specs=pl.BlockSpec((t, w), lambda b, g, s, qt, kt: (b * nq + qt[s], g)),
        scratch_shapes=[pltpu.VMEM((2 * pairs, t, LANES), F32), pltpu.VMEM((2 * pairs, t, LANES), F32),
                        pltpu.VMEM((t, w), F32)],
    )
    return pl.pallas_call(
        functools.partial(_att_body, hd=HEAD_DIM, pairs=pairs),
        grid_spec=grid_spec,
        out_shape=jax.ShapeDtypeStruct((n, aw), BF16),
        compiler_params=_params(("parallel", "parallel", "arbitrary")),
        name="attention_prompt",
    )(qi_tab, ki_tab, q_b, kt_b, vt_b, negd)


def _pool_suffix_body(lf_ref, tri_ref, ones_ref, w_ref, t_ref):
    x = lf_ref[...]
    w_ref[...] = _dot3(x, tri_ref[...])
    t_ref[...] = _dot3(x, ones_ref[...])


def _pool_suffix(lf_rows):
    n, page = lf_rows.shape
    tp = _pick(n, (2048, 1024, 512, 256, 128, 64, 32, 16, 8, n))
    tri = jnp.tril(jnp.ones((page, page), F32), -1).astype(BF16)
    ones = jnp.ones((page, page), BF16)
    blk = pl.BlockSpec((tp, page), lambda i: (i, 0))
    mat = pl.BlockSpec((page, page), lambda i: (0, 0))
    return pl.pallas_call(
        _pool_suffix_body,
        grid=(n // tp,),
        in_specs=[blk, mat, mat],
        out_specs=(blk, blk),
        out_shape=(jax.ShapeDtypeStruct((n, page), F32), jax.ShapeDtypeStruct((n, page), F32)),
        compiler_params=_params(("parallel",)),
        name="pool_forget_suffix",
    )(lf_rows, tri, ones)


def _satt_body(pt_ref, *refs, n_pages, heads):
    k_refs = refs[:n_pages]
    v_refs = refs[n_pages:2 * n_pages]
    w_refs = refs[2 * n_pages:3 * n_pages]
    t_refs = refs[3 * n_pages:4 * n_pages]
    q_ref, kn_ref, vn_ref, lfn_ref, o_ref, s_ref = refs[4 * n_pages:]
    hd, page = k_refs[0].shape[2:]
    q = q_ref[0].astype(F32)
    s_new = jnp.sum(q * kn_ref[0], axis=1, keepdims=True)
    q_cols = [jnp.broadcast_to(q[h:h + 1, :], (page, hd)).T for h in range(heads)]
    later = lfn_ref[0]
    m = s_new
    for j in reversed(range(n_pages)):
        for h in range(heads):
            s_ref[j, h:h + 1, :] = jnp.sum(k_refs[j][0, h] * q_cols[h], axis=0, keepdims=True)
        s = s_ref[j] + (w_refs[j][0] + later)
        s_ref[j] = s
        m = jnp.maximum(m, jnp.max(s, axis=1, keepdims=True))
        later = later + t_refs[j][0]
    p_new = jnp.exp(s_new - m)
    l = p_new
    for j in range(n_pages):
        p = jnp.exp(s_ref[j] - m)
        s_ref[j] = p
        l = l + jnp.sum(p, axis=1, keepdims=True)
    vn = vn_ref[0]
    for h in range(heads):
        acc = jnp.zeros((hd, page), F32)
        for j in range(n_pages):
            acc = acc + v_refs[j][0, h] * s_ref[j, h:h + 1, :]
        o = jnp.sum(acc.T, axis=0, keepdims=True) + p_new[h:h + 1, :] * vn[h:h + 1, :]
        o_ref[0, h:h + 1, :] = (o / l[h:h + 1, :]).astype(BF16)


def _attention_sample(q_b, k_new, v_new, lf_new, cache_kt, cache_vt, cache_lft, page_table):
    n, aw = q_b.shape
    n_pages = page_table.shape[1]
    pool, heads, hd, page = cache_kt.shape
    within, totals = _pool_suffix(cache_lft.reshape(pool * heads, page))
    within = within.reshape(pool, heads, page)
    totals = totals.reshape(pool, heads, page)
    lfn_lanes = jnp.broadcast_to(lf_new[:, :, None], (n, heads, page))

    def kv_spec(j):
        return pl.BlockSpec((1, heads, hd, page), lambda i, pt, j=j: (pt[i, j], 0, 0, 0))

    def row_spec(j):
        return pl.BlockSpec((1, heads, page), lambda i, pt, j=j: (pt[i, j], 0, 0))

    per_seq = lambda shape: pl.BlockSpec((1,) + shape, lambda i, pt: (i, 0, 0))
    grid_spec = pltpu.PrefetchScalarGridSpec(
        num_scalar_prefetch=1,
        grid=(n,),
        in_specs=([kv_spec(j) for j in range(n_pages)] + [kv_spec(j) for j in range(n_pages)]
                  + [row_spec(j) for j in range(n_pages)] + [row_spec(j) for j in range(n_pages)]
                  + [per_seq((heads, hd)), per_seq((heads, hd)), per_seq((heads, hd)), per_seq((heads, page))]),
        out_specs=per_seq((heads, hd)),
        scratch_shapes=[pltpu.VMEM((n_pages, heads, page), F32)],
    )
    out = pl.pallas_call(
        functools.partial(_satt_body, n_pages=n_pages, heads=heads),
        grid_spec=grid_spec,
        out_shape=jax.ShapeDtypeStruct((n, heads, hd), BF16),
        compiler_params=_params(("arbitrary",)),
        name="attention_sample",
    )(page_table, *([cache_kt] * n_pages), *([cache_vt] * n_pages), *([within] * n_pages), *([totals] * n_pages),
      q_b.reshape(n, heads, hd), k_new.reshape(n, heads, hd), v_new.reshape(n, heads, hd), lfn_lanes)
    return out.reshape(n, aw)


def _mix_body(x_ref, c_ref, a_ref, p_ref, wo_ref, g1_ref, b1_ref, wr_ref, br_ref,
              wpg_ref, bpg_ref, wpp_ref, h_ref, base_ref, idx_ref, gate_ref, cnt_ref, *, alpha, cc):
    mix = _dot(c_ref[...], wo_ref[:cc, :]) + _dot(a_ref[...], wo_ref[cc:, :])
    h = _layer_norm(alpha * x_ref[...] + mix, g1_ref[...], b1_ref[...])
    h_ref[...] = h
    hb = h.astype(BF16)
    hl = (h - hb.astype(F32)).astype(BF16)
    both = _dot(hb, wr_ref[...])
    logits = both[:, :LANES] + both[:, LANES:] + _dot(hl, wr_ref[:, :LANES]) + br_ref[...]

    tm = logits.shape[0]
    lane = lax.broadcasted_iota(I32, (tm, LANES), 1).astype(F32)
    vals, idxs = [], []
    l = logits
    for _ in range(TOP_K):
        m = jnp.max(l, axis=1, keepdims=True)
        ix = jnp.min(jnp.where(l == m, lane, float(LANES)), axis=1, keepdims=True)
        vals.append(m)
        idxs.append(ix)
        l = jnp.where(lane == ix, -jnp.inf, l)
    es = [jnp.exp(v - vals[0]) for v in vals]
    den = es[0] + es[1] + es[2] + es[3]
    idx_w = jnp.zeros((tm, LANES), F32)
    gate_w = jnp.zeros((tm, LANES), F32)
    sel = jnp.zeros((tm, LANES), F32)
    for r in range(TOP_K):
        idx_w = jnp.where(lane == float(r), idxs[r], idx_w)
        gate_w = jnp.where(lane == float(r), es[r] / den, gate_w)
        sel = sel + (lane == idxs[r]).astype(F32)
    idx_ref[...] = idx_w
    gate_ref[...] = gate_w

    @pl.when(pl.program_id(0) == 0)
    def _():
        cnt_ref[...] = jnp.zeros_like(cnt_ref)

    cnt_ref[...] += jnp.sum(sel, axis=0, keepdims=True)

    ple = _sigmoid(_dot(hb, wpg_ref[...]) + bpg_ref[...]) * _dot(p_ref[...].astype(BF16), wpp_ref[...])
    base_ref[...] = alpha * h + ple


def _mix(x, conv_b, att_b, p, wo_b, g1, b1, wr_pair, br_row, wpg_b, bpg, wpp_b, alpha):
    n, d = x.shape
    cc = conv_b.shape[1]
    aw = att_b.shape[1]
    pd = p.shape[1]
    tm = _pick(n, (512, 256, 128, 64, 32, 16, 8))
    row = lambda i: (i, 0)
    const = lambda i: (0, 0)
    outs = (
        jax.ShapeDtypeStruct((n, d), F32),
        jax.ShapeDtypeStruct((n, d), F32),
        jax.ShapeDtypeStruct((n, LANES), F32),
        jax.ShapeDtypeStruct((n, LANES), F32),
        jax.ShapeDtypeStruct((1, LANES), F32),
    )
    return pl.pallas_call(
        functools.partial(_mix_body, alpha=alpha, cc=cc),
        grid=(n // tm,),
        in_specs=[
            pl.BlockSpec((tm, d), row), pl.BlockSpec((tm, cc), row), pl.BlockSpec((tm, aw), row),
            pl.BlockSpec((tm, pd), row),
            pl.BlockSpec((d, d), const), pl.BlockSpec((1, d), const), pl.BlockSpec((1, d), const),
            pl.BlockSpec((d, 2 * LANES), const), pl.BlockSpec((1, LANES), const),
            pl.BlockSpec((d, d), const), pl.BlockSpec((1, d), const), pl.BlockSpec((pd, d), const),
        ],
        out_specs=(
            pl.BlockSpec((tm, d), row), pl.BlockSpec((tm, d), row),
            pl.BlockSpec((tm, LANES), row), pl.BlockSpec((tm, LANES), row),
            pl.BlockSpec((1, LANES), const),
        ),
        out_shape=outs,
        compiler_params=_params(("arbitrary",)),
        name="out_proj_router",
    )(x, conv_b, att_b, p, wo_b, g1, b1, wr_pair, br_row, wpg_b, bpg, wpp_b)


def _positions_body(idx_ref, start_ref, tri_ref, o_ref, carry_ref):
    @pl.when(pl.program_id(0) == 0)
    def _():
        carry_ref[...] = jnp.zeros_like(carry_ref)

    ids = idx_ref[...]
    tm = ids.shape[0]
    lane = lax.broadcasted_iota(I32, (tm, LANES), 1).astype(F32)
    hot = [lane == ids[:, r:r + 1] for r in range(TOP_K)]
    sel = jnp.zeros((tm, LANES), F32)
    for r in range(TOP_K):
        sel = sel + hot[r].astype(F32)
    rank = _dot(tri_ref[...], sel.astype(BF16)) + carry_ref[...]
    pos = rank + start_ref[...]
    out = jnp.zeros((tm, LANES), F32)
    for r in range(TOP_K):
        d = jnp.sum(jnp.where(hot[r], pos, 0.0), axis=1, keepdims=True)
        out = jnp.where(lane == float(r), d, out)
    o_ref[...] = out.astype(I32)
    carry_ref[...] += jnp.sum(sel, axis=0, keepdims=True)


def _positions(idx_w, start_row):
    n = idx_w.shape[0]
    tm = _pick(n, (512, 384, 256, 128, 64, 32, 16, 8))
    tri = jnp.tril(jnp.ones((tm, tm), F32), -1).astype(BF16)
    return pl.pallas_call(
        _positions_body,
        grid=(n // tm,),
        in_specs=[pl.BlockSpec((tm, LANES), lambda i: (i, 0)),
                  pl.BlockSpec((1, LANES), lambda i: (0, 0)),
                  pl.BlockSpec((tm, tm), lambda i: (0, 0))],
        out_specs=pl.BlockSpec((tm, LANES), lambda i: (i, 0)),
        out_shape=jax.ShapeDtypeStruct((n, LANES), I32),
        scratch_shapes=[pltpu.VMEM((1, LANES), F32)],
        compiler_params=_params(("arbitrary",)),
        name="route_positions",
    )(idx_w, start_row, tri)


def _dispatch_body(dest_ref, code_ref, h_ref, xs_in_ref, inv_src_ref, xs_ref, inv_ref, sem, inv_sem, *, tb):
    del xs_in_ref

    @pl.when(pl.program_id(0) == 0)
    def _():
        cp = pltpu.make_async_copy(inv_src_ref, inv_ref, inv_sem)
        cp.start()
        cp.wait()

    def issue(t, c):
        for r in range(TOP_K):
            j = t * TOP_K + r
            d = dest_ref[0, 0, j]
            inv_ref[d] = code_ref[0, 0, j]
            pltpu.make_async_copy(h_ref.at[pl.ds(t, 1)], xs_ref.at[pl.ds(d, 1)], sem).start(priority=r % 2)
        return c

    lax.fori_loop(0, tb, issue, 0)
    for r in range(TOP_K):
        pltpu.make_async_copy(h_ref, xs_ref.at[pl.ds(0, tb)], sem).wait()


def _dispatch(h, dest, xs, inv, tok0, n_tok):
    n, d = h.shape
    tb = _pick(n, (512, 256, 128, 64, 32, 16, 8))
    dest3 = dest.reshape(n // tb, 1, tb * TOP_K)
    code3 = (jnp.arange(TOP_K, dtype=I32)[None, :] * n_tok
             + (tok0 + jnp.arange(n, dtype=I32))[:, None]).reshape(n // tb, 1, tb * TOP_K)
    smem_blk = pl.BlockSpec((1, 1, tb * TOP_K), lambda i: (i, 0, 0), memory_space=pltpu.SMEM)
    return pl.pallas_call(
        functools.partial(_dispatch_body, tb=tb),
        grid=(n // tb,),
        in_specs=[
            smem_blk, smem_blk,
            pl.BlockSpec((tb, d), lambda i: (i, 0)),
            pl.BlockSpec(memory_space=pl.ANY),
            pl.BlockSpec(memory_space=pl.ANY),
        ],
        out_specs=(pl.BlockSpec(memory_space=pl.ANY), pl.BlockSpec(memory_space=pltpu.SMEM)),
        out_shape=(jax.ShapeDtypeStruct(xs.shape, xs.dtype), jax.ShapeDtypeStruct(inv.shape, inv.dtype)),
        scratch_shapes=[pltpu.SemaphoreType.DMA, pltpu.SemaphoreType.DMA],
        input_output_aliases={3: 0},
        compiler_params=_params(("arbitrary",)),
        name="moe_dispatch",
    )(dest3, code3, h, xs, inv)


def _expert_body(be_ref, nu_ref, inv_ref, x_ref, wg_ref, bg_ref, wu_ref, bu_ref, wd_ref, bd_ref, out_ref,
                 wgb_ref, wub_ref, wdb_ref, ya_ref, yb_ref, sem_a, sem_b, sem_z, *, n_slots):
    i = pl.program_id(0)
    half = MOE_BLOCK // 2

    def scatter(y_ref, row0, sem):
        for k in range(half):
            slot = inv_ref[row0 + k]
            pltpu.make_async_copy(y_ref.at[pl.ds(k, 1)], out_ref.at[pl.ds(slot, 1)], sem).start(priority=1)

    def drain(y_ref, sem):
        pltpu.make_async_copy(y_ref, out_ref.at[pl.ds(0, half)], sem).wait()

    def ffn(rows):
        x = x_ref[rows, :].astype(BF16)
        g = jnp.minimum(_dot(x, wgb_ref[...]) + bg_ref[0], SWIGLU_LIMIT)
        u = jnp.clip(_dot(x, wub_ref[...]) + bu_ref[0], -SWIGLU_LIMIT, SWIGLU_LIMIT)
        hidden = (u + 1.0) * (g * _sigmoid(SWIGLU_ALPHA * g))
        return _dot(hidden.astype(BF16), wdb_ref[...]) + bd_ref[0]

    @pl.when(i == 0)
    def _():
        ya_ref[...] = jnp.zeros_like(ya_ref)
        yb_ref[...] = jnp.zeros_like(yb_ref)
        pltpu.make_async_copy(ya_ref, out_ref.at[pl.ds(n_slots, half)], sem_a).start()
        cp = pltpu.make_async_copy(yb_ref, out_ref.at[pl.ds(n_slots + half, half)], sem_z)
        cp.start()
        cp.wait()

    @pl.when(i < nu_ref[0])
    def _():
        @pl.when((i == 0) | (be_ref[i] != be_ref[jnp.maximum(i - 1, 0)]))
        def _():
            wgb_ref[...] = wg_ref[0].astype(BF16)
            wub_ref[...] = wu_ref[0].astype(BF16)
            wdb_ref[...] = wd_ref[0].astype(BF16)

        drain(ya_ref, sem_a)
        scatter(yb_ref, i * MOE_BLOCK + half, sem_b)
        ya_ref[...] = ffn(slice(0, half))
        drain(yb_ref, sem_b)
        scatter(ya_ref, (i + 1) * MOE_BLOCK, sem_a)
        yb_ref[...] = ffn(slice(half, MOE_BLOCK))

    @pl.when(i == nu_ref[0])
    def _():
        drain(ya_ref, sem_a)
        scatter(yb_ref, i * MOE_BLOCK + half, sem_b)
        drain(yb_ref, sem_b)


def _experts(xs, block_e, n_used, inv_shifted, n_slots, w_gate, b_gate, w_up, b_up, w_down, b_down):
    rows, d = xs.shape
    n_e, _, de = w_gate.shape
    n_blocks = rows // MOE_BLOCK
    half = MOE_BLOCK // 2
    last = lambda i, be, nu, inv: (jnp.minimum(i, nu[0] - 1), 0)
    wsel = lambda i, be, nu, inv: (be[i], 0, 0)
    grid_spec = pltpu.PrefetchScalarGridSpec(
        num_scalar_prefetch=3,
        grid=(n_blocks,),
        in_specs=[
            pl.BlockSpec((MOE_BLOCK, d), last),
            pl.BlockSpec((1, d, de), wsel), pl.BlockSpec((1, 1, de), wsel),
            pl.BlockSpec((1, d, de), wsel), pl.BlockSpec((1, 1, de), wsel),
            pl.BlockSpec((1, de, d), wsel), pl.BlockSpec((1, 1, d), wsel),
        ],
        out_specs=pl.BlockSpec(memory_space=pl.ANY),
        scratch_shapes=[pltpu.VMEM((d, de), BF16), pltpu.VMEM((d, de), BF16), pltpu.VMEM((de, d), BF16),
                        pltpu.VMEM((half, d), F32), pltpu.VMEM((half, d), F32),
                        pltpu.SemaphoreType.DMA, pltpu.SemaphoreType.DMA, pltpu.SemaphoreType.DMA],
    )
    return pl.pallas_call(
        functools.partial(_expert_body, n_slots=n_slots),
        grid_spec=grid_spec,
        out_shape=jax.ShapeDtypeStruct((n_slots + MOE_BLOCK, d), F32),
        compiler_params=_params(("arbitrary",)),
        name="moe_experts",
    )(block_e, n_used, inv_shifted, xs, w_gate, b_gate[:, None, :], w_up, b_up[:, None, :],
      w_down, b_down[:, None, :])


def _combine_body(base_ref, gate_ref, *refs):
    y_refs = refs[:TOP_K]
    g2_ref, b2_ref, o_ref = refs[TOP_K:]
    gate = gate_ref[...]
    y = base_ref[...]
    for r in range(TOP_K):
        y = y + gate[:, r:r + 1] * y_refs[r][...]
    o_ref[...] = _layer_norm(y, g2_ref[...], b2_ref[...])


def _combine(base, gate_w, slots, tok0, n_tok, g2, b2):
    n, d = base.shape
    tb = _pick(math.gcd(n, math.gcd(tok0, n_tok)), (512, 256, 128, 64, 32, 16, 8))
    row = lambda i: (i, 0)
    const = lambda i: (0, 0)

    def slot_spec(r):
        return pl.BlockSpec((tb, d), lambda i, r=r: ((r * n_tok + tok0) // tb + i, 0))

    return pl.pallas_call(
        _combine_body,
        grid=(n // tb,),
        in_specs=([pl.BlockSpec((tb, d), row), pl.BlockSpec((tb, LANES), row)]
                  + [slot_spec(r) for r in range(TOP_K)]
                  + [pl.BlockSpec((1, d), const), pl.BlockSpec((1, d), const)]),
        out_specs=pl.BlockSpec((tb, d), row),
        out_shape=jax.ShapeDtypeStruct((n, d), F32),
        compiler_params=_params(("parallel",)),
        name="moe_combine",
    )(base, gate_w, *([slots] * TOP_K), g2, b2)


def _layer(x_p, x_s, ck, cv, clf, hist_s, page_table, p_p, p_s, lw, alpha):
    batch, seq, d = x_p.shape
    n_s = x_s.shape[0]
    n_p = batch * seq
    cc = lw["w_dw"].shape[1]
    aw = ATT_HEADS * HEAD_DIM
    heads = ATT_HEADS
    nz = 2 * cc + 3 * aw
    row = lambda v: v[None, :]

    nq_cols = 2 * cc + aw
    w_in_t = lw["w_in"].T
    w_in_b = lw["w_in"][:, :nq_cols].astype(BF16)
    b_in_row = row(lw["b_in"][:nq_cols])
    wkv_t = w_in_t[nq_cols:nz].astype(BF16)
    bkv = lw["b_in"][nq_cols:nz]
    wft_b = w_in_t[nz:].astype(BF16)
    bft_col = lw["b_in"][nz:, None]
    b_dw, gg, gb = row(lw["b_dw"]), row(lw["gn_gain"]), row(lw["gn_bias"])
    gsz = cc // CONV_GROUPS
    grp = jnp.arange(cc) // gsz
    gmat = ((grp[:, None] == grp[None, :]).astype(F32) / gsz).astype(BF16)
    wo_b = lw["w_out"].astype(BF16)
    wr = jnp.pad(lw["w_router"], ((0, 0), (0, LANES - N_EXPERTS)))
    wr_hi = wr.astype(BF16)
    wr_lo = (wr - wr_hi.astype(F32)).astype(BF16)
    wr_pair = jnp.concatenate([wr_hi, wr_lo], axis=1)
    br_row = row(jnp.pad(lw["b_router"], (0, LANES - N_EXPERTS), constant_values=NEG_INF))
    wpg_b = lw["w_ple_gate"].astype(BF16)
    wpp_b = lw["w_ple_proj"].astype(BF16)

    xp2 = x_p.reshape(n_p, d)
    xs2 = x_s.reshape(n_s, d)

    u_p, q_p, kt_p, vt_p, ktb_p, vtb_p, lft_p = _in_proj(xp2, w_in_b, b_in_row, wkv_t, bkv[:, None], wft_b, bft_col,
                                                         cc, aw, HEAD_DIM ** -0.5 * LOG2_E, batch=batch)
    conv_p = _conv_prompt(u_p, batch, seq, lw["w_dw"], b_dw, gg, gb, gmat)
    negd = _neg_cumsum(lft_p, batch, seq).reshape(heads // 2, 2, n_p)
    att_p = _attention_prompt(q_p, ktb_p, vtb_p, negd, batch, seq)
    h_p, base_p, idx_p, gate_p, cnt_p = _mix(xp2, conv_p, att_p, p_p.reshape(n_p, -1), wo_b,
                                             row(lw["ln1_gain"]), row(lw["ln1_bias"]), wr_pair, br_row,
                                             wpg_b, row(lw["b_ple_gate"]), wpp_b, alpha)

    u_s, q_s, k_s, v_s, lft_s = _in_proj(xs2, w_in_b, b_in_row, wkv_t, bkv[None, :], wft_b, bft_col,
                                         cc, aw, HEAD_DIM ** -0.5)
    conv_s = _conv_sample(hist_s.transpose(1, 0, 2), u_s, lw["w_dw"], b_dw, gg, gb, gmat)
    lf_s = lft_s.T
    att_s = _attention_sample(q_s, k_s, v_s, lf_s, ck.transpose(0, 2, 3, 1), cv.transpose(0, 2, 3, 1),
                              clf.transpose(0, 2, 1), page_table)
    h_s, base_s, idx_s, gate_s, cnt_s = _mix(xs2, conv_s, att_s, p_s.reshape(n_s, -1), wo_b,
                                             row(lw["ln1_gain"]), row(lw["ln1_bias"]), wr_pair, br_row,
                                             wpg_b, row(lw["b_ple_gate"]), wpp_b, alpha)

    n_tok = n_p + n_s
    counts = (cnt_p + cnt_s)[0].astype(I32)
    padded = (counts + MOE_BLOCK - 1) // MOE_BLOCK * MOE_BLOCK
    pad_end = jnp.cumsum(padded)
    pad_start = pad_end - padded
    n_blocks = -(-(n_tok * TOP_K) // MOE_BLOCK) + N_EXPERTS + 1
    n_used = (pad_end[N_EXPERTS - 1] // MOE_BLOCK).astype(I32).reshape(1)
    block_row0 = jnp.arange(n_blocks, dtype=I32) * MOE_BLOCK
    block_e = jnp.minimum(jnp.sum((pad_end[None, :N_EXPERTS] <= block_row0[:, None]).astype(I32), axis=1),
                          N_EXPERTS - 1)
    dest = _positions(jnp.concatenate([idx_p, idx_s], axis=0), pad_start.astype(F32)[None, :])[:, :TOP_K]
    dest_p, dest_s = dest[:n_p], dest[n_p:]

    n_rows = n_blocks * MOE_BLOCK
    n_slots = TOP_K * n_tok
    inv = n_slots + jnp.arange(n_rows, dtype=I32) % MOE_BLOCK
    xs_rows = jnp.zeros((n_rows, d), F32)
    xs_rows, inv = _dispatch(h_p, dest_p, xs_rows, inv, 0, n_tok)
    xs_rows, inv = _dispatch(h_s, dest_s, xs_rows, inv, n_p, n_tok)
    inv_shifted = jnp.concatenate([n_slots + jnp.arange(MOE_BLOCK, dtype=I32), inv])
    slots = _experts(xs_rows, block_e, n_used, inv_shifted, n_slots, lw["w_gate"], lw["b_gate"],
                     lw["w_up"], lw["b_up"], lw["w_down"], lw["b_down"])
    g2, b2 = row(lw["ln2_gain"]), row(lw["ln2_bias"])
    y_p = _combine(base_p, gate_p, slots, 0, n_tok, g2, b2).reshape(batch, seq, d)
    y_s = _combine(base_s, gate_s, slots, n_p, n_tok, g2, b2).reshape(n_s, 1, d)

    lf_p = lft_p.T.reshape(batch, seq, heads)
    hist_p = u_p.reshape(batch, seq, cc)[:, seq - (DW_WIDTH - 1):]
    hist_new_s = jnp.concatenate([hist_s[:, 1:], u_s[:, None, :]], axis=1)
    return (y_p, y_s,
            kt_p.reshape(batch, heads, HEAD_DIM, seq).transpose(0, 3, 1, 2),
            vt_p.reshape(batch, heads, HEAD_DIM, seq).transpose(0, 3, 1, 2), lf_p, hist_p,
            k_s.reshape(n_s, 1, heads, HEAD_DIM), v_s.reshape(n_s, 1, heads, HEAD_DIM),
            lf_s.reshape(n_s, 1, heads), hist_new_s)


def kernel(x_prompt, x_sample, cache_k, cache_v, cache_logf, state_conv, page_table, p_prompt, p_sample,
           w_in, b_in, w_dw, b_dw, gn_gain, gn_bias, w_out, ln1_gain, ln1_bias,
           w_router, b_router, w_gate, b_gate, w_up, b_up, w_down, b_down,
           w_ple_gate, b_ple_gate, w_ple_proj, ln2_gain, ln2_bias):
    depth = w_in.shape[0]
    if depth != 1 or x_sample.shape[1] != 1:
        raise NotImplementedError("one layer and one new token per sample sequence are supported")
    alpha = (2.0 * depth) ** 0.25
    lw = dict(w_in=w_in[0], b_in=b_in[0], w_dw=w_dw[0], b_dw=b_dw[0], gn_gain=gn_gain[0], gn_bias=gn_bias[0],
              w_out=w_out[0], ln1_gain=ln1_gain[0], ln1_bias=ln1_bias[0],
              w_router=w_router[0], b_router=b_router[0], w_gate=w_gate[0], b_gate=b_gate[0],
              w_up=w_up[0], b_up=b_up[0], w_down=w_down[0], b_down=b_down[0],
              w_ple_gate=w_ple_gate[0], b_ple_gate=b_ple_gate[0], w_ple_proj=w_ple_proj[0],
              ln2_gain=ln2_gain[0], ln2_bias=ln2_bias[0])
    outs = _layer(x_prompt, x_sample[:, 0], cache_k[0], cache_v[0], cache_logf[0], state_conv[0], page_table,
                  p_prompt[0], p_sample[0], lw, alpha)
    (y_p, y_s, k_p, v_p, lf_p, c_p, k_s, v_s, lf_s, c_s) = outs
    return (y_p, y_s, k_p[None], v_p[None], lf_p[None], c_p[None], k_s[None], v_s[None], lf_s[None], c_s[None])
```

```python
import functools
import math

import jax
import jax.numpy as jnp
from jax import lax
from jax.experimental import pallas as pl
from jax.experimental.pallas import tpu as pltpu

F32 = jnp.float32
BF16 = jnp.bfloat16
I32 = jnp.int32

ATT_HEADS = 8
HEAD_DIM = 64
CONV_GROUPS = 8
DW_WIDTH = 31
N_EXPERTS = 32
TOP_K = 4
SWIGLU_LIMIT = 7.0
SWIGLU_ALPHA = 1.702
LN_EPS = 1e-5
NEG_INF = -1e30
LOG2_E = 1.4426950408889634

LANES = 128
SUBLANES = 8
CONV_HALO = 32
MOE_BLOCK = 512
ATT_PAIRS_PER_STEP = 4
VMEM_LIMIT = 56 * 1024 * 1024


def _pick(n, cands):
    for c in cands:
        if n % c == 0:
            return c
    raise ValueError(f"no block size in {cands} divides {n}")


def _params(sem, vmem=VMEM_LIMIT):
    return pltpu.CompilerParams(dimension_semantics=sem, vmem_limit_bytes=vmem)


def _sigmoid(x):
    return 1.0 / (1.0 + jnp.exp(-x))


def _log_sigmoid(x):
    return jnp.minimum(x, 0.0) - jnp.log1p(jnp.exp(-jnp.abs(x)))


def _dot(a, b):
    return jnp.dot(a, b, preferred_element_type=F32)


def _dot_nt(a, b):
    return lax.dot_general(a, b, (((1,), (1,)), ((), ())), preferred_element_type=F32)


def _split3(x):
    x1 = x.astype(BF16)
    r1 = x - x1.astype(F32)
    x2 = r1.astype(BF16)
    x3 = (r1 - x2.astype(F32)).astype(BF16)
    return x1, x2, x3


def _dot3(x, m):
    x1, x2, x3 = _split3(x)
    return _dot(x1, m) + _dot(x2, m) + _dot(x3, m)


def _layer_norm(x, g, b):
    mu = jnp.mean(x, axis=-1, keepdims=True)
    d = x - mu
    var = jnp.mean(d * d, axis=-1, keepdims=True)
    return d * lax.rsqrt(var + LN_EPS) * g + b


def _inproj_common(x_ref, w_ref, b_ref, wft_ref, bft_ref, u_ref, q_ref, lft_ref, *, cc, aw, scale):
    x = x_ref[...].astype(BF16)

    def seg(lo, hi):
        return _dot(x, w_ref[:, lo:hi]) + b_ref[:, lo:hi]

    u_ref[...] = seg(0, cc) * _sigmoid(seg(cc, 2 * cc))
    q_ref[...] = (seg(2 * cc, 2 * cc + aw) * scale).astype(BF16)
    lft_ref[...] = _log_sigmoid(_dot_nt(wft_ref[...], x) + bft_ref[...])
    return x


def _inproj_prompt_body(x_ref, w_ref, b_ref, wkv_ref, bkv_ref, wft_ref, bft_ref,
                        u_ref, q_ref, kt_ref, vt_ref, ktb_ref, vtb_ref, lft_ref, *, cc, aw, scale):
    x = _inproj_common(x_ref, w_ref, b_ref, wft_ref, bft_ref, u_ref, q_ref, lft_ref, cc=cc, aw=aw, scale=scale)
    kt = _dot_nt(wkv_ref[0:aw, :], x) + bkv_ref[0:aw, :]
    kt_ref[0] = kt
    ktb_ref[0] = kt.astype(BF16)
    vt = _dot_nt(wkv_ref[aw:2 * aw, :], x) + bkv_ref[aw:2 * aw, :]
    vt_ref[0] = vt
    vtb_ref[0] = vt.astype(BF16)


def _inproj_sample_body(x_ref, w_ref, b_ref, wkv_ref, bkv_ref, wft_ref, bft_ref,
                        u_ref, q_ref, k_ref, v_ref, lft_ref, *, cc, aw, scale):
    x = _inproj_common(x_ref, w_ref, b_ref, wft_ref, bft_ref, u_ref, q_ref, lft_ref, cc=cc, aw=aw, scale=scale)
    k_ref[...] = _dot_nt(x, wkv_ref[0:aw, :]) + bkv_ref[:, 0:aw]
    v_ref[...] = _dot_nt(x, wkv_ref[aw:2 * aw, :]) + bkv_ref[:, aw:2 * aw]


def _in_proj(x, w_b, b_row, wkv_t, bkv, wft_b, bft_col, cc, aw, q_scale, batch=None):
    n, d = x.shape
    nw = w_b.shape[1]
    heads = wft_b.shape[0]
    const = lambda *_: (0, 0)
    w_specs = [pl.BlockSpec((d, nw), const), pl.BlockSpec((1, nw), const),
               pl.BlockSpec((2 * aw, d), const), pl.BlockSpec(bkv.shape, const),
               pl.BlockSpec((heads, d), const), pl.BlockSpec((heads, 1), const)]
    common_shapes = (jax.ShapeDtypeStruct((n, cc), F32),
                     jax.ShapeDtypeStruct((n, aw), BF16))
    lft_shape = jax.ShapeDtypeStruct((heads, n), F32)
    if batch is None:
        tm = _pick(n, (512, 384, 256, 128, 64, 32, 16, 8))
        row = lambda i: (i, 0)
        return pl.pallas_call(
            functools.partial(_inproj_sample_body, cc=cc, aw=aw, scale=q_scale),
            grid=(n // tm,),
            in_specs=[pl.BlockSpec((tm, d), row)] + w_specs,
            out_specs=(pl.BlockSpec((tm, cc), row), pl.BlockSpec((tm, aw), row),
                       pl.BlockSpec((tm, aw), row), pl.BlockSpec((tm, aw), row),
                       pl.BlockSpec((heads, tm), lambda i: (0, i))),
            out_shape=common_shapes + (jax.ShapeDtypeStruct((n, aw), F32), jax.ShapeDtypeStruct((n, aw), F32),
                                       lft_shape),
            compiler_params=_params(("parallel",)),
            name="in_proj_sample",
        )(x, w_b, b_row, wkv_t, bkv, wft_b, bft_col)
    seq = n // batch
    tm = _pick(seq, (512, 256, 128))
    nt = seq // tm
    row = lambda b, i: (b * nt + i, 0)
    tr = lambda b, i: (b, 0, i)
    t_f32 = jax.ShapeDtypeStruct((batch, aw, seq), F32)
    t_b16 = jax.ShapeDtypeStruct((batch, aw, seq), BF16)
    return pl.pallas_call(
        functools.partial(_inproj_prompt_body, cc=cc, aw=aw, scale=q_scale),
        grid=(batch, nt),
        in_specs=[pl.BlockSpec((tm, d), row)] + w_specs,
        out_specs=(pl.BlockSpec((tm, cc), row), pl.BlockSpec((tm, aw), row),
                   pl.BlockSpec((1, aw, tm), tr), pl.BlockSpec((1, aw, tm), tr),
                   pl.BlockSpec((1, aw, tm), tr), pl.BlockSpec((1, aw, tm), tr),
                   pl.BlockSpec((heads, tm), lambda b, i: (0, b * nt + i))),
        out_shape=common_shapes + (t_f32, t_f32, t_b16, t_b16, lft_shape),
        compiler_params=_params(("parallel", "parallel")),
        name="in_proj_prompt",
    )(x, w_b, b_row, wkv_t, bkv, wft_b, bft_col)


def _group_norm_silu(y, m, gg, gb):
    mean = _dot3(y, m)
    d = y - mean
    var = _dot3(d * d, m)
    yn = d * lax.rsqrt(var + LN_EPS) * gg + gb
    return yn * _sigmoid(yn)


def _conv_body(u_ref, halo_ref, w_ref, b_ref, gg_ref, gb_ref, m_ref, o_ref, ext_ref, sh_ref, y_ref, *, ts, rc):
    i = pl.program_id(1)
    c = u_ref.shape[1]
    rows = CONV_HALO + ts
    halo = halo_ref[...]
    ext_ref[0:CONV_HALO, :] = jnp.where(i > 0, halo, jnp.zeros_like(halo))
    ext_ref[CONV_HALO:rows, :] = u_ref[...]
    ext_ref[rows:rows + SUBLANES, :] = jnp.zeros((SUBLANES, c), F32)
    for r in range(SUBLANES):
        sh_ref[r] = ext_ref[r:r + rows, :]
    off = CONV_HALO - (DW_WIDTH - 1)
    for r0 in range(0, ts, rc):
        acc = jnp.broadcast_to(b_ref[...], (rc, c))
        for j in range(DW_WIDTH):
            s = r0 + off + j
            r = s % SUBLANES
            acc = acc + w_ref[j:j + 1, :] * sh_ref[r, s - r:s - r + rc, :]
        y_ref[r0:r0 + rc, :] = acc
    o_ref[...] = _group_norm_silu(y_ref[...], m_ref[...], gg_ref[...], gb_ref[...]).astype(BF16)


def _conv_prompt(u, batch, seq, w_dw, b_row, gg_row, gb_row, gmat):
    n, c = u.shape
    ts = _pick(seq, (512, 256, 128, 64, 32))
    rc = min(ts, 256)
    nt = seq // ts
    hb = ts // CONV_HALO
    rows = CONV_HALO + ts
    const = lambda b, i: (0, 0)
    return pl.pallas_call(
        functools.partial(_conv_body, ts=ts, rc=rc),
        grid=(batch, nt),
        in_specs=[
            pl.BlockSpec((ts, c), lambda b, i: (b * nt + i, 0)),
            pl.BlockSpec((CONV_HALO, c), lambda b, i: (jnp.maximum((b * nt + i) * hb - 1, 0), 0)),
            pl.BlockSpec((DW_WIDTH, c), const),
            pl.BlockSpec((1, c), const),
            pl.BlockSpec((1, c), const),
            pl.BlockSpec((1, c), const),
            pl.BlockSpec((c, c), const),
        ],
        out_specs=pl.BlockSpec((ts, c), lambda b, i: (b * nt + i, 0)),
        out_shape=jax.ShapeDtypeStruct((n, c), BF16),
        scratch_shapes=[pltpu.VMEM((rows + SUBLANES, c), F32), pltpu.VMEM((SUBLANES, rows, c), F32),
                        pltpu.VMEM((ts, c), F32)],
        compiler_params=_params(("parallel", "parallel")),
        name="conv_prompt",
    )(u, u, w_dw, b_row, gg_row, gb_row, gmat)


def _conv_sample_body(hist_ref, u_ref, w_ref, b_ref, gg_ref, gb_ref, m_ref, o_ref):
    acc = b_ref[...] + w_ref[DW_WIDTH - 1:DW_WIDTH, :] * u_ref[...]
    for j in range(DW_WIDTH - 1):
        acc = acc + w_ref[j:j + 1, :] * hist_ref[j]
    o_ref[...] = _group_norm_silu(acc, m_ref[...], gg_ref[...], gb_ref[...]).astype(BF16)


def _conv_sample(hist_t, u, w_dw, b_row, gg_row, gb_row, gmat):
    n, c = u.shape
    full2 = lambda i: (0, 0)
    return pl.pallas_call(
        _conv_sample_body,
        grid=(1,),
        in_specs=[
            pl.BlockSpec(hist_t.shape, lambda i: (0, 0, 0)),
            pl.BlockSpec((n, c), full2),
            pl.BlockSpec((DW_WIDTH, c), full2),
            pl.BlockSpec((1, c), full2),
            pl.BlockSpec((1, c), full2),
            pl.BlockSpec((1, c), full2),
            pl.BlockSpec((c, c), full2),
        ],
        out_specs=pl.BlockSpec((n, c), full2),
        out_shape=jax.ShapeDtypeStruct((n, c), BF16),
        compiler_params=_params(("arbitrary",)),
        name="conv_sample",
    )(hist_t, u, w_dw, b_row, gg_row, gb_row, gmat)


def _cumsum_body(lf_ref, tri_ref, o_ref, carry_ref):
    @pl.when(pl.program_id(1) == 0)
    def _():
        carry_ref[...] = jnp.zeros_like(carry_ref)

    cl = lf_ref.shape[1]
    cum = _dot3(lf_ref[...], tri_ref[...]) + carry_ref[:, 0:1]
    o_ref[...] = -LOG2_E * cum
    carry_ref[...] = jnp.broadcast_to(cum[:, cl - 1:cl], carry_ref.shape)


def _neg_cumsum(lft, batch, seq):
    heads, n = lft.shape
    cl = _pick(seq, (512, 256, 128))
    nc = seq // cl
    tri = jnp.triu(jnp.ones((cl, cl), F32)).astype(BF16)
    return pl.pallas_call(
        _cumsum_body,
        grid=(batch, nc),
        in_specs=[pl.BlockSpec((heads, cl), lambda b, c: (0, b * nc + c)),
                  pl.BlockSpec((cl, cl), lambda b, c: (0, 0))],
        out_specs=pl.BlockSpec((heads, cl), lambda b, c: (0, b * nc + c)),
        out_shape=jax.ShapeDtypeStruct((heads, n), F32),
        scratch_shapes=[pltpu.VMEM((heads, LANES), F32)],
        compiler_params=_params(("parallel", "arbitrary")),
        name="forget_cumsum",
    )(lft, tri)


def _att_body(qi_ref, ki_ref, q_ref, k_ref, v_ref, nd_ref, o_ref, m_ref, l_ref, acc_ref, *, hd, pairs):
    step = pl.program_id(2)
    qi = qi_ref[step]
    ki = ki_ref[step]
    tq = q_ref.shape[0]
    tk = k_ref.shape[2]
    is_a = lax.broadcasted_iota(I32, (1, LANES), 1) < hd

    @pl.when(ki == 0)
    def _():
        m_ref[...] = jnp.full(m_ref.shape, NEG_INF, F32)
        l_ref[...] = jnp.zeros_like(l_ref)
        acc_ref[...] = jnp.zeros_like(acc_ref)

    def process(diagonal):
        if diagonal:
            causal = (lax.broadcasted_iota(I32, (tq, tk), 1) <= lax.broadcasted_iota(I32, (tq, tk), 0))
        for g in range(pairs):
            lanes = slice(g * LANES, (g + 1) * LANES)
            q = q_ref[:, lanes]
            kt = k_ref[0, lanes, :]
            vt = v_ref[0, lanes, :]
            nd = nd_ref[g]
            zero = jnp.zeros_like(q)
            pvs = []
            alphas = []
            for h in range(2):
                qh = jnp.where(is_a, q, zero) if h == 0 else jnp.where(is_a, zero, q)
                s = _dot(qh, kt) + nd[h:h + 1, :]
                if diagonal:
                    s = jnp.where(causal, s, NEG_INF)
                m_old = m_ref[2 * g + h]
                m_new = jnp.maximum(m_old, jnp.max(s, axis=1, keepdims=True))
                alpha = jnp.exp2(m_old - m_new)
                p = jnp.exp2(s - jnp.tile(m_new, (1, tk // LANES)))
                l_ref[2 * g + h] = alpha * l_ref[2 * g + h] + jnp.sum(p, axis=1, keepdims=True)
                m_ref[2 * g + h] = m_new
                pvs.append(_dot_nt(p.astype(BF16), vt))
                alphas.append(alpha)
            acc_ref[:, lanes] = (acc_ref[:, lanes] * jnp.where(is_a, alphas[0], alphas[1])
                                 + jnp.where(is_a, pvs[0], pvs[1]))

    @pl.when(ki < qi)
    def _():
        process(False)

    @pl.when(ki == qi)
    def _():
        process(True)
        for g in range(pairs):
            lanes = slice(g * LANES, (g + 1) * LANES)
            o_ref[:, lanes] = (acc_ref[:, lanes]
                               / jnp.where(is_a, l_ref[2 * g], l_ref[2 * g + 1])).astype(BF16)


def _attention_prompt(q_b, kt_b, vt_b, negd, batch, seq):
    n, aw = q_b.shape
    pairs = ATT_PAIRS_PER_STEP
    groups = aw // (pairs * LANES)
    w = pairs * LANES
    t = _pick(seq, (512, 256, 128))
    nq = seq // t
    steps = [(a, b) for a in range(nq) for b in range(a + 1)]
    qi_tab = jnp.asarray([s[0] for s in steps], I32)
    ki_tab = jnp.asarray([s[1] for s in steps], I32)
    grid_spec = pltpu.PrefetchScalarGridSpec(
        num_scalar_prefetch=2,
        grid=(batch, groups, len(steps)),
        in_specs=[
            pl.BlockSpec((t, w), lambda b, g, s, qt, kt: (b * nq + qt[s], g)),
            pl.BlockSpec((1, w, t), lambda b, g, s, qt, kt: (b, g, kt[s])),
            pl.BlockSpec((1, w, t), lambda b, g, s, qt, kt: (b, g, kt[s])),
            pl.BlockSpec((pairs, 2, t), lambda b, g, s, qt, kt: (g, 0, b * nq + kt[s])),
        ],
        out_specs=pl.BlockSpec((t, w), lambda b, g, s, qt, kt: (b * nq + qt[s], g)),
        scratch_shapes=[pltpu.VMEM((2 * pairs, t, LANES), F32), pltpu.VMEM((2 * pairs, t, LANES), F32),
                        pltpu.VMEM((t, w), F32)],
    )
    return pl.pallas_call(
        functools.partial(_att_body, hd=HEAD_DIM, pairs=pairs),
        grid_spec=grid_spec,
        out_shape=jax.ShapeDtypeStruct((n, aw), BF16),
        compiler_params=_params(("parallel", "parallel", "arbitrary")),
        name="attention_prompt",
    )(qi_tab, ki_tab, q_b, kt_b, vt_b, negd)


def _pool_suffix_body(lf_ref, tri_ref, ones_ref, w_ref, t_ref):
    x = lf_ref[...]
    w_ref[...] = _dot3(x, tri_ref[...])
    t_ref[...] = _dot3(x, ones_ref[...])


def _pool_suffix(lf_rows):
    n, page = lf_rows.shape
    tp = _pick(n, (2048, 1024, 512, 256, 128, 64, 32, 16, 8, n))
    tri = jnp.tril(jnp.ones((page, page), F32), -1).astype(BF16)
    ones = jnp.ones((page, page), BF16)
    blk = pl.BlockSpec((tp, page), lambda i: (i, 0))
    mat = pl.BlockSpec((page, page), lambda i: (0, 0))
    return pl.pallas_call(
        _pool_suffix_body,
        grid=(n // tp,),
        in_specs=[blk, mat, mat],
        out_specs=(blk, blk),
        out_shape=(jax.ShapeDtypeStruct((n, page), F32), jax.ShapeDtypeStruct((n, page), F32)),
        compiler_params=_params(("parallel",)),
        name="pool_forget_suffix",
    )(lf_rows, tri, ones)


def _satt_body(pt_ref, *refs, n_pages, heads):
    k_refs = refs[:n_pages]
    v_refs = refs[n_pages:2 * n_pages]
    w_refs = refs[2 * n_pages:3 * n_pages]
    t_refs = refs[3 * n_pages:4 * n_pages]
    q_ref, kn_ref, vn_ref, lfn_ref, o_ref, s_ref = refs[4 * n_pages:]
    hd, page = k_refs[0].shape[2:]
    q = q_ref[0].astype(F32)
    s_new = jnp.sum(q * kn_ref[0], axis=1, keepdims=True)
    q_cols = [jnp.broadcast_to(q[h:h + 1, :], (page, hd)).T for h in range(heads)]
    later = lfn_ref[0]
    m = s_new
    for j in reversed(range(n_pages)):
        for h in range(heads):
            s_ref[j, h:h + 1, :] = jnp.sum(k_refs[j][0, h] * q_cols[h], axis=0, keepdims=True)
        s = s_ref[j] + (w_refs[j][0] + later)
        s_ref[j] = s
        m = jnp.maximum(m, jnp.max(s, axis=1, keepdims=True))
        later = later + t_refs[j][0]
    p_new = jnp.exp(s_new - m)
    l = p_new
    for j in range(n_pages):
        p = jnp.exp(s_ref[j] - m)
        s_ref[j] = p
        l = l + jnp.sum(p, axis=1, keepdims=True)
    vn = vn_ref[0]
    for h in range(heads):
        acc = jnp.zeros((hd, page), F32)
        for j in range(n_pages):
            acc = acc + v_refs[j][0, h] * s_ref[j, h:h + 1, :]
        o = jnp.sum(acc.T, axis=0, keepdims=True) + p_new[h:h + 1, :] * vn[h:h + 1, :]
        o_ref[0, h:h + 1, :] = (o / l[h:h + 1, :]).astype(BF16)


def _attention_sample(q_b, k_new, v_new, lf_new, cache_kt, cache_vt, cache_lft, page_table):
    n, aw = q_b.shape
    n_pages = page_table.shape[1]
    pool, heads, hd, page = cache_kt.shape
    within, totals = _pool_suffix(cache_lft.reshape(pool * heads, page))
    within = within.reshape(pool, heads, page)
    totals = totals.reshape(pool, heads, page)
    lfn_lanes = jnp.broadcast_to(lf_new[:, :, None], (n, heads, page))

    def kv_spec(j):
        return pl.BlockSpec((1, heads, hd, page), lambda i, pt, j=j: (pt[i, j], 0, 0, 0))

    def row_spec(j):
        return pl.BlockSpec((1, heads, page), lambda i, pt, j=j: (pt[i, j], 0, 0))

    per_seq = lambda shape: pl.BlockSpec((1,) + shape, lambda i, pt: (i, 0, 0))
    grid_spec = pltpu.PrefetchScalarGridSpec(
        num_scalar_prefetch=1,
        grid=(n,),
        in_specs=([kv_spec(j) for j in range(n_pages)] + [kv_spec(j) for j in range(n_pages)]
                  + [row_spec(j) for j in range(n_pages)] + [row_spec(j) for j in range(n_pages)]
                  + [per_seq((heads, hd)), per_seq((heads, hd)), per_seq((heads, hd)), per_seq((heads, page))]),
        out_specs=per_seq((heads, hd)),
        scratch_shapes=[pltpu.VMEM((n_pages, heads, page), F32)],
    )
    out = pl.pallas_call(
        functools.partial(_satt_body, n_pages=n_pages, heads=heads),
        grid_spec=grid_spec,
        out_shape=jax.ShapeDtypeStruct((n, heads, hd), BF16),
        compiler_params=_params(("arbitrary",)),
        name="attention_sample",
    )(page_table, *([cache_kt] * n_pages), *([cache_vt] * n_pages), *([within] * n_pages), *([totals] * n_pages),
      q_b.reshape(n, heads, hd), k_new.reshape(n, heads, hd), v_new.reshape(n, heads, hd), lfn_lanes)
    return out.reshape(n, aw)


def _mix_body(x_ref, c_ref, a_ref, p_ref, wo_ref, g1_ref, b1_ref, wr_ref, br_ref,
              wpg_ref, bpg_ref, wpp_ref, h_ref, base_ref, idx_ref, gate_ref, cnt_ref, *, alpha, cc):
    mix = _dot(c_ref[...], wo_ref[:cc, :]) + _dot(a_ref[...], wo_ref[cc:, :])
    h = _layer_norm(alpha * x_ref[...] + mix, g1_ref[...], b1_ref[...])
    h_ref[...] = h
    hb = h.astype(BF16)
    hl = (h - hb.astype(F32)).astype(BF16)
    both = _dot(hb, wr_ref[...])
    logits = both[:, :LANES] + both[:, LANES:] + _dot(hl, wr_ref[:, :LANES]) + br_ref[...]

    tm = logits.shape[0]
    lane = lax.broadcasted_iota(I32, (tm, LANES), 1).astype(F32)
    vals, idxs = [], []
    l = logits
    for _ in range(TOP_K):
        m = jnp.max(l, axis=1, keepdims=True)
        ix = jnp.min(jnp.where(l == m, lane, float(LANES)), axis=1, keepdims=True)
        vals.append(m)
        idxs.append(ix)
        l = jnp.where(lane == ix, -jnp.inf, l)
    es = [jnp.exp(v - vals[0]) for v in vals]
    den = es[0] + es[1] + es[2] + es[3]
    idx_w = jnp.zeros((tm, LANES), F32)
    gate_w = jnp.zeros((tm, LANES), F32)
    sel = jnp.zeros((tm, LANES), F32)
    for r in range(TOP_K):
        idx_w = jnp.where(lane == float(r), idxs[r], idx_w)
        gate_w = jnp.where(lane == float(r), es[r] / den, gate_w)
        sel = sel + (lane == idxs[r]).astype(F32)
    idx_ref[...] = idx_w
    gate_ref[...] = gate_w

    @pl.when(pl.program_id(0) == 0)
    def _():
        cnt_ref[...] = jnp.zeros_like(cnt_ref)

    cnt_ref[...] += jnp.sum(sel, axis=0, keepdims=True)

    ple = _sigmoid(_dot(hb, wpg_ref[...]) + bpg_ref[...]) * _dot(p_ref[...].astype(BF16), wpp_ref[...])
    base_ref[...] = alpha * h + ple


def _mix(x, conv_b, att_b, p, wo_b, g1, b1, wr_pair, br_row, wpg_b, bpg, wpp_b, alpha):
    n, d = x.shape
    cc = conv_b.shape[1]
    aw = att_b.shape[1]
    pd = p.shape[1]
    tm = _pick(n, (512, 256, 128, 64, 32, 16, 8))
    row = lambda i: (i, 0)
    const = lambda i: (0, 0)
    outs = (
        jax.ShapeDtypeStruct((n, d), F32),
        jax.ShapeDtypeStruct((n, d), F32),
        jax.ShapeDtypeStruct((n, LANES), F32),
        jax.ShapeDtypeStruct((n, LANES), F32),
        jax.ShapeDtypeStruct((1, LANES), F32),
    )
    return pl.pallas_call(
        functools.partial(_mix_body, alpha=alpha, cc=cc),
        grid=(n // tm,),
        in_specs=[
            pl.BlockSpec((tm, d), row), pl.BlockSpec((tm, cc), row), pl.BlockSpec((tm, aw), row),
            pl.BlockSpec((tm, pd), row),
            pl.BlockSpec((d, d), const), pl.BlockSpec((1, d), const), pl.BlockSpec((1, d), const),
            pl.BlockSpec((d, 2 * LANES), const), pl.BlockSpec((1, LANES), const),
            pl.BlockSpec((d, d), const), pl.BlockSpec((1, d), const), pl.BlockSpec((pd, d), const),
        ],
        out_specs=(
            pl.BlockSpec((tm, d), row), pl.BlockSpec((tm, d), row),
            pl.BlockSpec((tm, LANES), row), pl.BlockSpec((tm, LANES), row),
            pl.BlockSpec((1, LANES), const),
        ),
        out_shape=outs,
        compiler_params=_params(("arbitrary",)),
        name="out_proj_router",
    )(x, conv_b, att_b, p, wo_b, g1, b1, wr_pair, br_row, wpg_b, bpg, wpp_b)


def _positions_body(idx_ref, start_ref, tri_ref, o_ref, carry_ref):
    @pl.when(pl.program_id(0) == 0)
    def _():
        carry_ref[...] = jnp.zeros_like(carry_ref)

    ids = idx_ref[...]
    tm = ids.shape[0]
    lane = lax.broadcasted_iota(I32, (tm, LANES), 1).astype(F32)
    hot = [lane == ids[:, r:r + 1] for r in range(TOP_K)]
    sel = jnp.zeros((tm, LANES), F32)
    for r in range(TOP_K):
        sel = sel + hot[r].astype(F32)
    rank = _dot(tri_ref[...], sel.astype(BF16)) + carry_ref[...]
    pos = rank + start_ref[...]
    out = jnp.zeros((tm, LANES), F32)
    for r in range(TOP_K):
        d = jnp.sum(jnp.where(hot[r], pos, 0.0), axis=1, keepdims=True)
        out = jnp.where(lane == float(r), d, out)
    o_ref[...] = out.astype(I32)
    carry_ref[...] += jnp.sum(sel, axis=0, keepdims=True)


def _positions(idx_w, start_row):
    n = idx_w.shape[0]
    tm = _pick(n, (1376, 1024, 512, 384, 256, 128, 64, 32, 16, 8))
    tri = jnp.tril(jnp.ones((tm, tm), F32), -1).astype(BF16)
    return pl.pallas_call(
        _positions_body,
        grid=(n // tm,),
        in_specs=[pl.BlockSpec((tm, LANES), lambda i: (i, 0)),
                  pl.BlockSpec((1, LANES), lambda i: (0, 0)),
                  pl.BlockSpec((tm, tm), lambda i: (0, 0))],
        out_specs=pl.BlockSpec((tm, LANES), lambda i: (i, 0)),
        out_shape=jax.ShapeDtypeStruct((n, LANES), I32),
        scratch_shapes=[pltpu.VMEM((1, LANES), F32)],
        compiler_params=_params(("arbitrary",)),
        name="route_positions",
    )(idx_w, start_row, tri)


def _dispatch_body(dest_ref, code_ref, h_ref, xs_in_ref, inv_src_ref, xs_ref, inv_ref, sem, inv_sem, *, tb):
    del xs_in_ref

    @pl.when(pl.program_id(0) == 0)
    def _():
        cp = pltpu.make_async_copy(inv_src_ref, inv_ref, inv_sem)
        cp.start()
        cp.wait()

    def issue(t, c):
        for r in range(TOP_K):
            j = t * TOP_K + r
            d = dest_ref[0, 0, j]
            inv_ref[d] = code_ref[0, 0, j]
            pltpu.make_async_copy(h_ref.at[pl.ds(t, 1)], xs_ref.at[pl.ds(d, 1)], sem).start(priority=r % 2)
        return c

    lax.fori_loop(0, tb, issue, 0)
    for r in range(TOP_K):
        pltpu.make_async_copy(h_ref, xs_ref.at[pl.ds(0, tb)], sem).wait()


def _dispatch(h, dest, xs, inv, tok0, n_tok):
    n, d = h.shape
    tb = _pick(n, (1024, 512, 256, 128, 64, 32, 16, 8))
    dest3 = dest.reshape(n // tb, 1, tb * TOP_K)
    code3 = (jnp.arange(TOP_K, dtype=I32)[None, :] * n_tok
             + (tok0 + jnp.arange(n, dtype=I32))[:, None]).reshape(n // tb, 1, tb * TOP_K)
    smem_blk = pl.BlockSpec((1, 1, tb * TOP_K), lambda i: (i, 0, 0), memory_space=pltpu.SMEM)
    return pl.pallas_call(
        functools.partial(_dispatch_body, tb=tb),
        grid=(n // tb,),
        in_specs=[
            smem_blk, smem_blk,
            pl.BlockSpec((tb, d), lambda i: (i, 0)),
            pl.BlockSpec(memory_space=pl.ANY),
            pl.BlockSpec(memory_space=pl.ANY),
        ],
        out_specs=(pl.BlockSpec(memory_space=pl.ANY), pl.BlockSpec(memory_space=pltpu.SMEM)),
        out_shape=(jax.ShapeDtypeStruct(xs.shape, xs.dtype), jax.ShapeDtypeStruct(inv.shape, inv.dtype)),
        scratch_shapes=[pltpu.SemaphoreType.DMA, pltpu.SemaphoreType.DMA],
        input_output_aliases={3: 0},
        compiler_params=_params(("arbitrary",)),
        name="moe_dispatch",
    )(dest3, code3, h, xs, inv)


def _expert_body(be_ref, nu_ref, inv_ref, x_ref, wg_ref, bg_ref, wu_ref, bu_ref, wd_ref, bd_ref, out_ref,
                 wgb_ref, wub_ref, wdb_ref, ya_ref, yb_ref, sem_a, sem_b, sem_z, *, n_slots):
    i = pl.program_id(0)
    half = MOE_BLOCK // 2

    def scatter(y_ref, row0, sem):
        for k in range(half):
            slot = inv_ref[row0 + k]
            pltpu.make_async_copy(y_ref.at[pl.ds(k, 1)], out_ref.at[pl.ds(slot, 1)], sem).start(priority=k % 2)

    def drain(y_ref, sem):
        pltpu.make_async_copy(y_ref, out_ref.at[pl.ds(0, half)], sem).wait()

    def ffn(rows):
        x = x_ref[rows, :].astype(BF16)
        g = jnp.minimum(_dot(x, wgb_ref[...]) + bg_ref[0], SWIGLU_LIMIT)
        u = jnp.clip(_dot(x, wub_ref[...]) + bu_ref[0], -SWIGLU_LIMIT, SWIGLU_LIMIT)
        hidden = (u + 1.0) * (g * _sigmoid(SWIGLU_ALPHA * g))
        return _dot(hidden.astype(BF16), wdb_ref[...]) + bd_ref[0]

    @pl.when(i == 0)
    def _():
        ya_ref[...] = jnp.zeros_like(ya_ref)
        yb_ref[...] = jnp.zeros_like(yb_ref)
        pltpu.make_async_copy(ya_ref, out_ref.at[pl.ds(n_slots, half)], sem_a).start()
        cp = pltpu.make_async_copy(yb_ref, out_ref.at[pl.ds(n_slots + half, half)], sem_z)
        cp.start()
        cp.wait()

    @pl.when(i < nu_ref[0])
    def _():
        @pl.when((i == 0) | (be_ref[i] != be_ref[jnp.maximum(i - 1, 0)]))
        def _():
            wgb_ref[...] = wg_ref[0].astype(BF16)
            wub_ref[...] = wu_ref[0].astype(BF16)
            wdb_ref[...] = wd_ref[0].astype(BF16)

        drain(ya_ref, sem_a)
        scatter(yb_ref, i * MOE_BLOCK + half, sem_b)
        ya_ref[...] = ffn(slice(0, half))
        drain(yb_ref, sem_b)
        scatter(ya_ref, (i + 1) * MOE_BLOCK, sem_a)
        yb_ref[...] = ffn(slice(half, MOE_BLOCK))

    @pl.when(i == nu_ref[0])
    def _():
        drain(ya_ref, sem_a)
        scatter(yb_ref, i * MOE_BLOCK + half, sem_b)
        drain(yb_ref, sem_b)


def _experts(xs, block_e, n_used, inv_shifted, n_slots, w_gate, b_gate, w_up, b_up, w_down, b_down):
    rows, d = xs.shape
    n_e, _, de = w_gate.shape
    n_blocks = rows // MOE_BLOCK
    half = MOE_BLOCK // 2
    last = lambda i, be, nu, inv: (jnp.minimum(i, nu[0] - 1), 0)
    wsel = lambda i, be, nu, inv: (be[i], 0, 0)
    grid_spec = pltpu.PrefetchScalarGridSpec(
        num_scalar_prefetch=3,
        grid=(n_blocks,),
        in_specs=[
            pl.BlockSpec((MOE_BLOCK, d), last),
            pl.BlockSpec((1, d, de), wsel), pl.BlockSpec((1, 1, de), wsel),
            pl.BlockSpec((1, d, de), wsel), pl.BlockSpec((1, 1, de), wsel),
            pl.BlockSpec((1, de, d), wsel), pl.BlockSpec((1, 1, d), wsel),
        ],
        out_specs=pl.BlockSpec(memory_space=pl.ANY),
        scratch_shapes=[pltpu.VMEM((d, de), BF16), pltpu.VMEM((d, de), BF16), pltpu.VMEM((de, d), BF16),
                        pltpu.VMEM((half, d), F32), pltpu.VMEM((half, d), F32),
                        pltpu.SemaphoreType.DMA, pltpu.SemaphoreType.DMA, pltpu.SemaphoreType.DMA],
    )
    return pl.pallas_call(
        functools.partial(_expert_body, n_slots=n_slots),
        grid_spec=grid_spec,
        out_shape=jax.ShapeDtypeStruct((n_slots + MOE_BLOCK, d), F32),
        compiler_params=_params(("arbitrary",)),
        name="moe_experts",
    )(block_e, n_used, inv_shifted, xs, w_gate, b_gate[:, None, :], w_up, b_up[:, None, :],
      w_down, b_down[:, None, :])


def _combine_body(base_ref, gate_ref, *refs):
    y_refs = refs[:TOP_K]
    g2_ref, b2_ref, o_ref = refs[TOP_K:]
    gate = gate_ref[...]
    y = base_ref[...]
    for r in range(TOP_K):
        y = y + gate[:, r:r + 1] * y_refs[r][...]
    o_ref[...] = _layer_norm(y, g2_ref[...], b2_ref[...])


def _combine(base, gate_w, slots, tok0, n_tok, g2, b2):
    n, d = base.shape
    tb = _pick(math.gcd(n, math.gcd(tok0, n_tok)), (512, 256, 128, 64, 32, 16, 8))
    row = lambda i: (i, 0)
    const = lambda i: (0, 0)

    def slot_spec(r):
        return pl.BlockSpec((tb, d), lambda i, r=r: ((r * n_tok + tok0) // tb + i, 0))

    return pl.pallas_call(
        _combine_body,
        grid=(n // tb,),
        in_specs=([pl.BlockSpec((tb, d), row), pl.BlockSpec((tb, LANES), row)]
                  + [slot_spec(r) for r in range(TOP_K)]
                  + [pl.BlockSpec((1, d), const), pl.BlockSpec((1, d), const)]),
        out_specs=pl.BlockSpec((tb, d), row),
        out_shape=jax.ShapeDtypeStruct((n, d), F32),
        compiler_params=_params(("parallel",)),
        name="moe_combine",
    )(base, gate_w, *([slots] * TOP_K), g2, b2)


def _layer(x_p, x_s, ck, cv, clf, hist_s, page_table, p_p, p_s, lw, alpha):
    batch, seq, d = x_p.shape
    n_s = x_s.shape[0]
    n_p = batch * seq
    cc = lw["w_dw"].shape[1]
    aw = ATT_HEADS * HEAD_DIM
    heads = ATT_HEADS
    nz = 2 * cc + 3 * aw
    row = lambda v: v[None, :]

    nq_cols = 2 * cc + aw
    w_in_t = lw["w_in"].T
    w_in_b = lw["w_in"][:, :nq_cols].astype(BF16)
    b_in_row = row(lw["b_in"][:nq_cols])
    wkv_t = w_in_t[nq_cols:nz].astype(BF16)
    bkv = lw["b_in"][nq_cols:nz]
    wft_b = w_in_t[nz:].astype(BF16)
    bft_col = lw["b_in"][nz:, None]
    b_dw, gg, gb = row(lw["b_dw"]), row(lw["gn_gain"]), row(lw["gn_bias"])
    gsz = cc // CONV_GROUPS
    grp = jnp.arange(cc) // gsz
    gmat = ((grp[:, None] == grp[None, :]).astype(F32) / gsz).astype(BF16)
    wo_b = lw["w_out"].astype(BF16)
    wr = jnp.pad(lw["w_router"], ((0, 0), (0, LANES - N_EXPERTS)))
    wr_hi = wr.astype(BF16)
    wr_lo = (wr - wr_hi.astype(F32)).astype(BF16)
    wr_pair = jnp.concatenate([wr_hi, wr_lo], axis=1)
    br_row = row(jnp.pad(lw["b_router"], (0, LANES - N_EXPERTS), constant_values=NEG_INF))
    wpg_b = lw["w_ple_gate"].astype(BF16)
    wpp_b = lw["w_ple_proj"].astype(BF16)

    xp2 = x_p.reshape(n_p, d)
    xs2 = x_s.reshape(n_s, d)

    u_p, q_p, kt_p, vt_p, ktb_p, vtb_p, lft_p = _in_proj(xp2, w_in_b, b_in_row, wkv_t, bkv[:, None], wft_b, bft_col,
                                                         cc, aw, HEAD_DIM ** -0.5 * LOG2_E, batch=batch)
    conv_p = _conv_prompt(u_p, batch, seq, lw["w_dw"], b_dw, gg, gb, gmat)
    negd = _neg_cumsum(lft_p, batch, seq).reshape(heads // 2, 2, n_p)
    att_p = _attention_prompt(q_p, ktb_p, vtb_p, negd, batch, seq)
    h_p, base_p, idx_p, gate_p, cnt_p = _mix(xp2, conv_p, att_p, p_p.reshape(n_p, -1), wo_b,
                                             row(lw["ln1_gain"]), row(lw["ln1_bias"]), wr_pair, br_row,
                                             wpg_b, row(lw["b_ple_gate"]), wpp_b, alpha)

    u_s, q_s, k_s, v_s, lft_s = _in_proj(xs2, w_in_b, b_in_row, wkv_t, bkv[None, :], wft_b, bft_col,
                                         cc, aw, HEAD_DIM ** -0.5)
    conv_s = _conv_sample(hist_s.transpose(1, 0, 2), u_s, lw["w_dw"], b_dw, gg, gb, gmat)
    lf_s = lft_s.T
    att_s = _attention_sample(q_s, k_s, v_s, lf_s, ck.transpose(0, 2, 3, 1), cv.transpose(0, 2, 3, 1),
                              clf.transpose(0, 2, 1), page_table)
    h_s, base_s, idx_s, gate_s, cnt_s = _mix(xs2, conv_s, att_s, p_s.reshape(n_s, -1), wo_b,
                                             row(lw["ln1_gain"]), row(lw["ln1_bias"]), wr_pair, br_row,
                                             wpg_b, row(lw["b_ple_gate"]), wpp_b, alpha)

    n_tok = n_p + n_s
    counts = (cnt_p + cnt_s)[0].astype(I32)
    padded = (counts + MOE_BLOCK - 1) // MOE_BLOCK * MOE_BLOCK
    pad_end = jnp.cumsum(padded)
    pad_start = pad_end - padded
    n_blocks = -(-(n_tok * TOP_K) // MOE_BLOCK) + N_EXPERTS + 1
    n_used = (pad_end[N_EXPERTS - 1] // MOE_BLOCK).astype(I32).reshape(1)
    block_row0 = jnp.arange(n_blocks, dtype=I32) * MOE_BLOCK
    block_e = jnp.minimum(jnp.sum((pad_end[None, :N_EXPERTS] <= block_row0[:, None]).astype(I32), axis=1),
                          N_EXPERTS - 1)
    dest = _positions(jnp.concatenate([idx_p, idx_s], axis=0), pad_start.astype(F32)[None, :])[:, :TOP_K]
    dest_p, dest_s = dest[:n_p], dest[n_p:]

    n_rows = n_blocks * MOE_BLOCK
    n_slots = TOP_K * n_tok
    inv = n_slots + jnp.arange(n_rows, dtype=I32) % MOE_BLOCK
    xs_rows = jnp.zeros((n_rows, d), F32)
    xs_rows, inv = _dispatch(h_p, dest_p, xs_rows, inv, 0, n_tok)
    xs_rows, inv = _dispatch(h_s, dest_s, xs_rows, inv, n_p, n_tok)
    inv_shifted = jnp.concatenate([n_slots + jnp.arange(MOE_BLOCK, dtype=I32), inv])
    slots = _experts(xs_rows, block_e, n_used, inv_shifted, n_slots, lw["w_gate"], lw["b_gate"],
                     lw["w_up"], lw["b_up"], lw["w_down"], lw["b_down"])
    g2, b2 = row(lw["ln2_gain"]), row(lw["ln2_bias"])
    y_p = _combine(base_p, gate_p, slots, 0, n_tok, g2, b2).reshape(batch, seq, d)
    y_s = _combine(base_s, gate_s, slots, n_p, n_tok, g2, b2).reshape(n_s, 1, d)

    lf_p = lft_p.T.reshape(batch, seq, heads)
    hist_p = u_p.reshape(batch, seq, cc)[:, seq - (DW_WIDTH - 1):]
    hist_new_s = jnp.concatenate([hist_s[:, 1:], u_s[:, None, :]], axis=1)
    return (y_p, y_s,
            kt_p.reshape(batch, heads, HEAD_DIM, seq).transpose(0, 3, 1, 2),
            vt_p.reshape(batch, heads, HEAD_DIM, seq).transpose(0, 3, 1, 2), lf_p, hist_p,
            k_s.reshape(n_s, 1, heads, HEAD_DIM), v_s.reshape(n_s, 1, heads, HEAD_DIM),
            lf_s.reshape(n_s, 1, heads), hist_new_s)


def kernel(x_prompt, x_sample, cache_k, cache_v, cache_logf, state_conv, page_table, p_prompt, p_sample,
           w_in, b_in, w_dw, b_dw, gn_gain, gn_bias, w_out, ln1_gain, ln1_bias,
           w_router, b_router, w_gate, b_gate, w_up, b_up, w_down, b_down,
           w_ple_gate, b_ple_gate, w_ple_proj, ln2_gain, ln2_bias):
    depth = w_in.shape[0]
    if depth != 1 or x_sample.shape[1] != 1:
        raise NotImplementedError("one layer and one new token per sample sequence are supported")
    alpha = (2.0 * depth) ** 0.25
    lw = dict(w_in=w_in[0], b_in=b_in[0], w_dw=w_dw[0], b_dw=b_dw[0], gn_gain=gn_gain[0], gn_bias=gn_bias[0],
              w_out=w_out[0], ln1_gain=ln1_gain[0], ln1_bias=ln1_bias[0],
              w_router=w_router[0], b_router=b_router[0], w_gate=w_gate[0], b_gate=b_gate[0],
              w_up=w_up[0], b_up=b_up[0], w_down=w_down[0], b_down=b_down[0],
              w_ple_gate=w_ple_gate[0], b_ple_gate=b_ple_gate[0], w_ple_proj=w_ple_proj[0],
              ln2_gain=ln2_gain[0], ln2_bias=ln2_bias[0])
    outs = _layer(x_prompt, x_sample[:, 0], cache_k[0], cache_v[0], cache_logf[0], state_conv[0], page_table,
                  p_prompt[0], p_sample[0], lw, alpha)
    (y_p, y_s, k_p, v_p, lf_p, c_p, k_s, v_s, lf_s, c_s) = outs
    return (y_p, y_s, k_p[None], v_p[None], lf_p[None], c_p[None], k_s[None], v_s[None], lf_s[None], c_s[None])
```
